```python
import math
import jax
import jax.numpy as jnp
from jax import lax
import numpy as np

D_MODEL = 1024
BATCH = 8
SEQ = 2048
DEPTH = 2

GRID_W = 64
CTX_LEN = 256
A_HEADS = 4
A_HEAD_DIM = 64
A_WIDTH = 2 * A_HEADS * A_HEAD_DIM
ROPE_THETA = 10000.0
Q_BLOCK = 128
B_WIDTH = D_MODEL // 2
B_CONV = 3
C_WIDTH = D_MODEL
C_BLOCKS = 4
C_BLOCK = C_WIDTH // C_BLOCKS
C_CONV = 4
LRU_C = 8.0
N_EXPERTS = 32
TOP_K = 4
EXPERT_FF = D_MODEL
SWIGLU_ALPHA = 1.702
SWIGLU_LIMIT = 7.0
NORM_EPS = 1e-6
N_EVEN = (DEPTH + 1) // 2
N_ODD = DEPTH // 2
EVEN_IN = 3 * A_WIDTH + 3 * B_WIDTH
ODD_IN = 2 * C_WIDTH

kernel_name = "hybrid_diffattn_shortconv_rglru_moe_prefix_trunk"


def rmsnorm(x, g):
    xf = x.astype(jnp.float32)
    y = xf * lax.rsqrt(jnp.mean(xf * xf, axis=-1, keepdims=True) + NORM_EPS)
    return (y * g.astype(jnp.float32)).astype(x.dtype)


def modulate(h, shift, scale):
    return h * (1 + scale) + shift


def adaln(cvec, w_mod, b_mod, n_chunks):
    d = cvec.shape[-1]
    m = jax.nn.silu(cvec) @ w_mod[:, : n_chunks * d] + b_mod[: n_chunks * d]
    return jnp.split(m, n_chunks, axis=-1)


def depthwise_conv(x, w, pad):
    return lax.conv_general_dilated(
        x, w[:, None, :].astype(x.dtype), window_strides=(1,), padding=[pad],
        dimension_numbers=("NWC", "WIO", "NWC"), feature_group_count=x.shape[-1])


def axial_rope_tables(n_tokens, dtype):
    n_rows = n_tokens // GRID_W
    rows, cols = jnp.meshgrid(jnp.arange(n_rows), jnp.arange(GRID_W), indexing="ij")
    pos = jnp.stack([rows.reshape(-1), cols.reshape(-1)], axis=-1).astype(jnp.float32)
    n_freq = A_HEAD_DIM // 4
    inv = ROPE_THETA ** (-jnp.arange(n_freq, dtype=jnp.float32) / n_freq)
    ang = pos[:, :, None] * inv
    return jnp.cos(ang).astype(dtype), jnp.sin(ang).astype(dtype)


def apply_axial_rope(x, cos, sin):
    xs = x.reshape(x.shape[:-1] + (2, 2, A_HEAD_DIM // 4))
    x1, x2 = xs[..., 0, :], xs[..., 1, :]
    out = jnp.stack([x1 * cos - x2 * sin, x2 * cos + x1 * sin], axis=-2)
    return out.reshape(x.shape)


def qk_heads(t):
    b, l, _ = t.shape
    return t.reshape(b, l, A_HEADS, 2, A_HEAD_DIM).transpose(0, 2, 3, 1, 4)


def v_heads(t):
    b, l, _ = t.shape
    return t.reshape(b, l, A_HEADS, 2 * A_HEAD_DIM).transpose(0, 2, 1, 3)


def diff_attend(q, k, v, lam):
    s = jnp.einsum("bhmqd,bhmkd->bhmqk", q, k, preferred_element_type=jnp.float32) * (A_HEAD_DIM ** -0.5)
    p = jax.nn.softmax(s, axis=-1)
    w = p[:, :, 0] - lam * p[:, :, 1]
    return jnp.einsum("bhqk,bhkv->bhqv", w.astype(v.dtype), v)


def merge_heads(o, subln_g, lam_init):
    b, h, l, dv = o.shape
    o = rmsnorm(o, subln_g) * (1.0 - lam_init)
    return o.transpose(0, 2, 1, 3).reshape(b, l, h * dv)


def even_mixer(h_ctx, h_lat, w_in, w_out, lq1, lk1, lq2, lk2, subln_g, conv_w, lam_init, cos, sin, need_ctx):
    f32 = jnp.float32
    lam = (jnp.exp(jnp.sum(lq1.astype(f32) * lk1.astype(f32)))
           - jnp.exp(jnp.sum(lq2.astype(f32) * lk2.astype(f32))) + lam_init)
    splits = (A_WIDTH, 2 * A_WIDTH, 3 * A_WIDTH, 3 * A_WIDTH + B_WIDTH, 3 * A_WIDTH + 2 * B_WIDTH)
    q_l, k_l, v_l, bg_l, cg_l, u_l = jnp.split(h_lat @ w_in, splits, axis=-1)
    if need_ctx:
        q_c, k_c, v_c, bg_c, cg_c, u_c = jnp.split(h_ctx @ w_in, splits, axis=-1)
    else:
        k_c, v_c = jnp.split(h_ctx @ w_in[:, A_WIDTH:3 * A_WIDTH], 2, axis=-1)
    kh_c, vh_c = qk_heads(k_c), v_heads(v_c)
    q_l = apply_axial_rope(qk_heads(q_l), cos, sin)
    k_l = apply_axial_rope(qk_heads(k_l), cos, sin)
    k_all = jnp.concatenate([kh_c, k_l], axis=3)
    v_all = jnp.concatenate([vh_c, v_heads(v_l)], axis=2)
    b, h, _, s_len, d = q_l.shape
    nb = s_len // Q_BLOCK
    q_blocks = jnp.moveaxis(q_l.reshape(b, h, 2, nb, Q_BLOCK, d), 3, 0)
    o = lax.map(lambda qb: diff_attend(qb, k_all, v_all, lam), q_blocks)
    o = jnp.moveaxis(o, 0, 2).reshape(b, h, s_len, 2 * A_HEAD_DIM)
    attn_l = merge_heads(o, subln_g, lam_init)
    pad = (B_CONV // 2, B_CONV // 2)
    conv_l = bg_l * depthwise_conv(cg_l * u_l, conv_w, pad)
    y_lat = jnp.concatenate([attn_l, conv_l], axis=-1) @ w_out
    y_ctx = None
    if need_ctx:
        attn_c = merge_heads(diff_attend(qk_heads(q_c), kh_c, vh_c, lam), subln_g, lam_init)
        conv_c = bg_c * depthwise_conv(cg_c * u_c, conv_w, pad)
        y_ctx = jnp.concatenate([attn_c, conv_c], axis=-1) @ w_out
    return y_ctx, y_lat


def block_diag_linear(x, w, b):
    xb = x.reshape(x.shape[:-1] + (C_BLOCKS, C_BLOCK))
    return jnp.einsum("blgi,gio->blgo", xb, w).reshape(x.shape) + b


def linear_scan(a, u, h0, reverse, emit):
    def step(h, au):
        h = au[0] * h + au[1]
        return h, (h if emit else None)
    h_end, hs = lax.scan(step, h0, (jnp.swapaxes(a, 0, 1), jnp.swapaxes(u, 0, 1)), reverse=reverse)
    return h_end, (jnp.swapaxes(hs, 0, 1) if emit else None)


def rglru_direction(u_ctx, u_lat, conv_w, conv_b, ga_w, ga_b, gx_w, gx_b, lru_lam, reverse, need_ctx):
    f32 = jnp.float32
    pad = (0, C_CONV - 1) if reverse else (C_CONV - 1, 0)

    def gate_inputs(u):
        xc = depthwise_conv(u, conv_w, pad) + conv_b
        r = jax.nn.sigmoid(block_diag_linear(xc, ga_w, ga_b).astype(f32))
        i = jax.nn.sigmoid(block_diag_linear(xc, gx_w, gx_b).astype(f32))
        log_a = -LRU_C * r * jax.nn.softplus(-lru_lam.astype(f32))
        return jnp.exp(log_a), jnp.sqrt(-jnp.expm1(2.0 * log_a)) * (i * xc.astype(f32))

    a_c, b_c = gate_inputs(u_ctx)
    h0 = jnp.zeros((u_ctx.shape[0], C_WIDTH), f32)
    h_ctx_end, hs_c = linear_scan(a_c, b_c, h0, reverse, need_ctx)
    a_l, b_l = gate_inputs(u_lat)
    _, hs_l = linear_scan(a_l, b_l, h_ctx_end, reverse, True)
    return hs_c, hs_l


def odd_mixer(h_ctx, h_lat, w_in, w_out, conv_w, conv_b, ga_w, ga_b, gx_w, gx_b, lru_lam, need_ctx):
    gate_l, u_l = jnp.split(h_lat @ w_in, 2, axis=-1)
    if need_ctx:
        gate_c, u_c = jnp.split(h_ctx @ w_in, 2, axis=-1)
    else:
        u_c = h_ctx @ w_in[:, C_WIDTH:]
    outs = [rglru_direction(u_c, u_l, conv_w[d], conv_b[d], ga_w[d], ga_b[d], gx_w[d], gx_b[d],
                            lru_lam[d], d == 1, need_ctx) for d in range(2)]
    rec_l = outs[0][1] + outs[1][1]
    y_lat = (rec_l.astype(h_lat.dtype) * jax.nn.gelu(gate_l)) @ w_out
    y_ctx = None
    if need_ctx:
        rec_c = outs[0][0] + outs[1][0]
        y_ctx = (rec_c.astype(h_ctx.dtype) * jax.nn.gelu(gate_c)) @ w_out
    return y_ctx, y_lat


def moe_ffn(x, w_r, b_r, w1, b1, w2, b2):
    logits = (x @ w_r + b_r).astype(jnp.float32)
    top_v, top_i = lax.top_k(logits, TOP_K)
    probs = jax.nn.softmax(top_v, axis=-1)
    gates = jnp.sum(jax.nn.one_hot(top_i, N_EXPERTS, dtype=jnp.float32) * probs[..., None], axis=1)

    def expert(acc, ew):
        w1e, b1e, w2e, b2e, ge = ew
        h = x @ w1e + b1e
        glu = jnp.minimum(h[:, :EXPERT_FF], SWIGLU_LIMIT)
        lin = jnp.clip(h[:, EXPERT_FF:], -SWIGLU_LIMIT, SWIGLU_LIMIT)
        act = glu * jax.nn.sigmoid(SWIGLU_ALPHA * glu) * (lin + 1)
        return acc + ge[:, None] * (act @ w2e + b2e), None

    y, _ = lax.scan(expert, jnp.zeros_like(x), (w1, b1, w2, b2, gates.T.astype(x.dtype)))
    return y


def setup_inputs(seed: int = 0) -> dict:
    key = jax.random.key(seed)
    ks = iter(jax.random.split(key, 40))
    D, E, F = D_MODEL, N_EXPERTS, EXPERT_FF

    def nrm(shape, scale):
        return jax.random.normal(next(ks), shape, jnp.float32) * scale

    u = jax.random.uniform(next(ks), (N_ODD, 2, C_WIDTH), jnp.float32, 0.9, 0.999)
    s = u ** (1.0 / LRU_C)
    lru_lambda = jnp.log(s) - jnp.log1p(-s)
    return {
        "x": nrm((BATCH, SEQ, D), 1.0),
        "c": nrm((BATCH, D), 1.0),
        "ctx": nrm((BATCH, CTX_LEN, D), 1.0),
        "c_ctx": nrm((D,), 1.0),
        "w_mod": nrm((DEPTH, D, 6 * D), 0.5 * D ** -0.5),
        "b_mod": nrm((DEPTH, 6 * D), 0.02),
        "norm_mix": 1.0 + nrm((DEPTH, D), 0.05),
        "norm_ffn": 1.0 + nrm((DEPTH, D), 0.05),
        "ev_w_in": nrm((N_EVEN, D, EVEN_IN), D ** -0.5),
        "ev_w_out": nrm((N_EVEN, A_WIDTH + B_WIDTH, D), (A_WIDTH + B_WIDTH) ** -0.5),
        "ev_lambda_q1": nrm((N_EVEN, A_HEAD_DIM), 0.1),
        "ev_lambda_k1": nrm((N_EVEN, A_HEAD_DIM), 0.1),
        "ev_lambda_q2": nrm((N_EVEN, A_HEAD_DIM), 0.1),
        "ev_lambda_k2": nrm((N_EVEN, A_HEAD_DIM), 0.1),
        "ev_subln": 1.0 + nrm((N_EVEN, 2 * A_HEAD_DIM), 0.05),
        "ev_conv_w": nrm((N_EVEN, B_CONV, B_WIDTH), B_CONV ** -0.5),
        "od_w_in": nrm((N_ODD, D, ODD_IN), D ** -0.5),
        "od_w_out": nrm((N_ODD, C_WIDTH, D), C_WIDTH ** -0.5),
        "od_conv_w": nrm((N_ODD, 2, C_CONV, C_WIDTH), C_CONV ** -0.5),
        "od_conv_b": nrm((N_ODD, 2, C_WIDTH), 0.02),
        "od_gate_a_w": nrm((N_ODD, 2, C_BLOCKS, C_BLOCK, C_BLOCK), C_BLOCK ** -0.5),
        "od_gate_a_b": nrm((N_ODD, 2, C_WIDTH), 0.02),
        "od_gate_x_w": nrm((N_ODD, 2, C_BLOCKS, C_BLOCK, C_BLOCK), C_BLOCK ** -0.5),
        "od_gate_x_b": nrm((N_ODD, 2, C_WIDTH), 0.02),
        "od_lru_lambda": lru_lambda,
        "moe_w_router": nrm((DEPTH, D, E), D ** -0.5),
        "moe_b_router": nrm((DEPTH, E), 0.01),
        "moe_w1": nrm((DEPTH, E, D, 2 * F), D ** -0.5),
        "moe_b1": nrm((DEPTH, E, 2 * F), 0.02),
        "moe_w2": nrm((DEPTH, E, F, D), F ** -0.5),
        "moe_b2": nrm((DEPTH, E, D), 0.02),
        "final_norm": 1.0 + nrm((D,), 0.05),
    }


def reference(x, c, ctx, c_ctx, w_mod, b_mod, norm_mix, norm_ffn,
              ev_w_in, ev_w_out, ev_lambda_q1, ev_lambda_k1, ev_lambda_q2, ev_lambda_k2, ev_subln, ev_conv_w,
              od_w_in, od_w_out, od_conv_w, od_conv_b, od_gate_a_w, od_gate_a_b, od_gate_x_w, od_gate_x_b,
              od_lru_lambda, moe_w_router, moe_b_router, moe_w1, moe_b1, moe_w2, moe_b2, final_norm):
    b, s_len, d_model = x.shape
    n_ctx_tok = b * ctx.shape[1]
    cos, sin = axial_rope_tables(s_len, x.dtype)
    for i in range(DEPTH):
        last = i == DEPTH - 1
        need_ctx = not last
        sh1, sc1, g1, sh2, sc2, g2 = [m[:, None, :] for m in adaln(c, w_mod[i], b_mod[i], 6)]
        mc = adaln(c_ctx, w_mod[i], b_mod[i], 2 if last else 6)
        h_lat = modulate(rmsnorm(x, norm_mix[i]), sh1, sc1)
        h_ctx = modulate(rmsnorm(ctx, norm_mix[i]), mc[0], mc[1])
        j = i // 2
        if i % 2 == 0:
            lam_init = 0.8 - 0.6 * math.exp(-0.3 * i)
            y_ctx, y_lat = even_mixer(h_ctx, h_lat, ev_w_in[j], ev_w_out[j], ev_lambda_q1[j], ev_lambda_k1[j],
                                      ev_lambda_q2[j], ev_lambda_k2[j], ev_subln[j], ev_conv_w[j], lam_init,
                                      cos, sin, need_ctx)
        else:
            y_ctx, y_lat = odd_mixer(h_ctx, h_lat, od_w_in[j], od_w_out[j], od_conv_w[j], od_conv_b[j],
                                     od_gate_a_w[j], od_gate_a_b[j], od_gate_x_w[j], od_gate_x_b[j],
                                     od_lru_lambda[j], need_ctx)
        x = x + g1 * y_lat
        h_lat2 = modulate(rmsnorm(x, norm_ffn[i]), sh2, sc2)
        moe_p = (moe_w_router[i], moe_b_router[i], moe_w1[i], moe_b1[i], moe_w2[i], moe_b2[i])
        if last:
            x = x + g2 * moe_ffn(h_lat2.reshape(-1, d_model), *moe_p).reshape(x.shape)
        else:
            ctx = ctx + mc[2] * y_ctx
            h_ctx2 = modulate(rmsnorm(ctx, norm_ffn[i]), mc[3], mc[4])
            tok = jnp.concatenate([h_ctx2.reshape(-1, d_model), h_lat2.reshape(-1, d_model)], axis=0)
            out = moe_ffn(tok, *moe_p)
            ctx = ctx + mc[5] * out[:n_ctx_tok].reshape(ctx.shape)
            x = x + g2 * out[n_ctx_tok:].reshape(x.shape)
    return rmsnorm(x, final_norm)
```

```python
import functools
import math

import jax
import jax.numpy as jnp
from jax import lax
from jax.experimental import pallas as pl
from jax.experimental.pallas import tpu as pltpu

F32 = jnp.float32
BF16 = jnp.bfloat16
I32 = jnp.int32

NORM_EPS = 1e-6
ROPE_THETA = 10000.0
GRID_W = 64
N_HEADS = 4
HEAD_DIM = 64
A_WIDTH = 2 * N_HEADS * HEAD_DIM
B_CONV = 3
C_CONV = 4
C_BLOCKS = 4
LRU_C = 8.0
TOP_K = 4
SWIGLU_ALPHA = 1.702
SWIGLU_LIMIT = 7.0

LANES = 128
SUBLANES = 8
TM = 256
TME = 256
POS_CHUNK = 8192
VMEM_LIMIT = 56 * 1024 * 1024


def _cparams(sem, vmem=VMEM_LIMIT):
    return pltpu.CompilerParams(dimension_semantics=sem, vmem_limit_bytes=vmem)


def _norm_mod(x, g, shift, scale):
    ms = jnp.mean(x * x, axis=-1, keepdims=True)
    return (x * lax.rsqrt(ms + NORM_EPS) * g) * (1.0 + scale) + shift


def _dot(a, b):
    return jnp.dot(a, b, preferred_element_type=F32)


def _mod_kernel(cs_ref, w_ref, b_ref, o_ref):
    s = cs_ref[...]
    s = s * jax.nn.sigmoid(s)
    o_ref[0] = _dot(s.astype(BF16), w_ref[0].astype(BF16)) + b_ref[0]


def _modulation(cs, w_mod, b_mod):
    depth, d, n = w_mod.shape
    rows = cs.shape[0]
    tn = 1536
    return pl.pallas_call(
        _mod_kernel,
        grid=(depth, n // tn),
        in_specs=[
            pl.BlockSpec((rows, d), lambda i, j: (0, 0)),
            pl.BlockSpec((1, d, tn), lambda i, j: (i, 0, j)),
            pl.BlockSpec((1, 1, tn), lambda i, j: (i, 0, j)),
        ],
        out_specs=pl.BlockSpec((1, rows, tn), lambda i, j: (i, 0, j)),
        out_shape=jax.ShapeDtypeStruct((depth, rows, n), F32),
        compiler_params=_cparams(("parallel", "parallel")),
        name="adaln_modulation",
    )(cs, w_mod, b_mod.reshape(depth, 1, n))


def _in0_kernel(x_ref, mod_ref, g_ref, w_ref, cos_ref, sin_ref,
                q_ref, k_ref, v_ref, bg_ref, p_ref):
    mod = mod_ref[0]
    h = _norm_mod(x_ref[...], g_ref[...], mod[0:1], mod[1:2])
    y = _dot(h.astype(BF16), w_ref[...])
    cosv = cos_ref[...]
    sinv = sin_ref[...]
    lane = lax.broadcasted_iota(I32, (TM, LANES), 1)
    first_half = (lane & 16) == 0

    def rope(z):
        outs = []
        for g in range(A_WIDTH // LANES):
            zg = z[:, g * LANES:(g + 1) * LANES]
            partner = jnp.where(first_half, pltpu.roll(zg, LANES - 16, 1), pltpu.roll(zg, 16, 1))
            outs.append(zg * cosv + partner * sinv)
        return jnp.concatenate(outs, axis=1)

    aw = A_WIDTH
    q_ref[...] = (rope(y[:, :aw]) * (HEAD_DIM ** -0.5)).astype(BF16)
    k_ref[...] = rope(y[:, aw:2 * aw]).astype(BF16)
    v_ref[...] = y[:, 2 * aw:3 * aw].astype(BF16)
    bw = (y.shape[1] - 3 * aw) // 3
    bg_ref[...] = y[:, 3 * aw:3 * aw + bw].astype(BF16)
    p_ref[...] = (y[:, 3 * aw + bw:3 * aw + 2 * bw] * y[:, 3 * aw + 2 * bw:]).astype(BF16)


def _in_proj0(x, modv, g, w, cos_t, sin_t, ntb):
    t, d = x.shape
    n = w.shape[1]
    bw = (n - 3 * A_WIDTH) // 3
    row = lambda i: (i, 0)
    return pl.pallas_call(
        _in0_kernel,
        grid=(t // TM,),
        in_specs=[
            pl.BlockSpec((TM, d), row),
            pl.BlockSpec((1, 6, d), lambda i: ((i // ntb) * 2 + jnp.minimum(i % ntb, 1), 0, 0)),
            pl.BlockSpec((1, d), lambda i: (0, 0)),
            pl.BlockSpec((d, n), lambda i: (0, 0)),
            pl.BlockSpec((TM, LANES), lambda i: (i % ntb, 0)),
            pl.BlockSpec((TM, LANES), lambda i: (i % ntb, 0)),
        ],
        out_specs=[pl.BlockSpec((TM, A_WIDTH), row)] * 3 + [pl.BlockSpec((TM, bw), row)] * 2,
        out_shape=[jax.ShapeDtypeStruct((t, A_WIDTH), BF16)] * 3 + [jax.ShapeDtypeStruct((t, bw), BF16)] * 2,
        compiler_params=_cparams(("parallel",)),
        name="l0_in_proj_rope",
    )(x, modv, g, w, cos_t, sin_t)


def _attn_kernel(lq1_ref, lk1_ref, lq2_ref, lk2_ref, g_ref, q_ref, k_ref, v_ref, o_ref, *, lam_init, n_ctx):
    qi = pl.program_id(2)
    lam = (jnp.exp(jnp.sum(lq1_ref[...] * lk1_ref[...], axis=-1, keepdims=True))
           - jnp.exp(jnp.sum(lq2_ref[...] * lk2_ref[...], axis=-1, keepdims=True)) + lam_init)
    q = q_ref[...]
    lane = lax.broadcasted_iota(I32, q.shape, 1)
    zero = jnp.zeros_like(q)
    q1 = jnp.where(lane < HEAD_DIM, q, zero)
    q2 = jnp.where(lane < HEAD_DIM, zero, q)
    contract_last = (((1,), (1,)), ((), ()))

    def attend(nk):
        k = k_ref[0, :nk, :]
        v = v_ref[0, :nk, :]
        s1 = lax.dot_general(q1, k, contract_last, preferred_element_type=F32)
        s2 = lax.dot_general(q2, k, contract_last, preferred_element_type=F32)
        p1 = jnp.exp(s1 - jnp.max(s1, axis=-1, keepdims=True))
        p2 = jnp.exp(s2 - jnp.max(s2, axis=-1, keepdims=True))
        r1 = 1.0 / jnp.sum(p1, axis=-1, keepdims=True)
        r2 = lam / jnp.sum(p2, axis=-1, keepdims=True)
        w = p1 * r1 - p2 * r2
        o = _dot(w.astype(BF16), v)
        ms = jnp.mean(o * o, axis=-1, keepdims=True)
        o = o * lax.rsqrt(ms + NORM_EPS) * g_ref[...] * (1.0 - lam_init)
        o_ref[...] = o.astype(BF16)

    @pl.when(qi == 0)
    def _():
        attend(n_ctx)

    @pl.when(qi > 0)
    def _():
        attend(k_ref.shape[1])


def _diff_attention(q, k, v, lq1, lk1, lq2, lk2, subln, lam_init, b, l, n_ctx):
    t = q.shape[0]
    ntb = l // TM
    hw = 2 * HEAD_DIM
    k3 = k.reshape(b, l, A_WIDTH)
    v3 = v.reshape(b, l, A_WIDTH)
    vec = lambda n: pl.BlockSpec((1, n), lambda bi, h, qi: (0, 0))
    qspec = pl.BlockSpec((TM, hw), lambda bi, h, qi: (bi * ntb + qi, h))
    kspec = pl.BlockSpec((1, l, hw), lambda bi, h, qi: (bi, 0, h))
    return pl.pallas_call(
        functools.partial(_attn_kernel, lam_init=lam_init, n_ctx=n_ctx),
        grid=(b, N_HEADS, ntb),
        in_specs=[vec(HEAD_DIM)] * 4 + [vec(hw), qspec, kspec, kspec],
        out_specs=qspec,
        out_shape=jax.ShapeDtypeStruct((t, A_WIDTH), BF16),
        compiler_params=_cparams(("parallel", "parallel", "parallel")),
        name="l0_diff_attention",
    )(lq1, lk1, lq2, lk2, subln, q, k3, v3)


def _route_tail(y, x_ref, mod_ref, g_ref, wr_ref, br_ref,
                x1_ref, h2_ref, route_ref, cnt_ref, carry_ref):
    mod = mod_ref[0]
    x1 = x_ref[...] + mod[2:3] * y
    x1_ref[...] = x1
    h2 = _norm_mod(x1, g_ref[...], mod[3:4], mod[4:5])
    h2_ref[...] = h2

    wr = wr_ref[...]
    h_hi = h2.astype(BF16)
    h_lo = (h2 - h_hi.astype(F32)).astype(BF16)
    w_hi = wr.astype(BF16)
    w_lo = (wr - w_hi.astype(F32)).astype(BF16)
    logits = _dot(h_hi, w_hi) + _dot(h_hi, w_lo) + _dot(h_lo, w_hi) + br_ref[...]

    n_exp = logits.shape[1]
    lane = lax.broadcasted_iota(I32, logits.shape, 1).astype(F32)
    work = logits
    sels, vals, idxs = [], [], []
    for _ in range(TOP_K):
        m = jnp.max(work, axis=-1, keepdims=True)
        idx = jnp.min(jnp.where(work == m, lane, float(n_exp)), axis=-1, keepdims=True)
        sel = lane == idx
        sels.append(sel)
        vals.append(m)
        idxs.append(idx)
        work = jnp.where(sel, -jnp.inf, work)
    exps = [jnp.exp(vv - vals[0]) for vv in vals]
    inv_den = 1.0 / (exps[0] + exps[1] + exps[2] + exps[3])

    chosen = jnp.zeros(logits.shape, F32)
    for sel in sels:
        chosen = chosen + jnp.where(sel, 1.0, 0.0)
    r_i = lax.broadcasted_iota(I32, (TM, TM), 0)
    c_i = lax.broadcasted_iota(I32, (TM, TM), 1)
    earlier = jnp.where(c_i < r_i, 1.0, 0.0).astype(BF16)
    rank = _dot(earlier, chosen.astype(BF16)) + carry_ref[...]
    carry_ref[...] = carry_ref[...] + jnp.sum(chosen, axis=0, keepdims=True)
    cnt_ref[...] = carry_ref[...]

    out_lane = lax.broadcasted_iota(I32, (TM, LANES), 1)
    packed = jnp.zeros((TM, LANES), F32)
    for kk in range(TOP_K):
        rank_k = jnp.sum(jnp.where(sels[kk], rank, 0.0), axis=-1, keepdims=True)
        packed = jnp.where(out_lane == kk, idxs[kk], packed)
        packed = jnp.where(out_lane == TOP_K + kk, rank_k, packed)
        packed = jnp.where(out_lane == 2 * TOP_K + kk, exps[kk] * inv_den, packed)
    route_ref[...] = packed


def _out0_kernel(attn_ref, bg_ref, p_ref, pprev_ref, pnext_ref, cw_ref, wo_ref,
                 x_ref, mod_ref, g_ref, wr_ref, br_ref,
                 x1_ref, h2_ref, route_ref, cnt_ref, carry_ref, *, ntb):
    i = pl.program_id(0)
    seg = i % ntb

    @pl.when(i == 0)
    def _():
        carry_ref[...] = jnp.zeros_like(carry_ref)

    p = p_ref[...].astype(F32)
    row = lax.broadcasted_iota(I32, p.shape, 0)
    has_prev = seg > 1
    has_next = jnp.logical_and(seg > 0, seg < ntb - 1)
    prev_row = jnp.where(has_prev, pprev_ref[SUBLANES - 1:SUBLANES, :].astype(F32), 0.0)
    next_row = jnp.where(has_next, pnext_ref[0:1, :].astype(F32), 0.0)
    before = jnp.where(row == 0, prev_row, pltpu.roll(p, 1, 0))
    after = jnp.where(row == TM - 1, next_row, pltpu.roll(p, TM - 1, 0))
    cw = cw_ref[...]
    conv = bg_ref[...].astype(F32) * (cw[0:1] * before + cw[1:2] * p + cw[2:3] * after)
    aw = attn_ref.shape[1]
    y = _dot(attn_ref[...], wo_ref[:aw, :]) + _dot(conv.astype(BF16), wo_ref[aw:, :])
    _route_tail(y, x_ref, mod_ref, g_ref, wr_ref, br_ref, x1_ref, h2_ref, route_ref, cnt_ref, carry_ref)


def _out1_kernel(hs_ref, gg_ref, wo_ref, x_ref, mod_ref, g_ref, wr_ref, br_ref,
                 x1_ref, h2_ref, route_ref, cnt_ref, carry_ref, *, ntb):
    i = pl.program_id(0)

    @pl.when(i == 0)
    def _():
        carry_ref[...] = jnp.zeros_like(carry_ref)

    @pl.when(i % ntb > 0)
    def _():
        rec = hs_ref[0].astype(F32) + hs_ref[1].astype(F32)
        y = _dot((rec * gg_ref[...].astype(F32)).astype(BF16), wo_ref[...])
        _route_tail(y, x_ref, mod_ref, g_ref, wr_ref, br_ref, x1_ref, h2_ref, route_ref, cnt_ref, carry_ref)


def _tail_specs(d, n_exp, ntb, latent_only):
    row = lambda i: (i, 0)
    const = lambda i: (0, 0)
    out_row = (lambda i: ((i // ntb) * (ntb - 1) + jnp.maximum(i % ntb - 1, 0), 0)) if latent_only else row
    in_specs = [
        pl.BlockSpec((TM, d), row),
        pl.BlockSpec((1, 6, d), lambda i: ((i // ntb) * 2 + jnp.minimum(i % ntb, 1), 0, 0)),
        pl.BlockSpec((1, d), const),
        pl.BlockSpec((d, n_exp), const),
        pl.BlockSpec((1, n_exp), const),
    ]
    out_specs = [
        pl.BlockSpec((TM, d), out_row),
        pl.BlockSpec((TM, d), out_row),
        pl.BlockSpec((TM, LANES), out_row),
        pl.BlockSpec((1, n_exp), const),
    ]
    return in_specs, out_specs


def _tail_shapes(t, d, n_exp):
    return [jax.ShapeDtypeStruct((t, d), F32), jax.ShapeDtypeStruct((t, d), F32),
            jax.ShapeDtypeStruct((t, LANES), F32), jax.ShapeDtypeStruct((1, n_exp), F32)]


def _out_proj0(attn, bg, p, conv_w, w_out, x, modv, g, w_r, b_r, ntb):
    t, d = x.shape
    n_exp = w_r.shape[1]
    bw = bg.shape[1]
    row = lambda i: (i, 0)
    const = lambda i: (0, 0)
    nblk = TM // SUBLANES
    tail_in, tail_out = _tail_specs(d, n_exp, ntb, False)
    return pl.pallas_call(
        functools.partial(_out0_kernel, ntb=ntb),
        grid=(t // TM,),
        in_specs=[
            pl.BlockSpec((TM, attn.shape[1]), row),
            pl.BlockSpec((TM, bw), row),
            pl.BlockSpec((TM, bw), row),
            pl.BlockSpec((SUBLANES, bw), lambda i: (jnp.maximum(i * nblk - 1, 0), 0)),
            pl.BlockSpec((SUBLANES, bw), lambda i: (jnp.minimum((i + 1) * nblk, t // SUBLANES - 1), 0)),
            pl.BlockSpec(conv_w.shape, const),
            pl.BlockSpec(w_out.shape, const),
        ] + tail_in,
        out_specs=tail_out,
        out_shape=_tail_shapes(t, d, n_exp),
        scratch_shapes=[pltpu.VMEM((1, n_exp), F32)],
        compiler_params=_cparams(("arbitrary",)),
        name="l0_out_proj_router",
    )(attn, bg, p, p, p, conv_w, w_out, x, modv, g, w_r, b_r)


def _out_proj1(hs, gg, w_out, x, modv, g, w_r, b_r, ntb):
    t, d = x.shape
    n_exp = w_r.shape[1]
    row = lambda i: (i, 0)
    const = lambda i: (0, 0)
    tail_in, tail_out = _tail_specs(d, n_exp, ntb, True)
    t_lat = t // ntb * (ntb - 1)
    return pl.pallas_call(
        functools.partial(_out1_kernel, ntb=ntb),
        grid=(t // TM,),
        in_specs=[
            pl.BlockSpec((2, TM, hs.shape[2]), lambda i: (0, i, 0)),
            pl.BlockSpec((TM, gg.shape[1]), row),
            pl.BlockSpec(w_out.shape, const),
        ] + tail_in,
        out_specs=tail_out,
        out_shape=_tail_shapes(t_lat, d, n_exp),
        scratch_shapes=[pltpu.VMEM((1, n_exp), F32)],
        compiler_params=_cparams(("arbitrary",)),
        name="l1_out_proj_router",
    )(hs, gg, w_out, x, modv, g, w_r, b_r)


def _inv_kernel(pad_lo_ref, pad_hi_ref, pos_ref, inv_ref, buf_ref, sem):
    c = pl.program_id(0)
    n_chunks = pl.num_programs(0)
    slot = c % 2

    def chunk_copy(ci, s):
        return pltpu.make_async_copy(pos_ref.at[pl.ds(ci * POS_CHUNK, POS_CHUNK)], buf_ref.at[s], sem.at[s])

    @pl.when(c == 0)
    def _():
        chunk_copy(0, 0).start()

        def fill_range(e, carry):
            def fill(j, cc):
                inv_ref[j] = -1
                return cc
            return lax.fori_loop(pad_lo_ref[e], pad_hi_ref[e], fill, carry)

        lax.fori_loop(0, pad_lo_ref.shape[0], fill_range, 0)

    chunk_copy(c, slot).wait()

    @pl.when(c + 1 < n_chunks)
    def _():
        chunk_copy(c + 1, 1 - slot).start()

    base = c * POS_CHUNK

    def put(j8, carry):
        for u in range(SUBLANES):
            j = j8 * SUBLANES + u
            inv_ref[buf_ref[slot, j]] = base + j
        return carry

    lax.fori_loop(0, POS_CHUNK // SUBLANES, put, 0)


def _invert_slots(pos, pad_lo, pad_hi, n_inv):
    n_pairs = pos.shape[0]
    grid_spec = pltpu.PrefetchScalarGridSpec(
        num_scalar_prefetch=2,
        grid=(n_pairs // POS_CHUNK,),
        in_specs=[pl.BlockSpec(memory_space=pl.ANY)],
        out_specs=pl.BlockSpec(memory_space=pltpu.SMEM),
        scratch_shapes=[pltpu.SMEM((2, POS_CHUNK), I32), pltpu.SemaphoreType.DMA((2,))],
    )
    return pl.pallas_call(
        _inv_kernel,
        grid_spec=grid_spec,
        out_shape=jax.ShapeDtypeStruct((n_inv,), I32),
        compiler_params=_cparams(("arbitrary",)),
        name="slot_inversion",
    )(pad_lo, pad_hi, pos)


def _moe_kernel(te_ref, tv_ref, meta_ref, inv_ref, h2_ref, w1_ref, b1_ref, w2_ref, b2_ref, y_ref,
                xbuf, obuf, w1b, w2b, gsem, ssem, *, n_pairs):
    i = pl.program_id(0)
    n_grid = pl.num_programs(0)
    n_tiles = meta_ref[0]
    slot = i % 2
    ff = w2_ref.shape[1]

    def pair_of(tile, r):
        return inv_ref[tile * TME + jnp.minimum(r, tv_ref[tile] - 1)]

    def gather_row(tile, s, r):
        tok = jnp.right_shift(pair_of(tile, r), 2)
        return pltpu.make_async_copy(h2_ref.at[pl.ds(tok, 1)], xbuf.at[s, pl.ds(r, 1)], gsem.at[s])

    def scatter_row(tile, s, r):
        dst = jnp.where(r < tv_ref[tile], pair_of(tile, r), n_pairs + s * TME + r)
        return pltpu.make_async_copy(obuf.at[s, pl.ds(r, 1)], y_ref.at[pl.ds(dst, 1)], ssem.at[s])

    def for_rows(fn):
        def body(r8, c):
            for u in range(SUBLANES):
                fn(r8 * SUBLANES + u)
            return c
        lax.fori_loop(0, TME // SUBLANES, body, 0)

    def wait_gather(s):
        pltpu.make_async_copy(h2_ref.at[pl.ds(0, TME)], xbuf.at[s], gsem.at[s]).wait()

    def wait_scatter(s):
        pltpu.make_async_copy(obuf.at[s], y_ref.at[pl.ds(0, TME)], ssem.at[s]).wait()

    @pl.when(i == 0)
    def _():
        for_rows(lambda r: gather_row(0, 0, r).start())
        for s in range(2):
            obuf[s] = jnp.zeros(obuf.shape[1:], obuf.dtype)
            spare = pltpu.make_async_copy(obuf.at[s], y_ref.at[pl.ds(n_pairs + s * TME, TME)], ssem.at[s])
            spare.start()
            spare.wait()

    @pl.when(i < n_tiles)
    def _():
        wait_gather(slot)

        @pl.when(i + 1 < n_tiles)
        def _():
            for_rows(lambda r: gather_row(i + 1, 1 - slot, r).start())

        new_expert = jnp.logical_or(i == 0, te_ref[i] != te_ref[jnp.maximum(i - 1, 0)])

        @pl.when(new_expert)
        def _():
            w1b[...] = w1_ref[0].astype(BF16)
            w2b[...] = w2_ref[0].astype(BF16)

        h = _dot(xbuf[slot].astype(BF16), w1b[...]) + b1_ref[0]
        glu = jnp.minimum(h[:, :ff], SWIGLU_LIMIT)
        lin = jnp.clip(h[:, ff:], -SWIGLU_LIMIT, SWIGLU_LIMIT)
        act = glu * jax.nn.sigmoid(SWIGLU_ALPHA * glu) * (lin + 1.0)
        y = _dot(act.astype(BF16), w2b[...]) + b2_ref[0]

        @pl.when(i >= 2)
        def _():
            wait_scatter(slot)

        obuf[slot] = y
        for_rows(lambda r: scatter_row(i, slot, r).start())

    @pl.when(i == n_grid - 1)
    def _():
        wait_scatter((n_tiles - 1) % 2)

        @pl.when(n_tiles >= 2)
        def _():
            wait_scatter(n_tiles % 2)


def _routed_experts(tile_expert, tile_valid, meta, inv, h2, w1, b1, w2, b2, n_pairs):
    t, d = h2.shape
    n_exp, _, ff2 = w1.shape
    ff = w2.shape[1]
    n_grid = tile_expert.shape[0]
    exp3 = lambda i, te, tv, mt, iv: (te[i], 0, 0)
    grid_spec = pltpu.PrefetchScalarGridSpec(
        num_scalar_prefetch=4,
        grid=(n_grid,),
        in_specs=[
            pl.BlockSpec(memory_space=pl.ANY),
            pl.BlockSpec((1, d, ff2), exp3),
            pl.BlockSpec((1, 1, ff2), exp3),
            pl.BlockSpec((1, ff, d), exp3),
            pl.BlockSpec((1, 1, d), exp3),
        ],
        out_specs=pl.BlockSpec(memory_space=pl.ANY),
        scratch_shapes=[
            pltpu.VMEM((2, TME, d), F32),
            pltpu.VMEM((2, TME, d), F32),
            pltpu.VMEM((d, ff2), BF16),
            pltpu.VMEM((ff, d), BF16),
            pltpu.SemaphoreType.DMA((2,)),
            pltpu.SemaphoreType.DMA((2,)),
        ],
    )
    return pl.pallas_call(
        functools.partial(_moe_kernel, n_pairs=n_pairs),
        grid_spec=grid_spec,
        out_shape=jax.ShapeDtypeStruct((n_pairs + 2 * TME, d), F32),
        compiler_params=_cparams(("arbitrary",)),
        name="routed_experts",
    )(tile_expert, tile_valid, meta, inv, h2, w1, b1.reshape(n_exp, 1, ff2), w2, b2.reshape(n_exp, 1, d))


def _plan_routes(route, counts, n_grid):
    n_exp = counts.shape[-1]
    cnt = counts.reshape(n_exp).astype(I32)
    tiles_per = (cnt + TME - 1) // TME
    tile_end = jnp.cumsum(tiles_per)
    offset = (tile_end - tiles_per) * TME
    n_tiles = tile_end[-1]
    tile_ids = jnp.minimum(jnp.arange(n_grid, dtype=I32), n_tiles - 1)
    tile_expert = jnp.sum((tile_ids[:, None] >= tile_end[None, :]).astype(I32), axis=1)
    tile_start = tile_end - tiles_per
    tile_valid = jnp.clip(jnp.take(cnt, tile_expert) - (tile_ids - jnp.take(tile_start, tile_expert)) * TME, 1, TME)
    eidx = route[:, :TOP_K].astype(I32)
    rank = route[:, TOP_K:2 * TOP_K].astype(I32)
    pos = jnp.take(offset, eidx) + rank
    pad_lo = jnp.concatenate([offset + cnt, (n_tiles * TME).reshape(1)])
    pad_hi = jnp.concatenate([tile_end * TME, jnp.full((1,), n_grid * TME, I32)])
    return tile_expert, tile_valid, n_tiles.reshape(1), pos.reshape(-1), pad_lo, pad_hi


def _moe_layer(route, counts, h2, w1, b1, w2, b2):
    n_exp = w1.shape[0]
    n_pairs = h2.shape[0] * TOP_K
    n_grid = n_pairs // TME + n_exp
    tile_expert, tile_valid, meta, pos, pad_lo, pad_hi = _plan_routes(route, counts, n_grid)
    inv = _invert_slots(pos, pad_lo, pad_hi, n_grid * TME)
    return _routed_experts(tile_expert, tile_valid, meta, inv, h2, w1, b1, w2, b2, n_pairs)


def _combine(y_ref, route_ref):
    d = y_ref.shape[1] // TOP_K
    gates = route_ref[...]
    acc = gates[:, 2 * TOP_K:2 * TOP_K + 1] * y_ref[:, :d]
    for kk in range(1, TOP_K):
        acc = acc + gates[:, 2 * TOP_K + kk:2 * TOP_K + kk + 1] * y_ref[:, kk * d:(kk + 1) * d]
    return acc


def _in1_kernel(x_ref, y_ref, route_ref, mod0_ref, mod1_ref, g_ref, w_ref, x2_ref, gg_ref, u_ref):
    x2 = x_ref[...] + mod0_ref[0][5:6] * _combine(y_ref, route_ref)
    x2_ref[...] = x2
    mod1 = mod1_ref[0]
    h = _norm_mod(x2, g_ref[...], mod1[0:1], mod1[1:2])
    y = _dot(h.astype(BF16), w_ref[...])
    half = y.shape[1] // 2
    gg_ref[...] = jax.nn.gelu(y[:, :half]).astype(BF16)
    u_ref[...] = y[:, half:]


def _in_proj1(x, ybuf, route, modv0, modv1, g, w, ntb):
    t, d = x.shape
    n = w.shape[1]
    row = lambda i: (i, 0)
    mod_map = lambda i: ((i // ntb) * 2 + jnp.minimum(i % ntb, 1), 0, 0)
    y4 = ybuf.reshape(ybuf.shape[0] // TOP_K, TOP_K * d)
    return pl.pallas_call(
        _in1_kernel,
        grid=(t // TM,),
        in_specs=[
            pl.BlockSpec((TM, d), row),
            pl.BlockSpec((TM, TOP_K * d), row),
            pl.BlockSpec((TM, LANES), row),
            pl.BlockSpec((1, 6, d), mod_map),
            pl.BlockSpec((1, 6, d), mod_map),
            pl.BlockSpec((1, d), lambda i: (0, 0)),
            pl.BlockSpec((d, n), lambda i: (0, 0)),
        ],
        out_specs=[pl.BlockSpec((TM, d), row), pl.BlockSpec((TM, n // 2), row), pl.BlockSpec((TM, n // 2), row)],
        out_shape=[jax.ShapeDtypeStruct((t, d), F32), jax.ShapeDtypeStruct((t, n // 2), BF16),
                   jax.ShapeDtypeStruct((t, n // 2), F32)],
        compiler_params=_cparams(("parallel",)),
        name="l1_combine_in_proj",
    )(x, y4, route, modv0, modv1, g, w)


def _rglru_kernel(u_ref, halo_ref, cw_ref, cb_ref, gaw_ref, gab_ref, gxw_ref, gxb_ref, lam_ref, o_ref,
                  ext, a_s, b_s, h_s, state, *, ntb):
    d = pl.program_id(0)
    s = pl.program_id(2)
    nb, tc, cw_ = u_ref.shape
    u = u_ref[...]
    cw = cw_ref[0]

    def finish(xc, reverse):
        xc2 = xc.reshape(nb * tc, cw_) + cb_ref[0]
        xb = xc2.astype(BF16)
        r = jax.nn.sigmoid(_dot(xb, gaw_ref[0, 0]) + gab_ref[0])
        gi = jax.nn.sigmoid(_dot(xb, gxw_ref[0, 0]) + gxb_ref[0])
        nl = -lam_ref[0]
        softplus = jnp.maximum(nl, 0.0) + jnp.log1p(jnp.exp(-jnp.abs(nl)))
        log_a = (-LRU_C) * r * softplus
        a = jnp.exp(log_a)
        bb = jnp.sqrt(1.0 - a * a) * (gi * xc2)
        n_lane = cw_ // LANES
        for c in range(n_lane):
            a_s[c] = a[:, c * LANES:(c + 1) * LANES]
            b_s[c] = bb[:, c * LANES:(c + 1) * LANES]

        @pl.when(s == 0)
        def _():
            state[...] = jnp.zeros_like(state)

        def step(tt, hs):
            t = (tc - 1 - tt) if reverse else tt
            rows = pl.ds(t, nb, stride=tc)
            out = []
            for c in range(n_lane):
                h = a_s[c, rows, :] * hs[c] + b_s[c, rows, :]
                h_s[c, rows, :] = h
                out.append(h)
            return tuple(out)

        hs = lax.fori_loop(0, tc, step, tuple(state[c] for c in range(n_lane)))
        for c in range(n_lane):
            state[c] = hs[c]
            o_ref[0, :, :, c * LANES:(c + 1) * LANES] = h_s[c].reshape(nb, tc, LANES).astype(o_ref.dtype)

    @pl.when(d == 0)
    def _():
        chunk = s
        keep = chunk > 1
        ext[:, 0:SUBLANES, :] = jnp.where(keep, halo_ref[...], 0.0)
        ext[:, SUBLANES:, :] = u
        xc = cw[3:4] * u
        for j in range(1, C_CONV):
            xc = xc + cw[3 - j:4 - j] * ext[:, SUBLANES - j:SUBLANES - j + tc, :]
        finish(xc, False)

    @pl.when(d == 1)
    def _():
        chunk = jnp.where(s == 0, 0, ntb - s)
        keep = jnp.logical_and(chunk > 0, chunk < ntb - 1)
        ext[:, 0:tc, :] = u
        ext[:, tc:, :] = jnp.where(keep, halo_ref[...], 0.0)
        xc = cw[0:1] * u
        for j in range(1, C_CONV):
            xc = xc + cw[j:j + 1] * ext[:, j:j + tc, :]
        finish(xc, True)


def _rglru(u, conv_w, conv_b, ga_w, ga_b, gx_w, gx_b, lam, b, l):
    width = u.shape[1]
    cb = width // C_BLOCKS
    ntb = l // TM
    u3 = u.reshape(b, l, width)
    nblk = TM // SUBLANES

    def chunk_of(d, s):
        return jnp.where(d == 0, s, jnp.where(s == 0, 0, ntb - s))

    def halo_of(d, s):
        c = chunk_of(d, s)
        return jnp.where(d == 0, jnp.maximum(c * nblk - 1, 0), jnp.minimum((c + 1) * nblk, l // SUBLANES - 1))

    vec = pl.BlockSpec((1, 1, cb), lambda d, g, s: (d, 0, g))
    mat = pl.BlockSpec((1, 1, cb, cb), lambda d, g, s: (d, g, 0, 0))
    return pl.pallas_call(
        functools.partial(_rglru_kernel, ntb=ntb),
        grid=(2, C_BLOCKS, ntb),
        in_specs=[
            pl.BlockSpec((b, TM, cb), lambda d, g, s: (0, chunk_of(d, s), g)),
            pl.BlockSpec((b, SUBLANES, cb), lambda d, g, s: (0, halo_of(d, s), g)),
            pl.BlockSpec((1, C_CONV, cb), lambda d, g, s: (d, 0, g)),
            vec, mat, vec, mat, vec, vec,
        ],
        out_specs=pl.BlockSpec((1, b, TM, cb), lambda d, g, s: (d, 0, chunk_of(d, s), g)),
        out_shape=jax.ShapeDtypeStruct((2, b, l, width), BF16),
        scratch_shapes=[
            pltpu.VMEM((b, TM + SUBLANES, cb), F32),
            pltpu.VMEM((cb // LANES, b * TM, LANES), F32),
            pltpu.VMEM((cb // LANES, b * TM, LANES), F32),
            pltpu.VMEM((cb // LANES, b * TM, LANES), F32),
            pltpu.VMEM((cb // LANES, b, LANES), F32),
        ],
        compiler_params=_cparams(("arbitrary", "arbitrary", "arbitrary")),
        name="l1_rglru",
    )(u3, u3, conv_w, conv_b.reshape(2, 1, width), ga_w.astype(BF16), ga_b.reshape(2, 1, width),
      gx_w.astype(BF16), gx_b.reshape(2, 1, width), lam.reshape(2, 1, width))


def _final_kernel(x_ref, y_ref, route_ref, mod_ref, g_ref, o_ref):
    x = x_ref[...] + mod_ref[0][5:6] * _combine(y_ref, route_ref)
    ms = jnp.mean(x * x, axis=-1, keepdims=True)
    o_ref[...] = x * lax.rsqrt(ms + NORM_EPS) * g_ref[...]


def _final(x, ybuf, route, modv, g, b, s_len):
    t, d = x.shape
    per_b = s_len // TM
    row = lambda i: (i, 0)
    y4 = ybuf.reshape(ybuf.shape[0] // TOP_K, TOP_K * d)
    return pl.pallas_call(
        _final_kernel,
        grid=(t // TM,),
        in_specs=[
            pl.BlockSpec((TM, d), row),
            pl.BlockSpec((TM, TOP_K * d), row),
            pl.BlockSpec((TM, LANES), row),
            pl.BlockSpec((1, 6, d), lambda i: ((i // per_b) * 2 + 1, 0, 0)),
            pl.BlockSpec((1, d), lambda i: (0, 0)),
        ],
        out_specs=pl.BlockSpec((TM, d), row),
        out_shape=jax.ShapeDtypeStruct((t, d), F32),
        compiler_params=_cparams(("parallel",)),
        name="final_combine_norm",
    )(x, y4, route, modv, g)


def _rope_tables(s_len, n_ctx):
    n_rows = s_len // GRID_W
    rows, cols = jnp.meshgrid(jnp.arange(n_rows), jnp.arange(GRID_W), indexing="ij")
    pos = jnp.stack([rows.reshape(-1), cols.reshape(-1)], axis=-1).astype(F32)
    n_freq = HEAD_DIM // 4
    inv = ROPE_THETA ** (-jnp.arange(n_freq, dtype=F32) / n_freq)
    ang = pos[:, :, None] * inv
    cos, sin = jnp.cos(ang), jnp.sin(ang)
    cos64 = jnp.stack([cos, cos], axis=2).reshape(s_len, HEAD_DIM)
    sin64 = jnp.stack([-sin, sin], axis=2).reshape(s_len, HEAD_DIM)
    cos_l = jnp.tile(cos64, (1, LANES // HEAD_DIM))
    sin_l = jnp.tile(sin64, (1, LANES // HEAD_DIM))
    cos_t = jnp.concatenate([jnp.ones((n_ctx, LANES), F32), cos_l], axis=0)
    sin_t = jnp.concatenate([jnp.zeros((n_ctx, LANES), F32), sin_l], axis=0)
    return cos_t, sin_t


def kernel(x, c, ctx, c_ctx, w_mod, b_mod, norm_mix, norm_ffn, ev_w_in, ev_w_out, ev_lambda_q1, ev_lambda_k1, ev_lambda_q2, ev_lambda_k2, ev_subln, ev_conv_w, od_w_in, od_w_out, od_conv_w, od_conv_b, od_gate_a_w, od_gate_a_b, od_gate_x_w, od_gate_x_b, od_lru_lambda, moe_w_router, moe_b_router, moe_w1, moe_b1, moe_w2, moe_b2, final_norm):
    b, s_len, d = x.shape
    n_ctx = ctx.shape[1]
    l = n_ctx + s_len
    t = b * l
    ntb = l // TM
    assert n_ctx == TM and s_len % TM == 0 and w_mod.shape[0] == 2
    assert (t * TOP_K) % POS_CHUNK == 0 and (b * s_len * TOP_K) % POS_CHUNK == 0

    xs = jnp.concatenate([ctx, x], axis=1).reshape(t, d)

    n_rows = -(-(b + 1) // SUBLANES) * SUBLANES
    cs = jnp.concatenate([c, c_ctx[None, :], jnp.zeros((n_rows - b - 1, d), F32)], axis=0)
    mod = _modulation(cs, w_mod, b_mod)

    def mod_table(i):
        lat = mod[i, :b]
        cx = jnp.broadcast_to(mod[i, b][None, :], lat.shape)
        return jnp.stack([cx, lat], axis=1).reshape(b * 2, 6, d)

    modv0, modv1 = mod_table(0), mod_table(1)

    cos_t, sin_t = _rope_tables(s_len, n_ctx)
    q, k, v, bg, p = _in_proj0(xs, modv0, norm_mix[0:1], ev_w_in[0].astype(BF16), cos_t, sin_t, ntb)
    lam_init = 0.8 - 0.6 * math.exp(-0.3 * 0)
    attn = _diff_attention(q, k, v, ev_lambda_q1[0:1], ev_lambda_k1[0:1], ev_lambda_q2[0:1], ev_lambda_k2[0:1],
                           ev_subln[0:1], lam_init, b, l, n_ctx)
    x1, h2, route0, counts0 = _out_proj0(attn, bg, p, ev_conv_w[0], ev_w_out[0].astype(BF16), xs, modv0,
                                         norm_ffn[0:1], moe_w_router[0], moe_b_router[0:1], ntb)
    ybuf0 = _moe_layer(route0, counts0, h2, moe_w1[0], moe_b1[0], moe_w2[0], moe_b2[0])

    x2, gg, u = _in_proj1(x1, ybuf0, route0, modv0, modv1, norm_mix[1:2], od_w_in[0].astype(BF16), ntb)
    hs = _rglru(u, od_conv_w[0], od_conv_b[0], od_gate_a_w[0], od_gate_a_b[0], od_gate_x_w[0], od_gate_x_b[0],
                od_lru_lambda[0], b, l)
    x3, h3, route1, counts1 = _out_proj1(hs.reshape(2, t, hs.shape[-1]), gg, od_w_out[0].astype(BF16), x2, modv1,
                                         norm_ffn[1:2], moe_w_router[1], moe_b_router[1:2], ntb)
    ybuf1 = _moe_layer(route1, counts1, h3, moe_w1[1], moe_b1[1], moe_w2[1], moe_b2[1])

    out = _final(x3, ybuf1, route1, modv1, final_norm[None, :], b, s_len)
    return out.reshape(b, s_len, d)
```

```python
import functools
import math

import jax
import jax.numpy as jnp
from jax import lax
from jax.experimental import pallas as pl
from jax.experimental.pallas import tpu as pltpu

F32 = jnp.float32
BF16 = jnp.bfloat16
I32 = jnp.int32

NORM_EPS = 1e-6
ROPE_THETA = 10000.0
GRID_W = 64
N_HEADS = 4
HEAD_DIM = 64
A_WIDTH = 2 * N_HEADS * HEAD_DIM
B_CONV = 3
C_CONV = 4
C_BLOCKS = 4
LRU_C = 8.0
TOP_K = 4
SWIGLU_ALPHA = 1.702
SWIGLU_LIMIT = 7.0

LANES = 128
SUBLANES = 8
TM = 256
TME = 256
POS_CHUNK = 8192
INV_UNROLL = 16
VMEM_LIMIT = 56 * 1024 * 1024


def _cparams(sem, vmem=VMEM_LIMIT):
    return pltpu.CompilerParams(dimension_semantics=sem, vmem_limit_bytes=vmem)


def _norm_mod(x, g, shift, scale):
    ms = jnp.mean(x * x, axis=-1, keepdims=True)
    return (x * lax.rsqrt(ms + NORM_EPS) * g) * (1.0 + scale) + shift


def _dot(a, b):
    return jnp.dot(a, b, preferred_element_type=F32)


def _mod_kernel(cs_ref, w_ref, b_ref, o_ref):
    s = cs_ref[...]
    s = s * jax.nn.sigmoid(s)
    o_ref[0] = _dot(s.astype(BF16), w_ref[0].astype(BF16)) + b_ref[0]


def _modulation(cs, w_mod, b_mod):
    depth, d, n = w_mod.shape
    rows = cs.shape[0]
    tn = 1536
    return pl.pallas_call(
        _mod_kernel,
        grid=(depth, n // tn),
        in_specs=[
            pl.BlockSpec((rows, d), lambda i, j: (0, 0)),
            pl.BlockSpec((1, d, tn), lambda i, j: (i, 0, j)),
            pl.BlockSpec((1, 1, tn), lambda i, j: (i, 0, j)),
        ],
        out_specs=pl.BlockSpec((1, rows, tn), lambda i, j: (i, 0, j)),
        out_shape=jax.ShapeDtypeStruct((depth, rows, n), F32),
        compiler_params=_cparams(("parallel", "parallel")),
        name="adaln_modulation",
    )(cs, w_mod, b_mod.reshape(depth, 1, n))


def _in0_kernel(x_ref, mod_ref, g_ref, w_ref, cos_ref, sin_ref,
                q_ref, k_ref, v_ref, bg_ref, p_ref):
    mod = mod_ref[0]
    h = _norm_mod(x_ref[...], g_ref[...], mod[0:1], mod[1:2])
    y = _dot(h.astype(BF16), w_ref[...])
    cosv = cos_ref[...]
    sinv = sin_ref[...]
    lane = lax.broadcasted_iota(I32, (TM, LANES), 1)
    first_half = (lane & 16) == 0

    def rope(z):
        outs = []
        for g in range(A_WIDTH // LANES):
            zg = z[:, g * LANES:(g + 1) * LANES]
            partner = jnp.where(first_half, pltpu.roll(zg, LANES - 16, 1), pltpu.roll(zg, 16, 1))
            outs.append(zg * cosv + partner * sinv)
        return jnp.concatenate(outs, axis=1)

    aw = A_WIDTH
    q_ref[...] = (rope(y[:, :aw]) * (HEAD_DIM ** -0.5)).astype(BF16)
    k_ref[...] = rope(y[:, aw:2 * aw]).astype(BF16)
    v_ref[...] = y[:, 2 * aw:3 * aw].astype(BF16)
    bw = (y.shape[1] - 3 * aw) // 3
    bg_ref[...] = y[:, 3 * aw:3 * aw + bw].astype(BF16)
    p_ref[...] = (y[:, 3 * aw + bw:3 * aw + 2 * bw] * y[:, 3 * aw + 2 * bw:]).astype(BF16)


def _in_proj0(x, modv, g, w, cos_t, sin_t, ntb):
    t, d = x.shape
    n = w.shape[1]
    bw = (n - 3 * A_WIDTH) // 3
    row = lambda i: (i, 0)
    return pl.pallas_call(
        _in0_kernel,
        grid=(t // TM,),
        in_specs=[
            pl.BlockSpec((TM, d), row),
            pl.BlockSpec((1, 6, d), lambda i: ((i // ntb) * 2 + jnp.minimum(i % ntb, 1), 0, 0)),
            pl.BlockSpec((1, d), lambda i: (0, 0)),
            pl.BlockSpec((d, n), lambda i: (0, 0)),
            pl.BlockSpec((TM, LANES), lambda i: (i % ntb, 0)),
            pl.BlockSpec((TM, LANES), lambda i: (i % ntb, 0)),
        ],
        out_specs=[pl.BlockSpec((TM, A_WIDTH), row)] * 3 + [pl.BlockSpec((TM, bw), row)] * 2,
        out_shape=[jax.ShapeDtypeStruct((t, A_WIDTH), BF16)] * 3 + [jax.ShapeDtypeStruct((t, bw), BF16)] * 2,
        compiler_params=_cparams(("parallel",)),
        name="l0_in_proj_rope",
    )(x, modv, g, w, cos_t, sin_t)


def _attn_kernel(lq1_ref, lk1_ref, lq2_ref, lk2_ref, g_ref, q_ref, k_ref, v_ref, o_ref, *, lam_init, n_ctx):
    qi = pl.program_id(2)
    lam = (jnp.exp(jnp.sum(lq1_ref[...] * lk1_ref[...], axis=-1, keepdims=True))
           - jnp.exp(jnp.sum(lq2_ref[...] * lk2_ref[...], axis=-1, keepdims=True)) + lam_init)
    q = q_ref[...]
    lane = lax.broadcasted_iota(I32, q.shape, 1)
    zero = jnp.zeros_like(q)
    q1 = jnp.where(lane < HEAD_DIM, q, zero)
    q2 = jnp.where(lane < HEAD_DIM, zero, q)
    contract_last = (((1,), (1,)), ((), ()))

    def attend(nk):
        k = k_ref[0, :nk, :]
        v = v_ref[0, :nk, :]
        s1 = lax.dot_general(q1, k, contract_last, preferred_element_type=F32)
        s2 = lax.dot_general(q2, k, contract_last, preferred_element_type=F32)
        p1 = jnp.exp(s1 - jnp.max(s1, axis=-1, keepdims=True))
        p2 = jnp.exp(s2 - jnp.max(s2, axis=-1, keepdims=True))
        r1 = 1.0 / jnp.sum(p1, axis=-1, keepdims=True)
        r2 = lam / jnp.sum(p2, axis=-1, keepdims=True)
        w = p1 * r1 - p2 * r2
        o = _dot(w.astype(BF16), v)
        ms = jnp.mean(o * o, axis=-1, keepdims=True)
        o = o * lax.rsqrt(ms + NORM_EPS) * g_ref[...] * (1.0 - lam_init)
        o_ref[...] = o.astype(BF16)

    @pl.when(qi == 0)
    def _():
        attend(n_ctx)

    @pl.when(qi > 0)
    def _():
        attend(k_ref.shape[1])


def _diff_attention(q, k, v, lq1, lk1, lq2, lk2, subln, lam_init, b, l, n_ctx):
    t = q.shape[0]
    ntb = l // TM
    hw = 2 * HEAD_DIM
    k3 = k.reshape(b, l, A_WIDTH)
    v3 = v.reshape(b, l, A_WIDTH)
    vec = lambda n: pl.BlockSpec((1, n), lambda bi, h, qi: (0, 0))
    qspec = pl.BlockSpec((TM, hw), lambda bi, h, qi: (bi * ntb + qi, h))
    kspec = pl.BlockSpec((1, l, hw), lambda bi, h, qi: (bi, 0, h))
    return pl.pallas_call(
        functools.partial(_attn_kernel, lam_init=lam_init, n_ctx=n_ctx),
        grid=(b, N_HEADS, ntb),
        in_specs=[vec(HEAD_DIM)] * 4 + [vec(hw), qspec, kspec, kspec],
        out_specs=qspec,
        out_shape=jax.ShapeDtypeStruct((t, A_WIDTH), BF16),
        compiler_params=_cparams(("parallel", "parallel", "parallel")),
        name="l0_diff_attention",
    )(lq1, lk1, lq2, lk2, subln, q, k3, v3)


def _route_tail(y, x_ref, mod_ref, g_ref, wr_ref, br_ref,
                x1_ref, h2_ref, route_ref, cnt_ref, carry_ref):
    mod = mod_ref[0]
    x1 = x_ref[...] + mod[2:3] * y
    x1_ref[...] = x1
    h2 = _norm_mod(x1, g_ref[...], mod[3:4], mod[4:5])
    nch = h2.shape[1] // LANES
    for s in range(nch):
        h2_ref[pl.ds(s, TM, stride=nch), :] = h2[:, s * LANES:(s + 1) * LANES]

    wr = wr_ref[...]
    h_hi = h2.astype(BF16)
    h_lo = (h2 - h_hi.astype(F32)).astype(BF16)
    w_hi = wr.astype(BF16)
    w_lo = (wr - w_hi.astype(F32)).astype(BF16)
    logits = _dot(h_hi, w_hi) + _dot(h_hi, w_lo) + _dot(h_lo, w_hi) + br_ref[...]

    n_exp = logits.shape[1]
    lane = lax.broadcasted_iota(I32, logits.shape, 1).astype(F32)
    work = logits
    sels, vals, idxs = [], [], []
    for _ in range(TOP_K):
        m = jnp.max(work, axis=-1, keepdims=True)
        idx = jnp.min(jnp.where(work == m, lane, float(n_exp)), axis=-1, keepdims=True)
        sel = lane == idx
        sels.append(sel)
        vals.append(m)
        idxs.append(idx)
        work = jnp.where(sel, -jnp.inf, work)
    exps = [jnp.exp(vv - vals[0]) for vv in vals]
    inv_den = 1.0 / (exps[0] + exps[1] + exps[2] + exps[3])

    chosen = jnp.zeros(logits.shape, F32)
    for sel in sels:
        chosen = chosen + jnp.where(sel, 1.0, 0.0)
    r_i = lax.broadcasted_iota(I32, (TM, TM), 0)
    c_i = lax.broadcasted_iota(I32, (TM, TM), 1)
    earlier = jnp.where(c_i < r_i, 1.0, 0.0).astype(BF16)
    rank = _dot(earlier, chosen.astype(BF16)) + carry_ref[...]
    carry_ref[...] = carry_ref[...] + jnp.sum(chosen, axis=0, keepdims=True)
    cnt_ref[...] = carry_ref[...]

    out_lane = lax.broadcasted_iota(I32, (TM, LANES), 1)
    packed = jnp.zeros((TM, LANES), F32)
    for kk in range(TOP_K):
        rank_k = jnp.sum(jnp.where(sels[kk], rank, 0.0), axis=-1, keepdims=True)
        packed = jnp.where(out_lane == kk, idxs[kk], packed)
        packed = jnp.where(out_lane == TOP_K + kk, rank_k, packed)
        packed = jnp.where(out_lane == 2 * TOP_K + kk, exps[kk] * inv_den, packed)
    route_ref[...] = packed


def _out0_kernel(attn_ref, bg_ref, p_ref, pprev_ref, pnext_ref, cw_ref, wo_ref,
                 x_ref, mod_ref, g_ref, wr_ref, br_ref,
                 x1_ref, h2_ref, route_ref, cnt_ref, carry_ref, *, ntb):
    i = pl.program_id(0)
    seg = i % ntb

    @pl.when(i == 0)
    def _():
        carry_ref[...] = jnp.zeros_like(carry_ref)

    p = p_ref[...].astype(F32)
    row = lax.broadcasted_iota(I32, p.shape, 0)
    has_prev = seg > 1
    has_next = jnp.logical_and(seg > 0, seg < ntb - 1)
    prev_row = jnp.where(has_prev, pprev_ref[SUBLANES - 1:SUBLANES, :].astype(F32), 0.0)
    next_row = jnp.where(has_next, pnext_ref[0:1, :].astype(F32), 0.0)
    before = jnp.where(row == 0, prev_row, pltpu.roll(p, 1, 0))
    after = jnp.where(row == TM - 1, next_row, pltpu.roll(p, TM - 1, 0))
    cw = cw_ref[...]
    conv = bg_ref[...].astype(F32) * (cw[0:1] * before + cw[1:2] * p + cw[2:3] * after)
    aw = attn_ref.shape[1]
    y = _dot(attn_ref[...], wo_ref[:aw, :]) + _dot(conv.astype(BF16), wo_ref[aw:, :])
    _route_tail(y, x_ref, mod_ref, g_ref, wr_ref, br_ref, x1_ref, h2_ref, route_ref, cnt_ref, carry_ref)


def _out1_kernel(hs_ref, gg_ref, wo_ref, x_ref, mod_ref, g_ref, wr_ref, br_ref,
                 x1_ref, h2_ref, route_ref, cnt_ref, carry_ref, *, ntb):
    i = pl.program_id(0)

    @pl.when(i == 0)
    def _():
        carry_ref[...] = jnp.zeros_like(carry_ref)

    @pl.when(i % ntb > 0)
    def _():
        rec = hs_ref[0].astype(F32) + hs_ref[1].astype(F32)
        y = _dot((rec * gg_ref[...].astype(F32)).astype(BF16), wo_ref[...])
        _route_tail(y, x_ref, mod_ref, g_ref, wr_ref, br_ref, x1_ref, h2_ref, route_ref, cnt_ref, carry_ref)


def _tail_specs(d, n_exp, ntb, latent_only):
    row = lambda i: (i, 0)
    const = lambda i: (0, 0)
    out_row = (lambda i: ((i // ntb) * (ntb - 1) + jnp.maximum(i % ntb - 1, 0), 0)) if latent_only else row
    in_specs = [
        pl.BlockSpec((TM, d), row),
        pl.BlockSpec((1, 6, d), lambda i: ((i // ntb) * 2 + jnp.minimum(i % ntb, 1), 0, 0)),
        pl.BlockSpec((1, d), const),
        pl.BlockSpec((d, n_exp), const),
        pl.BlockSpec((1, n_exp), const),
    ]
    out_specs = [
        pl.BlockSpec((TM, d), out_row),
        pl.BlockSpec((TM * (d // LANES), LANES), out_row),
        pl.BlockSpec((TM, LANES), out_row),
        pl.BlockSpec((1, n_exp), const),
    ]
    return in_specs, out_specs


def _tail_shapes(t, d, n_exp):
    return [jax.ShapeDtypeStruct((t, d), F32), jax.ShapeDtypeStruct((t * (d // LANES), LANES), F32),
            jax.ShapeDtypeStruct((t, LANES), F32), jax.ShapeDtypeStruct((1, n_exp), F32)]


def _out_proj0(attn, bg, p, conv_w, w_out, x, modv, g, w_r, b_r, ntb):
    t, d = x.shape
    n_exp = w_r.shape[1]
    bw = bg.shape[1]
    row = lambda i: (i, 0)
    const = lambda i: (0, 0)
    nblk = TM // SUBLANES
    tail_in, tail_out = _tail_specs(d, n_exp, ntb, False)
    return pl.pallas_call(
        functools.partial(_out0_kernel, ntb=ntb),
        grid=(t // TM,),
        in_specs=[
            pl.BlockSpec((TM, attn.shape[1]), row),
            pl.BlockSpec((TM, bw), row),
            pl.BlockSpec((TM, bw), row),
            pl.BlockSpec((SUBLANES, bw), lambda i: (jnp.maximum(i * nblk - 1, 0), 0)),
            pl.BlockSpec((SUBLANES, bw), lambda i: (jnp.minimum((i + 1) * nblk, t // SUBLANES - 1), 0)),
            pl.BlockSpec(conv_w.shape, const),
            pl.BlockSpec(w_out.shape, const),
        ] + tail_in,
        out_specs=tail_out,
        out_shape=_tail_shapes(t, d, n_exp),
        scratch_shapes=[pltpu.VMEM((1, n_exp), F32)],
        compiler_params=_cparams(("arbitrary",)),
        name="l0_out_proj_router",
    )(attn, bg, p, p, p, conv_w, w_out, x, modv, g, w_r, b_r)


def _out_proj1(hs, gg, w_out, x, modv, g, w_r, b_r, ntb):
    t, d = x.shape
    n_exp = w_r.shape[1]
    row = lambda i: (i, 0)
    const = lambda i: (0, 0)
    tail_in, tail_out = _tail_specs(d, n_exp, ntb, True)
    t_lat = t // ntb * (ntb - 1)
    return pl.pallas_call(
        functools.partial(_out1_kernel, ntb=ntb),
        grid=(t // TM,),
        in_specs=[
            pl.BlockSpec((2, TM, hs.shape[2]), lambda i: (0, i, 0)),
            pl.BlockSpec((TM, gg.shape[1]), row),
            pl.BlockSpec(w_out.shape, const),
        ] + tail_in,
        out_specs=tail_out,
        out_shape=_tail_shapes(t_lat, d, n_exp),
        scratch_shapes=[pltpu.VMEM((1, n_exp), F32)],
        compiler_params=_cparams(("arbitrary",)),
        name="l1_out_proj_router",
    )(hs, gg, w_out, x, modv, g, w_r, b_r)


def _inv_kernel(pad_lo_ref, pad_hi_ref, pos_ref, inv_ref, buf_ref, sem):
    c = pl.program_id(0)
    n_chunks = pl.num_programs(0)
    slot = c % 2

    def chunk_copy(ci, s):
        return pltpu.make_async_copy(pos_ref.at[pl.ds(ci * POS_CHUNK, POS_CHUNK)], buf_ref.at[s], sem.at[s])

    @pl.when(c == 0)
    def _():
        chunk_copy(0, 0).start()

        def fill_range(e, carry):
            def fill(j, cc):
                inv_ref[j] = -1
                return cc
            return lax.fori_loop(pad_lo_ref[e], pad_hi_ref[e], fill, carry)

        lax.fori_loop(0, pad_lo_ref.shape[0], fill_range, 0)

    chunk_copy(c, slot).wait()

    @pl.when(c + 1 < n_chunks)
    def _():
        chunk_copy(c + 1, 1 - slot).start()

    base = c * POS_CHUNK

    def put(jb, carry):
        j0 = jb * INV_UNROLL
        slots = [buf_ref[slot, j0 + u] for u in range(INV_UNROLL)]
        for u in range(INV_UNROLL):
            inv_ref[slots[u]] = base + j0 + u
        return carry

    lax.fori_loop(0, POS_CHUNK // INV_UNROLL, put, 0)


def _invert_slots(pos, pad_lo, pad_hi, n_inv):
    n_pairs = pos.shape[0]
    grid_spec = pltpu.PrefetchScalarGridSpec(
        num_scalar_prefetch=2,
        grid=(n_pairs // POS_CHUNK,),
        in_specs=[pl.BlockSpec(memory_space=pl.ANY)],
        out_specs=pl.BlockSpec(memory_space=pltpu.SMEM),
        scratch_shapes=[pltpu.SMEM((2, POS_CHUNK), I32), pltpu.SemaphoreType.DMA((2,))],
    )
    return pl.pallas_call(
        _inv_kernel,
        grid_spec=grid_spec,
        out_shape=jax.ShapeDtypeStruct((n_inv,), I32),
        compiler_params=_cparams(("arbitrary",)),
        name="slot_inversion",
    )(pad_lo, pad_hi, pos)


def _moe_kernel(te_ref, tv_ref, meta_ref, inv_ref, h2_ref, w1_ref, b1_ref, w2_ref, b2_ref, y_ref,
                xbuf, obuf, w1b, w2b, gsem, ssem, *, n_pairs):
    i = pl.program_id(0)
    n_tiles = meta_ref[0]
    slot = i % 2
    other = 1 - slot
    d = w1_ref.shape[2]
    ff = w2_ref.shape[2]
    nch = d // LANES

    def pair_of(tile, r, valid):
        return inv_ref[tile * TME + jnp.maximum(jnp.minimum(r, valid - 1), 0)]

    def gather_row(tile, s, r, valid):
        src = pl.multiple_of(jnp.right_shift(pair_of(tile, r, valid), 2) * nch, nch)
        return pltpu.make_async_copy(h2_ref.at[pl.ds(src, nch)], xbuf.at[s, pl.ds(r * nch, nch)], gsem.at[s])

    def scatter_row(tile, s, r, valid):
        row = jnp.where(r < valid, pair_of(tile, r, valid), n_pairs + s * TME + r)
        dst = pl.multiple_of(row * nch, nch)
        return pltpu.make_async_copy(obuf.at[s, pl.ds(r * nch, nch)], y_ref.at[pl.ds(dst, nch)], ssem.at[s])

    def for_rows(fn):
        def body(r8, c):
            for u in range(SUBLANES):
                fn(r8 * SUBLANES + u)
            return c
        lax.fori_loop(0, TME // SUBLANES, body, 0)

    def wait_gather(s):
        pltpu.make_async_copy(h2_ref.at[pl.ds(0, TME * nch)], xbuf.at[s], gsem.at[s]).wait()

    def wait_scatter(s):
        pltpu.make_async_copy(obuf.at[s], y_ref.at[pl.ds(0, TME * nch)], ssem.at[s]).wait()

    @pl.when(i == 0)
    def _():
        valid0 = tv_ref[0]
        for_rows(lambda r: gather_row(0, 0, r, valid0).start())
        obuf[...] = jnp.zeros(obuf.shape, obuf.dtype)
        pltpu.make_async_copy(obuf.at[0], y_ref.at[pl.ds(n_pairs * nch, TME * nch)], ssem.at[0]).start()

    @pl.when(i < n_tiles)
    def _():
        wait_gather(slot)
        new_expert = jnp.logical_or(i == 0, te_ref[i] != te_ref[jnp.maximum(i - 1, 0)])

        @pl.when(new_expert)
        def _():
            w1b[...] = w1_ref[0, 0].astype(BF16)
            w2b[...] = w2_ref[0, 0].astype(BF16)

        nxt = jnp.minimum(i + 1, n_tiles - 1)
        v_nxt = tv_ref[nxt]
        prv = jnp.maximum(i - 1, 0)
        v_prv = jnp.where(i == 0, 0, tv_ref[prv])
        for r in range(TME):
            gather_row(nxt, other, r, v_nxt).start()
            scatter_row(prv, other, r, v_prv).start()

        x = jnp.concatenate([xbuf[slot, pl.ds(c, TME, stride=nch), :] for c in range(nch)], axis=1)
        h = _dot(x.astype(BF16), w1b[...]) + b1_ref[0]
        glu = jnp.minimum(h[:, :ff], SWIGLU_LIMIT)
        lin = jnp.clip(h[:, ff:], -SWIGLU_LIMIT, SWIGLU_LIMIT)
        act = glu * jax.nn.sigmoid(SWIGLU_ALPHA * glu) * (lin + 1.0)
        y = _dot(act.astype(BF16), w2b[...]) + b2_ref[0]

        wait_scatter(slot)
        for c in range(nch):
            obuf[slot, pl.ds(c, TME, stride=nch), :] = y[:, c * LANES:(c + 1) * LANES]

    @pl.when(i == n_tiles)
    def _():
        last = n_tiles - 1
        v_last = tv_ref[last]
        for_rows(lambda r: scatter_row(last, last % 2, r, v_last).start())
        wait_gather(n_tiles % 2)
        wait_scatter(0)
        wait_scatter(1)


def _routed_experts(tile_expert, tile_valid, meta, inv, h2, w1, b1, w2, b2, layer):
    n_pairs = h2.shape[0] // (w1.shape[2] // LANES) * TOP_K
    _, n_exp, d, ff2 = w1.shape
    ff = w2.shape[2]
    nch = d // LANES
    n_grid = tile_expert.shape[0]
    tile = lambda i: jnp.minimum(i, n_grid - 1)
    wmap = lambda i, te, tv, mt, iv: (layer, te[tile(i)], 0, 0)
    bmap = lambda i, te, tv, mt, iv: (layer * n_exp + te[tile(i)], 0, 0)
    grid_spec = pltpu.PrefetchScalarGridSpec(
        num_scalar_prefetch=4,
        grid=(n_grid + 1,),
        in_specs=[
            pl.BlockSpec(memory_space=pl.ANY),
            pl.BlockSpec((1, 1, d, ff2), wmap),
            pl.BlockSpec((1, 1, ff2), bmap),
            pl.BlockSpec((1, 1, ff, d), wmap),
            pl.BlockSpec((1, 1, d), bmap),
        ],
        out_specs=pl.BlockSpec(memory_space=pl.ANY),
        scratch_shapes=[
            pltpu.VMEM((2, TME * nch, LANES), F32),
            pltpu.VMEM((2, TME * nch, LANES), F32),
            pltpu.VMEM((d, ff2), BF16),
            pltpu.VMEM((ff, d), BF16),
            pltpu.SemaphoreType.DMA((2,)),
            pltpu.SemaphoreType.DMA((2,)),
        ],
    )
    return pl.pallas_call(
        functools.partial(_moe_kernel, n_pairs=n_pairs),
        grid_spec=grid_spec,
        out_shape=jax.ShapeDtypeStruct(((n_pairs + 2 * TME) * nch, LANES), F32),
        compiler_params=_cparams(("arbitrary",)),
        name="routed_experts",
    )(tile_expert, tile_valid, meta, inv, h2, w1, b1.reshape(-1, 1, ff2), w2, b2.reshape(-1, 1, d))


def _plan_routes(route, counts, n_grid):
    n_exp = counts.shape[-1]
    cnt = counts.reshape(n_exp).astype(I32)
    tiles_per = (cnt + TME - 1) // TME
    tile_end = jnp.cumsum(tiles_per)
    offset = (tile_end - tiles_per) * TME
    n_tiles = tile_end[-1]
    tile_ids = jnp.minimum(jnp.arange(n_grid, dtype=I32), n_tiles - 1)
    tile_expert = jnp.sum((tile_ids[:, None] >= tile_end[None, :]).astype(I32), axis=1)
    tile_start = tile_end - tiles_per
    tile_valid = jnp.clip(jnp.take(cnt, tile_expert) - (tile_ids - jnp.take(tile_start, tile_expert)) * TME, 1, TME)
    eidx = route[:, :TOP_K].astype(I32)
    rank = route[:, TOP_K:2 * TOP_K].astype(I32)
    expert_ids = jnp.arange(n_exp, dtype=I32)
    pos = jnp.sum(jnp.where(eidx[..., None] == expert_ids, offset, 0), axis=-1) + rank
    pad_lo = jnp.concatenate([offset + cnt, (n_tiles * TME).reshape(1)])
    pad_hi = jnp.concatenate([tile_end * TME, jnp.full((1,), n_grid * TME, I32)])
    return tile_expert, tile_valid, n_tiles.reshape(1), pos.reshape(-1), pad_lo, pad_hi


def _moe_layer(route, counts, h2, w1, b1, w2, b2, layer):
    n_exp = w1.shape[1]
    n_pairs = route.shape[0] * TOP_K
    n_grid = n_pairs // TME + n_exp
    tile_expert, tile_valid, meta, pos, pad_lo, pad_hi = _plan_routes(route, counts, n_grid)
    inv = _invert_slots(pos, pad_lo, pad_hi, n_grid * TME)
    return _routed_experts(tile_expert, tile_valid, meta, inv, h2, w1, b1, w2, b2, layer)


def _combine(y_ref, route_ref, d):
    nch = d // LANES
    gates = route_ref[...]
    gk = [jnp.broadcast_to(gates[:, 2 * TOP_K + kk:2 * TOP_K + kk + 1], (TM, LANES)) for kk in range(TOP_K)]
    chunks = []
    for c in range(nch):
        acc = gk[0] * y_ref[pl.ds(c, TM, stride=TOP_K * nch), :]
        for kk in range(1, TOP_K):
            acc = acc + gk[kk] * y_ref[pl.ds(kk * nch + c, TM, stride=TOP_K * nch), :]
        chunks.append(acc)
    return jnp.concatenate(chunks, axis=1)


def _in1_kernel(x_ref, y_ref, route_ref, mod0_ref, mod1_ref, g_ref, w_ref, x2_ref, gg_ref, u_ref):
    x2 = x_ref[...] + mod0_ref[0][5:6] * _combine(y_ref, route_ref, x_ref.shape[1])
    x2_ref[...] = x2
    mod1 = mod1_ref[0]
    h = _norm_mod(x2, g_ref[...], mod1[0:1], mod1[1:2])
    y = _dot(h.astype(BF16), w_ref[...])
    half = y.shape[1] // 2
    gg_ref[...] = jax.nn.gelu(y[:, :half]).astype(BF16)
    u_ref[...] = y[:, half:]


def _in_proj1(x, ybuf, route, modv0, modv1, g, w, ntb):
    t, d = x.shape
    n = w.shape[1]
    row = lambda i: (i, 0)
    mod_map = lambda i: ((i // ntb) * 2 + jnp.minimum(i % ntb, 1), 0, 0)
    return pl.pallas_call(
        _in1_kernel,
        grid=(t // TM,),
        in_specs=[
            pl.BlockSpec((TM, d), row),
            pl.BlockSpec((TM * TOP_K * (d // LANES), LANES), row),
            pl.BlockSpec((TM, LANES), row),
            pl.BlockSpec((1, 6, d), mod_map),
            pl.BlockSpec((1, 6, d), mod_map),
            pl.BlockSpec((1, d), lambda i: (0, 0)),
            pl.BlockSpec((d, n), lambda i: (0, 0)),
        ],
        out_specs=[pl.BlockSpec((TM, d), row), pl.BlockSpec((TM, n // 2), row), pl.BlockSpec((TM, n // 2), row)],
        out_shape=[jax.ShapeDtypeStruct((t, d), F32), jax.ShapeDtypeStruct((t, n // 2), BF16),
                   jax.ShapeDtypeStruct((t, n // 2), F32)],
        compiler_params=_cparams(("parallel",)),
        name="l1_combine_in_proj",
    )(x, ybuf, route, modv0, modv1, g, w)


def _rglru_kernel(u_ref, halo_ref, cw_ref, cb_ref, gaw_ref, gab_ref, gxw_ref, gxb_ref, lam_ref, o_ref,
                  ext, a_s, b_s, h_s, state, *, ntb):
    d = pl.program_id(0)
    s = pl.program_id(2)
    nb, tc, cw_ = u_ref.shape
    u = u_ref[...]
    cw = cw_ref[0]

    def finish(xc, reverse):
        xc2 = xc.reshape(nb * tc, cw_) + cb_ref[0]
        xb = xc2.astype(BF16)
        r = jax.nn.sigmoid(_dot(xb, gaw_ref[0, 0]) + gab_ref[0])
        gi = jax.nn.sigmoid(_dot(xb, gxw_ref[0, 0]) + gxb_ref[0])
        nl = -lam_ref[0]
        softplus = jnp.maximum(nl, 0.0) + jnp.log1p(jnp.exp(-jnp.abs(nl)))
        log_a = (-LRU_C) * r * softplus
        a = jnp.exp(log_a)
        bb = jnp.sqrt(1.0 - a * a) * (gi * xc2)
        n_lane = cw_ // LANES
        for c in range(n_lane):
            a_s[c] = a[:, c * LANES:(c + 1) * LANES]
            b_s[c] = bb[:, c * LANES:(c + 1) * LANES]

        @pl.when(s == 0)
        def _():
            state[...] = jnp.zeros_like(state)

        def step(tt, hs):
            t = (tc - 1 - tt) if reverse else tt
            rows = pl.ds(t, nb, stride=tc)
            out = []
            for c in range(n_lane):
                h = a_s[c, rows, :] * hs[c] + b_s[c, rows, :]
                h_s[c, rows, :] = h
                out.append(h)
            return tuple(out)

        hs = lax.fori_loop(0, tc, step, tuple(state[c] for c in range(n_lane)))
        for c in range(n_lane):
            state[c] = hs[c]
            o_ref[0, :, :, c * LANES:(c + 1) * LANES] = h_s[c].reshape(nb, tc, LANES).astype(o_ref.dtype)

    @pl.when(d == 0)
    def _():
        chunk = s
        keep = chunk > 1
        ext[:, 0:SUBLANES, :] = jnp.where(keep, halo_ref[...], 0.0)
        ext[:, SUBLANES:, :] = u
        xc = cw[3:4] * u
        for j in range(1, C_CONV):
            xc = xc + cw[3 - j:4 - j] * ext[:, SUBLANES - j:SUBLANES - j + tc, :]
        finish(xc, False)

    @pl.when(d == 1)
    def _():
        chunk = jnp.where(s == 0, 0, ntb - s)
        keep = jnp.logical_and(chunk > 0, chunk < ntb - 1)
        ext[:, 0:tc, :] = u
        ext[:, tc:, :] = jnp.where(keep, halo_ref[...], 0.0)
        xc = cw[0:1] * u
        for j in range(1, C_CONV):
            xc = xc + cw[j:j + 1] * ext[:, j:j + tc, :]
        finish(xc, True)


def _rglru(u, conv_w, conv_b, ga_w, ga_b, gx_w, gx_b, lam, b, l):
    width = u.shape[1]
    cb = width // C_BLOCKS
    ntb = l // TM
    u3 = u.reshape(b, l, width)
    nblk = TM // SUBLANES

    def chunk_of(d, s):
        return jnp.where(d == 0, s, jnp.where(s == 0, 0, ntb - s))

    def halo_of(d, s):
        c = chunk_of(d, s)
        return jnp.where(d == 0, jnp.maximum(c * nblk - 1, 0), jnp.minimum((c + 1) * nblk, l // SUBLANES - 1))

    vec = pl.BlockSpec((1, 1, cb), lambda d, g, s: (d, 0, g))
    mat = pl.BlockSpec((1, 1, cb, cb), lambda d, g, s: (d, g, 0, 0))
    return pl.pallas_call(
        functools.partial(_rglru_kernel, ntb=ntb),
        grid=(2, C_BLOCKS, ntb),
        in_specs=[
            pl.BlockSpec((b, TM, cb), lambda d, g, s: (0, chunk_of(d, s), g)),
            pl.BlockSpec((b, SUBLANES, cb), lambda d, g, s: (0, halo_of(d, s), g)),
            pl.BlockSpec((1, C_CONV, cb), lambda d, g, s: (d, 0, g)),
            vec, mat, vec, mat, vec, vec,
        ],
        out_specs=pl.BlockSpec((1, b, TM, cb), lambda d, g, s: (d, 0, chunk_of(d, s), g)),
        out_shape=jax.ShapeDtypeStruct((2, b, l, width), BF16),
        scratch_shapes=[
            pltpu.VMEM((b, TM + SUBLANES, cb), F32),
            pltpu.VMEM((cb // LANES, b * TM, LANES), F32),
            pltpu.VMEM((cb // LANES, b * TM, LANES), F32),
            pltpu.VMEM((cb // LANES, b * TM, LANES), F32),
            pltpu.VMEM((cb // LANES, b, LANES), F32),
        ],
        compiler_params=_cparams(("arbitrary", "arbitrary", "arbitrary")),
        name="l1_rglru",
    )(u3, u3, conv_w, conv_b.reshape(2, 1, width), ga_w.astype(BF16), ga_b.reshape(2, 1, width),
      gx_w.astype(BF16), gx_b.reshape(2, 1, width), lam.reshape(2, 1, width))


def _final_kernel(x_ref, y_ref, route_ref, mod_ref, g_ref, o_ref):
    x = x_ref[...] + mod_ref[0][5:6] * _combine(y_ref, route_ref, x_ref.shape[1])
    ms = jnp.mean(x * x, axis=-1, keepdims=True)
    o_ref[...] = x * lax.rsqrt(ms + NORM_EPS) * g_ref[...]


def _final(x, ybuf, route, modv, g, b, s_len):
    t, d = x.shape
    per_b = s_len // TM
    row = lambda i: (i, 0)
    return pl.pallas_call(
        _final_kernel,
        grid=(t // TM,),
        in_specs=[
            pl.BlockSpec((TM, d), row),
            pl.BlockSpec((TM * TOP_K * (d // LANES), LANES), row),
            pl.BlockSpec((TM, LANES), row),
            pl.BlockSpec((1, 6, d), lambda i: ((i // per_b) * 2 + 1, 0, 0)),
            pl.BlockSpec((1, d), lambda i: (0, 0)),
        ],
        out_specs=pl.BlockSpec((TM, d), row),
        out_shape=jax.ShapeDtypeStruct((t, d), F32),
        compiler_params=_cparams(("parallel",)),
        name="final_combine_norm",
    )(x, ybuf, route, modv, g)


def _rope_tables(s_len, n_ctx):
    n_rows = s_len // GRID_W
    rows, cols = jnp.meshgrid(jnp.arange(n_rows), jnp.arange(GRID_W), indexing="ij")
    pos = jnp.stack([rows.reshape(-1), cols.reshape(-1)], axis=-1).astype(F32)
    n_freq = HEAD_DIM // 4
    inv = ROPE_THETA ** (-jnp.arange(n_freq, dtype=F32) / n_freq)
    ang = pos[:, :, None] * inv
    cos, sin = jnp.cos(ang), jnp.sin(ang)
    cos64 = jnp.stack([cos, cos], axis=2).reshape(s_len, HEAD_DIM)
    sin64 = jnp.stack([-sin, sin], axis=2).reshape(s_len, HEAD_DIM)
    cos_l = jnp.tile(cos64, (1, LANES // HEAD_DIM))
    sin_l = jnp.tile(sin64, (1, LANES // HEAD_DIM))
    cos_t = jnp.concatenate([jnp.ones((n_ctx, LANES), F32), cos_l], axis=0)
    sin_t = jnp.concatenate([jnp.zeros((n_ctx, LANES), F32), sin_l], axis=0)
    return cos_t, sin_t


def kernel(x, c, ctx, c_ctx, w_mod, b_mod, norm_mix, norm_ffn, ev_w_in, ev_w_out, ev_lambda_q1, ev_lambda_k1, ev_lambda_q2, ev_lambda_k2, ev_subln, ev_conv_w, od_w_in, od_w_out, od_conv_w, od_conv_b, od_gate_a_w, od_gate_a_b, od_gate_x_w, od_gate_x_b, od_lru_lambda, moe_w_router, moe_b_router, moe_w1, moe_b1, moe_w2, moe_b2, final_norm):
    b, s_len, d = x.shape
    n_ctx = ctx.shape[1]
    l = n_ctx + s_len
    t = b * l
    ntb = l // TM
    assert n_ctx == TM and s_len % TM == 0 and w_mod.shape[0] == 2
    assert (t * TOP_K) % POS_CHUNK == 0 and (b * s_len * TOP_K) % POS_CHUNK == 0

    xs = jnp.concatenate([ctx, x], axis=1).reshape(t, d)

    n_rows = -(-(b + 1) // SUBLANES) * SUBLANES
    cs = jnp.concatenate([c, c_ctx[None, :], jnp.zeros((n_rows - b - 1, d), F32)], axis=0)
    mod = _modulation(cs, w_mod, b_mod)

    def mod_table(i):
        lat = mod[i, :b]
        cx = jnp.broadcast_to(mod[i, b][None, :], lat.shape)
        return jnp.stack([cx, lat], axis=1).reshape(b * 2, 6, d)

    modv0, modv1 = mod_table(0), mod_table(1)

    cos_t, sin_t = _rope_tables(s_len, n_ctx)
    q, k, v, bg, p = _in_proj0(xs, modv0, norm_mix[0:1], ev_w_in[0].astype(BF16), cos_t, sin_t, ntb)
    lam_init = 0.8 - 0.6 * math.exp(-0.3 * 0)
    attn = _diff_attention(q, k, v, ev_lambda_q1[0:1], ev_lambda_k1[0:1], ev_lambda_q2[0:1], ev_lambda_k2[0:1],
                           ev_subln[0:1], lam_init, b, l, n_ctx)
    x1, h2, route0, counts0 = _out_proj0(attn, bg, p, ev_conv_w[0], ev_w_out[0].astype(BF16), xs, modv0,
                                         norm_ffn[0:1], moe_w_router[0], moe_b_router[0:1], ntb)
    ybuf0 = _moe_layer(route0, counts0, h2, moe_w1, moe_b1, moe_w2, moe_b2, 0)

    x2, gg, u = _in_proj1(x1, ybuf0, route0, modv0, modv1, norm_mix[1:2], od_w_in[0].astype(BF16), ntb)
    hs = _rglru(u, od_conv_w[0], od_conv_b[0], od_gate_a_w[0], od_gate_a_b[0], od_gate_x_w[0], od_gate_x_b[0],
                od_lru_lambda[0], b, l)
    x3, h3, route1, counts1 = _out_proj1(hs.reshape(2, t, hs.shape[-1]), gg, od_w_out[0].astype(BF16), x2, modv1,
                                         norm_ffn[1:2], moe_w_router[1], moe_b_router[1:2], ntb)
    ybuf1 = _moe_layer(route1, counts1, h3, moe_w1, moe_b1, moe_w2, moe_b2, 1)

    out = _final(x3, ybuf1, route1, modv1, final_norm[None, :], b, s_len)
    return out.reshape(b, s_len, d)
```

```python
import functools
import math

import jax
import jax.numpy as jnp
from jax import lax
from jax.experimental import pallas as pl
from jax.experimental.pallas import tpu as pltpu

F32 = jnp.float32
BF16 = jnp.bfloat16
I32 = jnp.int32

NORM_EPS = 1e-6
ROPE_THETA = 10000.0
GRID_W = 64
N_HEADS = 4
HEAD_DIM = 64
A_WIDTH = 2 * N_HEADS * HEAD_DIM
B_CONV = 3
C_CONV = 4
C_BLOCKS = 4
LRU_C = 8.0
TOP_K = 4
SWIGLU_ALPHA = 1.702
SWIGLU_LIMIT = 7.0

LANES = 128
SUBLANES = 8
TM = 256
TME = 256
POS_CHUNK = 8192
INV_UNROLL = 16
KEY_SHIFT = 15
KEY_MASK = (1 << KEY_SHIFT) - 1
VMEM_LIMIT = 56 * 1024 * 1024


def _cparams(sem, vmem=VMEM_LIMIT):
    return pltpu.CompilerParams(dimension_semantics=sem, vmem_limit_bytes=vmem)


def _norm_mod(x, g, shift, scale):
    ms = jnp.mean(x * x, axis=-1, keepdims=True)
    return (x * lax.rsqrt(ms + NORM_EPS) * g) * (1.0 + scale) + shift


def _dot(a, b):
    return jnp.dot(a, b, preferred_element_type=F32)


def _as_i32(pattern):
    return (pattern + (1 << 31)) % (1 << 32) - (1 << 31)


def _mod_kernel(cs_ref, w_ref, b_ref, o_ref):
    s = cs_ref[...]
    s = s * jax.nn.sigmoid(s)
    o_ref[0] = _dot(s.astype(BF16), w_ref[0].astype(BF16)) + b_ref[0]


def _modulation(cs, w_mod, b_mod):
    depth, d, n = w_mod.shape
    rows = cs.shape[0]
    tn = 1536
    return pl.pallas_call(
        _mod_kernel,
        grid=(depth, n // tn),
        in_specs=[
            pl.BlockSpec((rows, d), lambda i, j: (0, 0)),
            pl.BlockSpec((1, d, tn), lambda i, j: (i, 0, j)),
            pl.BlockSpec((1, 1, tn), lambda i, j: (i, 0, j)),
        ],
        out_specs=pl.BlockSpec((1, rows, tn), lambda i, j: (i, 0, j)),
        out_shape=jax.ShapeDtypeStruct((depth, rows, n), F32),
        compiler_params=_cparams(("parallel", "parallel")),
        name="adaln_modulation",
    )(cs, w_mod, b_mod.reshape(depth, 1, n))


def _in0_kernel(x_ref, mod_ref, g_ref, w_ref, cos_ref, sin_ref,
                q_ref, k_ref, v_ref, bg_ref, p_ref):
    mod = mod_ref[0]
    h = _norm_mod(x_ref[...], g_ref[...], mod[0:1], mod[1:2])
    y = _dot(h.astype(BF16), w_ref[...])
    cosv = cos_ref[...]
    sinv = sin_ref[...]
    lane = lax.broadcasted_iota(I32, (TM, LANES), 1)
    first_half = (lane & 16) == 0

    def rope(z):
        outs = []
        for g in range(A_WIDTH // LANES):
            zg = z[:, g * LANES:(g + 1) * LANES]
            partner = jnp.where(first_half, pltpu.roll(zg, LANES - 16, 1), pltpu.roll(zg, 16, 1))
            outs.append(zg * cosv + partner * sinv)
        return jnp.concatenate(outs, axis=1)

    aw = A_WIDTH
    q_ref[...] = (rope(y[:, :aw]) * (HEAD_DIM ** -0.5)).astype(BF16)
    k_ref[...] = rope(y[:, aw:2 * aw]).astype(BF16)
    v_ref[...] = y[:, 2 * aw:3 * aw].astype(BF16)
    bw = (y.shape[1] - 3 * aw) // 3
    bg_ref[...] = y[:, 3 * aw:3 * aw + bw].astype(BF16)
    p_ref[...] = (y[:, 3 * aw + bw:3 * aw + 2 * bw] * y[:, 3 * aw + 2 * bw:]).astype(BF16)


def _in_proj0(x, modv, g, w, cos_t, sin_t, ntb):
    t, d = x.shape
    n = w.shape[1]
    bw = (n - 3 * A_WIDTH) // 3
    row = lambda i: (i, 0)
    return pl.pallas_call(
        _in0_kernel,
        grid=(t // TM,),
        in_specs=[
            pl.BlockSpec((TM, d), row),
            pl.BlockSpec((1, 6, d), lambda i: ((i // ntb) * 2 + jnp.minimum(i % ntb, 1), 0, 0)),
            pl.BlockSpec((1, d), lambda i: (0, 0)),
            pl.BlockSpec((d, n), lambda i: (0, 0)),
            pl.BlockSpec((TM, LANES), lambda i: (i % ntb, 0)),
            pl.BlockSpec((TM, LANES), lambda i: (i % ntb, 0)),
        ],
        out_specs=[pl.BlockSpec((TM, A_WIDTH), row)] * 3 + [pl.BlockSpec((TM, bw), row)] * 2,
        out_shape=[jax.ShapeDtypeStruct((t, A_WIDTH), BF16)] * 3 + [jax.ShapeDtypeStruct((t, bw), BF16)] * 2,
        compiler_params=_cparams(("parallel",)),
        name="l0_in_proj_rope",
    )(x, modv, g, w, cos_t, sin_t)


def _attn_kernel(lq1_ref, lk1_ref, lq2_ref, lk2_ref, g_ref, q_ref, k_ref, v_ref, o_ref, *, lam_init, n_ctx):
    qi = pl.program_id(2)
    lam = (jnp.exp(jnp.sum(lq1_ref[...] * lk1_ref[...], axis=-1, keepdims=True))
           - jnp.exp(jnp.sum(lq2_ref[...] * lk2_ref[...], axis=-1, keepdims=True)) + lam_init)
    q = q_ref[...]
    lane = lax.broadcasted_iota(I32, q.shape, 1)
    zero = jnp.zeros_like(q)
    q1 = jnp.where(lane < HEAD_DIM, q, zero)
    q2 = jnp.where(lane < HEAD_DIM, zero, q)
    contract_last = (((1,), (1,)), ((), ()))

    def attend(nk):
        k = k_ref[0, :nk, :]
        v = v_ref[0, :nk, :]
        s1 = lax.dot_general(q1, k, contract_last, preferred_element_type=F32)
        s2 = lax.dot_general(q2, k, contract_last, preferred_element_type=F32)
        p1 = jnp.exp(s1 - jnp.max(s1, axis=-1, keepdims=True))
        p2 = jnp.exp(s2 - jnp.max(s2, axis=-1, keepdims=True))
        r1 = 1.0 / jnp.sum(p1, axis=-1, keepdims=True)
        r2 = lam / jnp.sum(p2, axis=-1, keepdims=True)
        w = p1 * r1 - p2 * r2
        o = _dot(w.astype(BF16), v)
        ms = jnp.mean(o * o, axis=-1, keepdims=True)
        o = o * lax.rsqrt(ms + NORM_EPS) * g_ref[...] * (1.0 - lam_init)
        o_ref[...] = o.astype(BF16)

    @pl.when(qi == 0)
    def _():
        attend(n_ctx)

    @pl.when(qi > 0)
    def _():
        attend(k_ref.shape[1])


def _diff_attention(q, k, v, lq1, lk1, lq2, lk2, subln, lam_init, b, l, n_ctx):
    t = q.shape[0]
    ntb = l // TM
    hw = 2 * HEAD_DIM
    k3 = k.reshape(b, l, A_WIDTH)
    v3 = v.reshape(b, l, A_WIDTH)
    vec = lambda n: pl.BlockSpec((1, n), lambda bi, h, qi: (0, 0))
    qspec = pl.BlockSpec((TM, hw), lambda bi, h, qi: (bi * ntb + qi, h))
    kspec = pl.BlockSpec((1, l, hw), lambda bi, h, qi: (bi, 0, h))
    return pl.pallas_call(
        functools.partial(_attn_kernel, lam_init=lam_init, n_ctx=n_ctx),
        grid=(b, N_HEADS, ntb),
        in_specs=[vec(HEAD_DIM)] * 4 + [vec(hw), qspec, kspec, kspec],
        out_specs=qspec,
        out_shape=jax.ShapeDtypeStruct((t, A_WIDTH), BF16),
        compiler_params=_cparams(("parallel", "parallel", "parallel")),
        name="l0_diff_attention",
    )(lq1, lk1, lq2, lk2, subln, q, k3, v3)


def _route_tail(y, x_ref, mod_ref, g_ref, wr_ref, br_ref,
                x1_ref, h2_ref, route_ref, cnt_ref, carry_ref):
    mod = mod_ref[0]
    x1 = x_ref[...] + mod[2:3] * y
    x1_ref[...] = x1
    h2 = _norm_mod(x1, g_ref[...], mod[3:4], mod[4:5])
    nch = h2.shape[1] // LANES
    for s in range(nch):
        h2_ref[pl.ds(s, TM, stride=nch), :] = h2[:, s * LANES:(s + 1) * LANES]

    wr = wr_ref[...]
    h_hi = h2.astype(BF16)
    h_lo = (h2 - h_hi.astype(F32)).astype(BF16)
    w_hi = wr.astype(BF16)
    w_lo = (wr - w_hi.astype(F32)).astype(BF16)
    logits = _dot(h_hi, w_hi) + _dot(h_hi, w_lo) + _dot(h_lo, w_hi) + br_ref[...]

    n_exp = logits.shape[1]
    lane = lax.broadcasted_iota(I32, logits.shape, 1).astype(F32)
    work = logits
    sels, vals, idxs = [], [], []
    for _ in range(TOP_K):
        m = jnp.max(work, axis=-1, keepdims=True)
        idx = jnp.min(jnp.where(work == m, lane, float(n_exp)), axis=-1, keepdims=True)
        sel = lane == idx
        sels.append(sel)
        vals.append(m)
        idxs.append(idx)
        work = jnp.where(sel, -jnp.inf, work)
    exps = [jnp.exp(vv - vals[0]) for vv in vals]
    inv_den = 1.0 / (exps[0] + exps[1] + exps[2] + exps[3])

    chosen = jnp.zeros(logits.shape, F32)
    for sel in sels:
        chosen = chosen + jnp.where(sel, 1.0, 0.0)
    r_i = lax.broadcasted_iota(I32, (TM, TM), 0)
    c_i = lax.broadcasted_iota(I32, (TM, TM), 1)
    earlier = jnp.where(c_i < r_i, 1.0, 0.0).astype(BF16)
    rank = _dot(earlier, chosen.astype(BF16)) + carry_ref[...]
    carry_ref[...] = carry_ref[...] + jnp.sum(chosen, axis=0, keepdims=True)
    cnt_ref[...] = carry_ref[...]

    out_lane = lax.broadcasted_iota(I32, (TM, LANES), 1)
    packed = jnp.zeros((TM, LANES), F32)
    for kk in range(TOP_K):
        rank_k = jnp.sum(jnp.where(sels[kk], rank, 0.0), axis=-1, keepdims=True)
        packed = jnp.where(out_lane == kk, idxs[kk], packed)
        packed = jnp.where(out_lane == TOP_K + kk, rank_k, packed)
        packed = jnp.where(out_lane == 2 * TOP_K + kk, exps[kk] * inv_den, packed)
    route_ref[...] = packed


def _out0_kernel(attn_ref, bg_ref, p_ref, pprev_ref, pnext_ref, cw_ref, wo_ref,
                 x_ref, mod_ref, g_ref, wr_ref, br_ref,
                 x1_ref, h2_ref, route_ref, cnt_ref, carry_ref, *, ntb):
    i = pl.program_id(0)
    seg = i % ntb

    @pl.when(i == 0)
    def _():
        carry_ref[...] = jnp.zeros_like(carry_ref)

    p = p_ref[...].astype(F32)
    row = lax.broadcasted_iota(I32, p.shape, 0)
    has_prev = seg > 1
    has_next = jnp.logical_and(seg > 0, seg < ntb - 1)
    prev_row = jnp.where(has_prev, pprev_ref[SUBLANES - 1:SUBLANES, :].astype(F32), 0.0)
    next_row = jnp.where(has_next, pnext_ref[0:1, :].astype(F32), 0.0)
    before = jnp.where(row == 0, prev_row, pltpu.roll(p, 1, 0))
    after = jnp.where(row == TM - 1, next_row, pltpu.roll(p, TM - 1, 0))
    cw = cw_ref[...]
    conv = bg_ref[...].astype(F32) * (cw[0:1] * before + cw[1:2] * p + cw[2:3] * after)
    aw = attn_ref.shape[1]
    y = _dot(attn_ref[...], wo_ref[:aw, :]) + _dot(conv.astype(BF16), wo_ref[aw:, :])
    _route_tail(y, x_ref, mod_ref, g_ref, wr_ref, br_ref, x1_ref, h2_ref, route_ref, cnt_ref, carry_ref)


def _out1_kernel(hs_ref, gg_ref, wo_ref, x_ref, mod_ref, g_ref, wr_ref, br_ref,
                 x1_ref, h2_ref, route_ref, cnt_ref, carry_ref, *, ntb):
    i = pl.program_id(0)

    @pl.when(i == 0)
    def _():
        carry_ref[...] = jnp.zeros_like(carry_ref)

    @pl.when(i % ntb > 0)
    def _():
        rec = hs_ref[0].astype(F32) + hs_ref[1].astype(F32)
        y = _dot((rec * gg_ref[...].astype(F32)).astype(BF16), wo_ref[...])
        _route_tail(y, x_ref, mod_ref, g_ref, wr_ref, br_ref, x1_ref, h2_ref, route_ref, cnt_ref, carry_ref)


def _tail_specs(d, n_exp, ntb, latent_only):
    row = lambda i: (i, 0)
    const = lambda i: (0, 0)
    out_row = (lambda i: ((i // ntb) * (ntb - 1) + jnp.maximum(i % ntb - 1, 0), 0)) if latent_only else row
    in_specs = [
        pl.BlockSpec((TM, d), row),
        pl.BlockSpec((1, 6, d), lambda i: ((i // ntb) * 2 + jnp.minimum(i % ntb, 1), 0, 0)),
        pl.BlockSpec((1, d), const),
        pl.BlockSpec((d, n_exp), const),
        pl.BlockSpec((1, n_exp), const),
    ]
    out_specs = [
        pl.BlockSpec((TM, d), out_row),
        pl.BlockSpec((TM * (d // LANES), LANES), out_row),
        pl.BlockSpec((TM, LANES), out_row),
        pl.BlockSpec((1, n_exp), const),
    ]
    return in_specs, out_specs


def _tail_shapes(t, d, n_exp):
    return [jax.ShapeDtypeStruct((t, d), F32), jax.ShapeDtypeStruct((t * (d // LANES), LANES), F32),
            jax.ShapeDtypeStruct((t, LANES), F32), jax.ShapeDtypeStruct((1, n_exp), F32)]


def _out_proj0(attn, bg, p, conv_w, w_out, x, modv, g, w_r, b_r, ntb):
    t, d = x.shape
    n_exp = w_r.shape[1]
    bw = bg.shape[1]
    row = lambda i: (i, 0)
    const = lambda i: (0, 0)
    nblk = TM // SUBLANES
    tail_in, tail_out = _tail_specs(d, n_exp, ntb, False)
    return pl.pallas_call(
        functools.partial(_out0_kernel, ntb=ntb),
        grid=(t // TM,),
        in_specs=[
            pl.BlockSpec((TM, attn.shape[1]), row),
            pl.BlockSpec((TM, bw), row),
            pl.BlockSpec((TM, bw), row),
            pl.BlockSpec((SUBLANES, bw), lambda i: (jnp.maximum(i * nblk - 1, 0), 0)),
            pl.BlockSpec((SUBLANES, bw), lambda i: (jnp.minimum((i + 1) * nblk, t // SUBLANES - 1), 0)),
            pl.BlockSpec(conv_w.shape, const),
            pl.BlockSpec(w_out.shape, const),
        ] + tail_in,
        out_specs=tail_out,
        out_shape=_tail_shapes(t, d, n_exp),
        scratch_shapes=[pltpu.VMEM((1, n_exp), F32)],
        compiler_params=_cparams(("arbitrary",)),
        name="l0_out_proj_router",
    )(attn, bg, p, p, p, conv_w, w_out, x, modv, g, w_r, b_r)


def _out_proj1(hs, gg, w_out, x, modv, g, w_r, b_r, ntb):
    t, d = x.shape
    n_exp = w_r.shape[1]
    row = lambda i: (i, 0)
    const = lambda i: (0, 0)
    tail_in, tail_out = _tail_specs(d, n_exp, ntb, True)
    t_lat = t // ntb * (ntb - 1)
    return pl.pallas_call(
        functools.partial(_out1_kernel, ntb=ntb),
        grid=(t // TM,),
        in_specs=[
            pl.BlockSpec((2, TM, hs.shape[2]), lambda i: (0, i, 0)),
            pl.BlockSpec((TM, gg.shape[1]), row),
            pl.BlockSpec(w_out.shape, const),
        ] + tail_in,
        out_specs=tail_out,
        out_shape=_tail_shapes(t_lat, d, n_exp),
        scratch_shapes=[pltpu.VMEM((1, n_exp), F32)],
        compiler_params=_cparams(("arbitrary",)),
        name="l1_out_proj_router",
    )(hs, gg, w_out, x, modv, g, w_r, b_r)


def _inv_kernel(pad_lo_ref, pad_hi_ref, pos_ref, val_ref, inv_ref, pbuf, vbuf, sem, *, spare_row0):
    c = pl.program_id(0)
    n_chunks = pl.num_programs(0)
    slot = c % 2

    def chunk_copies(ci, s):
        src = pl.ds(ci * POS_CHUNK, POS_CHUNK)
        dst = pl.ds(s * POS_CHUNK, POS_CHUNK)
        return (pltpu.make_async_copy(pos_ref.at[src], pbuf.at[dst], sem.at[0, s]),
                pltpu.make_async_copy(val_ref.at[src], vbuf.at[dst], sem.at[1, s]))

    @pl.when(c == 0)
    def _():
        for cp in chunk_copies(0, 0):
            cp.start()

        def fill_range(e, carry):
            def fill(j, cc):
                sp = j & (2 * TME - 1)
                inv_ref[j] = ((spare_row0 + sp) << KEY_SHIFT) | sp
                return cc
            return lax.fori_loop(pad_lo_ref[e], pad_hi_ref[e], fill, carry)

        lax.fori_loop(0, pad_lo_ref.shape[0], fill_range, 0)

    for cp in chunk_copies(c, slot):
        cp.wait()

    @pl.when(c + 1 < n_chunks)
    def _():
        for cp in chunk_copies(c + 1, 1 - slot):
            cp.start()

    off = slot * POS_CHUNK

    def put(jb, carry):
        j0 = off + jb * INV_UNROLL
        slots = [pbuf[j0 + u] for u in range(INV_UNROLL)]
        vals = [vbuf[j0 + u] for u in range(INV_UNROLL)]
        for u in range(INV_UNROLL):
            inv_ref[slots[u]] = vals[u]
        return carry

    lax.fori_loop(0, POS_CHUNK // INV_UNROLL, put, 0)


def _invert_slots(pos, val, pad_lo, pad_hi, n_inv, spare_row0):
    n_pairs = pos.shape[0]
    grid_spec = pltpu.PrefetchScalarGridSpec(
        num_scalar_prefetch=2,
        grid=(n_pairs // POS_CHUNK,),
        in_specs=[pl.BlockSpec(memory_space=pl.ANY), pl.BlockSpec(memory_space=pl.ANY)],
        out_specs=pl.BlockSpec(memory_space=pltpu.SMEM),
        scratch_shapes=[pltpu.SMEM((2 * POS_CHUNK,), I32), pltpu.SMEM((2 * POS_CHUNK,), I32),
                        pltpu.SemaphoreType.DMA((2, 2))],
    )
    return pl.pallas_call(
        functools.partial(_inv_kernel, spare_row0=spare_row0),
        grid_spec=grid_spec,
        out_shape=jax.ShapeDtypeStruct((n_inv,), I32),
        compiler_params=_cparams(("arbitrary",)),
        name="slot_inversion",
    )(pad_lo, pad_hi, pos, val)


def _moe_kernel(te_ref, meta_ref, inv_ref, h2_ref, w1_ref, b1_ref, w2_ref, b2_ref, y_ref,
                xbuf, obuf, w1b, w2b, gsem, ssem, *, n_tok):
    i = pl.program_id(0)
    n_tiles = meta_ref[0]
    slot = i % 2
    other = 1 - slot
    d = w1_ref.shape[2]
    ff = w2_ref.shape[2]
    nch = d // LANES
    spare0 = TOP_K * n_tok * nch

    def gather_row(entry, s, r):
        src = pl.multiple_of((entry & KEY_MASK) * nch, nch)
        return pltpu.make_async_copy(h2_ref.at[pl.ds(src, nch)], xbuf.at[s, pl.ds(r * nch, nch)], gsem.at[s])

    def scatter_row(entry, s, r):
        dst = pl.multiple_of(lax.shift_right_logical(entry, KEY_SHIFT) * nch, nch)
        return pltpu.make_async_copy(obuf.at[s, pl.ds(r * nch, nch)], y_ref.at[pl.ds(dst, nch)], ssem.at[s])

    def for_rows(fn):
        def body(r8, c):
            for u in range(SUBLANES):
                fn(r8 * SUBLANES + u)
            return c
        lax.fori_loop(0, TME // SUBLANES, body, 0)

    def wait_gather(s):
        pltpu.make_async_copy(h2_ref.at[pl.ds(0, TME * nch)], xbuf.at[s], gsem.at[s]).wait()

    def wait_scatter(s):
        pltpu.make_async_copy(obuf.at[s], y_ref.at[pl.ds(0, TME * nch)], ssem.at[s]).wait()

    @pl.when(i == 0)
    def _():
        for_rows(lambda r: gather_row(inv_ref[r], 0, r).start())
        obuf[...] = jnp.zeros(obuf.shape, obuf.dtype)
        pltpu.make_async_copy(obuf.at[0], y_ref.at[pl.ds(spare0, TME * nch)], ssem.at[0]).start()

    @pl.when(i < n_tiles)
    def _():
        wait_gather(slot)
        new_expert = jnp.logical_or(i == 0, te_ref[i] != te_ref[jnp.maximum(i - 1, 0)])

        @pl.when(new_expert)
        def _():
            w1b[...] = w1_ref[0, 0].astype(BF16)
            w2b[...] = w2_ref[0, 0].astype(BF16)

        nxt = jnp.minimum(i + 1, n_tiles - 1) * TME
        prv = jnp.maximum(i - 1, 0) * TME
        first = i == 0
        for r in range(TME):
            gather_row(inv_ref[nxt + r], other, r).start()
            spare_entry = _as_i32(((TOP_K * n_tok + TME + r) << KEY_SHIFT) | (TME + r))
            prev_entry = jnp.where(first, spare_entry, inv_ref[prv + r])
            scatter_row(prev_entry, other, r).start()

        x = jnp.concatenate([xbuf[slot, pl.ds(c, TME, stride=nch), :] for c in range(nch)], axis=1)
        h = _dot(x.astype(BF16), w1b[...]) + b1_ref[0]
        glu = jnp.minimum(h[:, :ff], SWIGLU_LIMIT)
        lin = jnp.clip(h[:, ff:], -SWIGLU_LIMIT, SWIGLU_LIMIT)
        act = glu * jax.nn.sigmoid(SWIGLU_ALPHA * glu) * (lin + 1.0)
        y = _dot(act.astype(BF16), w2b[...]) + b2_ref[0]

        wait_scatter(slot)
        for c in range(nch):
            obuf[slot, pl.ds(c, TME, stride=nch), :] = y[:, c * LANES:(c + 1) * LANES]

    @pl.when(i == n_tiles)
    def _():
        last = n_tiles - 1
        for_rows(lambda r: scatter_row(inv_ref[last * TME + r], last % 2, r).start())
        wait_gather(n_tiles % 2)
        wait_scatter(0)
        wait_scatter(1)


def _routed_experts(tile_expert, meta, inv, h2, w1, b1, w2, b2, layer):
    _, n_exp, d, ff2 = w1.shape
    n_tok = h2.shape[0] // (d // LANES)
    assert n_tok <= KEY_MASK + 1 and 2 * TME <= n_tok and (TOP_K * n_tok + 2 * TME) < (1 << (32 - KEY_SHIFT))
    ff = w2.shape[2]
    nch = d // LANES
    n_grid = tile_expert.shape[0]
    tile = lambda i: jnp.minimum(i, n_grid - 1)
    wmap = lambda i, te, mt, iv: (layer, te[tile(i)], 0, 0)
    bmap = lambda i, te, mt, iv: (layer * n_exp + te[tile(i)], 0, 0)
    grid_spec = pltpu.PrefetchScalarGridSpec(
        num_scalar_prefetch=3,
        grid=(n_grid + 1,),
        in_specs=[
            pl.BlockSpec(memory_space=pl.ANY),
            pl.BlockSpec((1, 1, d, ff2), wmap),
            pl.BlockSpec((1, 1, ff2), bmap),
            pl.BlockSpec((1, 1, ff, d), wmap),
            pl.BlockSpec((1, 1, d), bmap),
        ],
        out_specs=pl.BlockSpec(memory_space=pl.ANY),
        scratch_shapes=[
            pltpu.VMEM((2, TME * nch, LANES), F32),
            pltpu.VMEM((2, TME * nch, LANES), F32),
            pltpu.VMEM((d, ff2), BF16),
            pltpu.VMEM((ff, d), BF16),
            pltpu.SemaphoreType.DMA((2,)),
            pltpu.SemaphoreType.DMA((2,)),
        ],
    )
    return pl.pallas_call(
        functools.partial(_moe_kernel, n_tok=n_tok),
        grid_spec=grid_spec,
        out_shape=jax.ShapeDtypeStruct(((TOP_K * n_tok + 2 * TME) * nch, LANES), F32),
        compiler_params=_cparams(("arbitrary",)),
        name="routed_experts",
    )(tile_expert, meta, inv, h2, w1, b1.reshape(-1, 1, ff2), w2, b2.reshape(-1, 1, d))


def _plan_routes(route, counts, n_grid):
    n_exp = counts.shape[-1]
    cnt = counts.reshape(n_exp).astype(I32)
    tiles_per = (cnt + TME - 1) // TME
    tile_end = jnp.cumsum(tiles_per)
    offset = (tile_end - tiles_per) * TME
    n_tiles = tile_end[-1]
    tile_ids = jnp.minimum(jnp.arange(n_grid, dtype=I32), n_tiles - 1)
    tile_expert = jnp.sum((tile_ids[:, None] >= tile_end[None, :]).astype(I32), axis=1)
    eidx = route[:, :TOP_K].astype(I32)
    rank = route[:, TOP_K:2 * TOP_K].astype(I32)
    expert_ids = jnp.arange(n_exp, dtype=I32)
    pos = jnp.sum(jnp.where(eidx[..., None] == expert_ids, offset, 0), axis=-1) + rank
    n_tok = route.shape[0]
    tok = jnp.arange(n_tok, dtype=jnp.uint32)[:, None]
    choice = jnp.arange(TOP_K, dtype=jnp.uint32)[None, :]
    val = lax.bitcast_convert_type(((choice * n_tok + tok) << KEY_SHIFT) | tok, I32)
    pad_lo = jnp.concatenate([offset + cnt, (n_tiles * TME).reshape(1)])
    pad_hi = jnp.concatenate([tile_end * TME, jnp.full((1,), n_grid * TME, I32)])
    return tile_expert, n_tiles.reshape(1), pos.reshape(-1), val.reshape(-1), pad_lo, pad_hi


def _moe_layer(route, counts, h2, w1, b1, w2, b2, layer):
    n_exp = w1.shape[1]
    n_pairs = route.shape[0] * TOP_K
    n_grid = n_pairs // TME + n_exp
    tile_expert, meta, pos, val, pad_lo, pad_hi = _plan_routes(route, counts, n_grid)
    inv = _invert_slots(pos, val, pad_lo, pad_hi, n_grid * TME, TOP_K * route.shape[0])
    return _routed_experts(tile_expert, meta, inv, h2, w1, b1, w2, b2, layer)


def _combine(y_refs, route_ref, d):
    nch = d // LANES
    gates = route_ref[...]
    gk = [jnp.broadcast_to(gates[:, 2 * TOP_K + kk:2 * TOP_K + kk + 1], (TM, LANES)) for kk in range(TOP_K)]
    chunks = []
    for c in range(nch):
        acc = gk[0] * y_refs[0][pl.ds(c, TM, stride=nch), :]
        for kk in range(1, TOP_K):
            acc = acc + gk[kk] * y_refs[kk][pl.ds(c, TM, stride=nch), :]
        chunks.append(acc)
    return jnp.concatenate(chunks, axis=1)


def _choice_specs(n_tok, d):
    per = n_tok // TM
    return [pl.BlockSpec((TM * (d // LANES), LANES), functools.partial(lambda i, kk: (kk * per + i, 0), kk=kk))
            for kk in range(TOP_K)]


def _in1_kernel(x_ref, y0_ref, y1_ref, y2_ref, y3_ref, route_ref, mod0_ref, mod1_ref, g_ref, w_ref,
                x2_ref, gg_ref, u_ref):
    x2 = x_ref[...] + mod0_ref[0][5:6] * _combine((y0_ref, y1_ref, y2_ref, y3_ref), route_ref, x_ref.shape[1])
    x2_ref[...] = x2
    mod1 = mod1_ref[0]
    h = _norm_mod(x2, g_ref[...], mod1[0:1], mod1[1:2])
    y = _dot(h.astype(BF16), w_ref[...])
    half = y.shape[1] // 2
    gg_ref[...] = jax.nn.gelu(y[:, :half]).astype(BF16)
    u_ref[...] = y[:, half:]


def _in_proj1(x, ybuf, route, modv0, modv1, g, w, ntb):
    t, d = x.shape
    n = w.shape[1]
    row = lambda i: (i, 0)
    mod_map = lambda i: ((i // ntb) * 2 + jnp.minimum(i % ntb, 1), 0, 0)
    return pl.pallas_call(
        _in1_kernel,
        grid=(t // TM,),
        in_specs=[
            pl.BlockSpec((TM, d), row),
        ] + _choice_specs(t, d) + [
            pl.BlockSpec((TM, LANES), row),
            pl.BlockSpec((1, 6, d), mod_map),
            pl.BlockSpec((1, 6, d), mod_map),
            pl.BlockSpec((1, d), lambda i: (0, 0)),
            pl.BlockSpec((d, n), lambda i: (0, 0)),
        ],
        out_specs=[pl.BlockSpec((TM, d), row), pl.BlockSpec((TM, n // 2), row), pl.BlockSpec((TM, n // 2), row)],
        out_shape=[jax.ShapeDtypeStruct((t, d), F32), jax.ShapeDtypeStruct((t, n // 2), BF16),
                   jax.ShapeDtypeStruct((t, n // 2), F32)],
        compiler_params=_cparams(("parallel",)),
        name="l1_combine_in_proj",
    )(x, ybuf, ybuf, ybuf, ybuf, route, modv0, modv1, g, w)


def _rglru_kernel(u_ref, halo_ref, cw_ref, cb_ref, gaw_ref, gab_ref, gxw_ref, gxb_ref, lam_ref, o_ref,
                  ext, a_s, b_s, h_s, state, *, ntb):
    d = pl.program_id(0)
    s = pl.program_id(2)
    nb, tc, cw_ = u_ref.shape
    u = u_ref[...]
    cw = cw_ref[0]

    def finish(xc, reverse):
        xc2 = xc.reshape(nb * tc, cw_) + cb_ref[0]
        xb = xc2.astype(BF16)
        r = jax.nn.sigmoid(_dot(xb, gaw_ref[0, 0]) + gab_ref[0])
        gi = jax.nn.sigmoid(_dot(xb, gxw_ref[0, 0]) + gxb_ref[0])
        nl = -lam_ref[0]
        softplus = jnp.maximum(nl, 0.0) + jnp.log1p(jnp.exp(-jnp.abs(nl)))
        log_a = (-LRU_C) * r * softplus
        a = jnp.exp(log_a)
        bb = jnp.sqrt(1.0 - a * a) * (gi * xc2)
        n_lane = cw_ // LANES
        for c in range(n_lane):
            a_s[c] = a[:, c * LANES:(c + 1) * LANES]
            b_s[c] = bb[:, c * LANES:(c + 1) * LANES]

        @pl.when(s == 0)
        def _():
            state[...] = jnp.zeros_like(state)

        def step(tt, hs):
            t = (tc - 1 - tt) if reverse else tt
            rows = pl.ds(t, nb, stride=tc)
            out = []
            for c in range(n_lane):
                h = a_s[c, rows, :] * hs[c] + b_s[c, rows, :]
                h_s[c, rows, :] = h
                out.append(h)
            return tuple(out)

        hs = lax.fori_loop(0, tc, step, tuple(state[c] for c in range(n_lane)))
        for c in range(n_lane):
            state[c] = hs[c]
            o_ref[0, :, :, c * LANES:(c + 1) * LANES] = h_s[c].reshape(nb, tc, LANES).astype(o_ref.dtype)

    @pl.when(d == 0)
    def _():
        chunk = s
        keep = chunk > 1
        ext[:, 0:SUBLANES, :] = jnp.where(keep, halo_ref[...], 0.0)
        ext[:, SUBLANES:, :] = u
        xc = cw[3:4] * u
        for j in range(1, C_CONV):
            xc = xc + cw[3 - j:4 - j] * ext[:, SUBLANES - j:SUBLANES - j + tc, :]
        finish(xc, False)

    @pl.when(d == 1)
    def _():
        chunk = jnp.where(s == 0, 0, ntb - s)
        keep = jnp.logical_and(chunk > 0, chunk < ntb - 1)
        ext[:, 0:tc, :] = u
        ext[:, tc:, :] = jnp.where(keep, halo_ref[...], 0.0)
        xc = cw[0:1] * u
        for j in range(1, C_CONV):
            xc = xc + cw[j:j + 1] * ext[:, j:j + tc, :]
        finish(xc, True)


def _rglru(u, conv_w, conv_b, ga_w, ga_b, gx_w, gx_b, lam, b, l):
    width = u.shape[1]
    cb = width // C_BLOCKS
    ntb = l // TM
    u3 = u.reshape(b, l, width)
    nblk = TM // SUBLANES

    def chunk_of(d, s):
        return jnp.where(d == 0, s, jnp.where(s == 0, 0, ntb - s))

    def halo_of(d, s):
        c = chunk_of(d, s)
        return jnp.where(d == 0, jnp.maximum(c * nblk - 1, 0), jnp.minimum((c + 1) * nblk, l // SUBLANES - 1))

    vec = pl.BlockSpec((1, 1, cb), lambda d, g, s: (d, 0, g))
    mat = pl.BlockSpec((1, 1, cb, cb), lambda d, g, s: (d, g, 0, 0))
    return pl.pallas_call(
        functools.partial(_rglru_kernel, ntb=ntb),
        grid=(2, C_BLOCKS, ntb),
        in_specs=[
            pl.BlockSpec((b, TM, cb), lambda d, g, s: (0, chunk_of(d, s), g)),
            pl.BlockSpec((b, SUBLANES, cb), lambda d, g, s: (0, halo_of(d, s), g)),
            pl.BlockSpec((1, C_CONV, cb), lambda d, g, s: (d, 0, g)),
            vec, mat, vec, mat, vec, vec,
        ],
        out_specs=pl.BlockSpec((1, b, TM, cb), lambda d, g, s: (d, 0, chunk_of(d, s), g)),
        out_shape=jax.ShapeDtypeStruct((2, b, l, width), BF16),
        scratch_shapes=[
            pltpu.VMEM((b, TM + SUBLANES, cb), F32),
            pltpu.VMEM((cb // LANES, b * TM, LANES), F32),
            pltpu.VMEM((cb // LANES, b * TM, LANES), F32),
            pltpu.VMEM((cb // LANES, b * TM, LANES), F32),
            pltpu.VMEM((cb // LANES, b, LANES), F32),
        ],
        compiler_params=_cparams(("arbitrary", "arbitrary", "arbitrary")),
        name="l1_rglru",
    )(u3, u3, conv_w, conv_b.reshape(2, 1, width), ga_w.astype(BF16), ga_b.reshape(2, 1, width),
      gx_w.astype(BF16), gx_b.reshape(2, 1, width), lam.reshape(2, 1, width))


def _final_kernel(x_ref, y0_ref, y1_ref, y2_ref, y3_ref, route_ref, mod_ref, g_ref, o_ref):
    x = x_ref[...] + mod_ref[0][5:6] * _combine((y0_ref, y1_ref, y2_ref, y3_ref), route_ref, x_ref.shape[1])
    ms = jnp.mean(x * x, axis=-1, keepdims=True)
    o_ref[...] = x * lax.rsqrt(ms + NORM_EPS) * g_ref[...]


def _final(x, ybuf, route, modv, g, b, s_len):
    t, d = x.shape
    per_b = s_len // TM
    row = lambda i: (i, 0)
    return pl.pallas_call(
        _final_kernel,
        grid=(t // TM,),
        in_specs=[
            pl.BlockSpec((TM, d), row),
        ] + _choice_specs(t, d) + [
            pl.BlockSpec((TM, LANES), row),
            pl.BlockSpec((1, 6, d), lambda i: ((i // per_b) * 2 + 1, 0, 0)),
            pl.BlockSpec((1, d), lambda i: (0, 0)),
        ],
        out_specs=pl.BlockSpec((TM, d), row),
        out_shape=jax.ShapeDtypeStruct((t, d), F32),
        compiler_params=_cparams(("parallel",)),
        name="final_combine_norm",
    )(x, ybuf, ybuf, ybuf, ybuf, route, modv, g)


def _rope_tables(s_len, n_ctx):
    n_rows = s_len // GRID_W
    rows, cols = jnp.meshgrid(jnp.arange(n_rows), jnp.arange(GRID_W), indexing="ij")
    pos = jnp.stack([rows.reshape(-1), cols.reshape(-1)], axis=-1).astype(F32)
    n_freq = HEAD_DIM // 4
    inv = ROPE_THETA ** (-jnp.arange(n_freq, dtype=F32) / n_freq)
    ang = pos[:, :, None] * inv
    cos, sin = jnp.cos(ang), jnp.sin(ang)
    cos64 = jnp.stack([cos, cos], axis=2).reshape(s_len, HEAD_DIM)
    sin64 = jnp.stack([-sin, sin], axis=2).reshape(s_len, HEAD_DIM)
    cos_l = jnp.tile(cos64, (1, LANES // HEAD_DIM))
    sin_l = jnp.tile(sin64, (1, LANES // HEAD_DIM))
    cos_t = jnp.concatenate([jnp.ones((n_ctx, LANES), F32), cos_l], axis=0)
    sin_t = jnp.concatenate([jnp.zeros((n_ctx, LANES), F32), sin_l], axis=0)
    return cos_t, sin_t


def kernel(x, c, ctx, c_ctx, w_mod, b_mod, norm_mix, norm_ffn, ev_w_in, ev_w_out, ev_lambda_q1, ev_lambda_k1, ev_lambda_q2, ev_lambda_k2, ev_subln, ev_conv_w, od_w_in, od_w_out, od_conv_w, od_conv_b, od_gate_a_w, od_gate_a_b, od_gate_x_w, od_gate_x_b, od_lru_lambda, moe_w_router, moe_b_router, moe_w1, moe_b1, moe_w2, moe_b2, final_norm):
    b, s_len, d = x.shape
    n_ctx = ctx.shape[1]
    l = n_ctx + s_len
    t = b * l
    ntb = l // TM
    assert n_ctx == TM and s_len % TM == 0 and w_mod.shape[0] == 2
    assert (t * TOP_K) % POS_CHUNK == 0 and (b * s_len * TOP_K) % POS_CHUNK == 0

    xs = jnp.concatenate([ctx, x], axis=1).reshape(t, d)

    n_rows = -(-(b + 1) // SUBLANES) * SUBLANES
    cs = jnp.concatenate([c, c_ctx[None, :], jnp.zeros((n_rows - b - 1, d), F32)], axis=0)
    mod = _modulation(cs, w_mod, b_mod)

    def mod_table(i):
        lat = mod[i, :b]
        cx = jnp.broadcast_to(mod[i, b][None, :], lat.shape)
        return jnp.stack([cx, lat], axis=1).reshape(b * 2, 6, d)

    modv0, modv1 = mod_table(0), mod_table(1)

    cos_t, sin_t = _rope_tables(s_len, n_ctx)
    q, k, v, bg, p = _in_proj0(xs, modv0, norm_mix[0:1], ev_w_in[0].astype(BF16), cos_t, sin_t, ntb)
    lam_init = 0.8 - 0.6 * math.exp(-0.3 * 0)
    attn = _diff_attention(q, k, v, ev_lambda_q1[0:1], ev_lambda_k1[0:1], ev_lambda_q2[0:1], ev_lambda_k2[0:1],
                           ev_subln[0:1], lam_init, b, l, n_ctx)
    x1, h2, route0, counts0 = _out_proj0(attn, bg, p, ev_conv_w[0], ev_w_out[0].astype(BF16), xs, modv0,
                                         norm_ffn[0:1], moe_w_router[0], moe_b_router[0:1], ntb)
    ybuf0 = _moe_layer(route0, counts0, h2, moe_w1, moe_b1, moe_w2, moe_b2, 0)

    x2, gg, u = _in_proj1(x1, ybuf0, route0, modv0, modv1, norm_mix[1:2], od_w_in[0].astype(BF16), ntb)
    hs = _rglru(u, od_conv_w[0], od_conv_b[0], od_gate_a_w[0], od_gate_a_b[0], od_gate_x_w[0], od_gate_x_b[0],
                od_lru_lambda[0], b, l)
    x3, h3, route1, counts1 = _out_proj1(hs.reshape(2, t, hs.shape[-1]), gg, od_w_out[0].astype(BF16), x2, modv1,
                                         norm_ffn[1:2], moe_w_router[1], moe_b_router[1:2], ntb)
    ybuf1 = _moe_layer(route1, counts1, h3, moe_w1, moe_b1, moe_w2, moe_b2, 1)

    out = _final(x3, ybuf1, route1, modv1, final_norm[None, :], b, s_len)
    return out.reshape(b, s_len, d)
```

```python
import functools
import math

import jax
import jax.numpy as jnp
from jax import lax
from jax.experimental import pallas as pl
from jax.experimental.pallas import tpu as pltpu

F32 = jnp.float32
BF16 = jnp.bfloat16
I32 = jnp.int32

NORM_EPS = 1e-6
ROPE_THETA = 10000.0
GRID_W = 64
N_HEADS = 4
HEAD_DIM = 64
A_WIDTH = 2 * N_HEADS * HEAD_DIM
B_CONV = 3
C_CONV = 4
C_BLOCKS = 4
LRU_C = 8.0
TOP_K = 4
SWIGLU_ALPHA = 1.702
SWIGLU_LIMIT = 7.0

LANES = 128
SUBLANES = 8
TM = 256
TME = 256
POS_CHUNK = 8192
INV_UNROLL = 16
KEY_SHIFT = 15
KEY_MASK = (1 << KEY_SHIFT) - 1
VMEM_LIMIT = 56 * 1024 * 1024
SCAN_PAD = 8
SCAN_UNROLL = 8
LOG2_E = 1.4426950408889634


def _cparams(sem, vmem=VMEM_LIMIT):
    return pltpu.CompilerParams(dimension_semantics=sem, vmem_limit_bytes=vmem)


def _norm_mod(x, g, shift, scale):
    ms = jnp.mean(x * x, axis=-1, keepdims=True)
    return (x * lax.rsqrt(ms + NORM_EPS) * g) * (1.0 + scale) + shift


def _dot(a, b):
    return jnp.dot(a, b, preferred_element_type=F32)


def _sigmoid(x):
    return 0.5 * jnp.tanh(0.5 * x) + 0.5


def _as_i32(pattern):
    return (pattern + (1 << 31)) % (1 << 32) - (1 << 31)


def _mod_kernel(cs_ref, w_ref, b_ref, o_ref):
    s = cs_ref[...]
    s = s * jax.nn.sigmoid(s)
    o_ref[0] = _dot(s.astype(BF16), w_ref[0].astype(BF16)) + b_ref[0]


def _modulation(cs, w_mod, b_mod):
    depth, d, n = w_mod.shape
    rows = cs.shape[0]
    tn = 1536
    return pl.pallas_call(
        _mod_kernel,
        grid=(depth, n // tn),
        in_specs=[
            pl.BlockSpec((rows, d), lambda i, j: (0, 0)),
            pl.BlockSpec((1, d, tn), lambda i, j: (i, 0, j)),
            pl.BlockSpec((1, 1, tn), lambda i, j: (i, 0, j)),
        ],
        out_specs=pl.BlockSpec((1, rows, tn), lambda i, j: (i, 0, j)),
        out_shape=jax.ShapeDtypeStruct((depth, rows, n), F32),
        compiler_params=_cparams(("parallel", "parallel")),
        name="adaln_modulation",
    )(cs, w_mod, b_mod.reshape(depth, 1, n))


def _in0_kernel(x_ref, mod_ref, g_ref, w_ref, cos_ref, sin_ref,
                q_ref, k_ref, v_ref, bg_ref, p_ref):
    mod = mod_ref[0]
    h = _norm_mod(x_ref[...], g_ref[...], mod[0:1], mod[1:2])
    y = _dot(h.astype(BF16), w_ref[...])
    cosv = cos_ref[...]
    sinv = sin_ref[...]
    lane = lax.broadcasted_iota(I32, (TM, LANES), 1)
    first_half = (lane & 16) == 0

    def rope(z):
        outs = []
        for g in range(A_WIDTH // LANES):
            zg = z[:, g * LANES:(g + 1) * LANES]
            partner = jnp.where(first_half, pltpu.roll(zg, LANES - 16, 1), pltpu.roll(zg, 16, 1))
            outs.append(zg * cosv + partner * sinv)
        return jnp.concatenate(outs, axis=1)

    aw = A_WIDTH
    q_ref[...] = (rope(y[:, :aw]) * (HEAD_DIM ** -0.5 * LOG2_E)).astype(BF16)
    k_ref[...] = rope(y[:, aw:2 * aw]).astype(BF16)
    v_ref[...] = y[:, 2 * aw:3 * aw].astype(BF16)
    bw = (y.shape[1] - 3 * aw) // 3
    bg_ref[...] = y[:, 3 * aw:3 * aw + bw].astype(BF16)
    p_ref[...] = (y[:, 3 * aw + bw:3 * aw + 2 * bw] * y[:, 3 * aw + 2 * bw:]).astype(BF16)


def _in_proj0(x, modv, g, w, cos_t, sin_t, ntb):
    t, d = x.shape
    n = w.shape[1]
    bw = (n - 3 * A_WIDTH) // 3
    row = lambda i: (i, 0)
    return pl.pallas_call(
        _in0_kernel,
        grid=(t // TM,),
        in_specs=[
            pl.BlockSpec((TM, d), row),
            pl.BlockSpec((1, 6, d), lambda i: ((i // ntb) * 2 + jnp.minimum(i % ntb, 1), 0, 0)),
            pl.BlockSpec((1, d), lambda i: (0, 0)),
            pl.BlockSpec((d, n), lambda i: (0, 0)),
            pl.BlockSpec((TM, LANES), lambda i: (i % ntb, 0)),
            pl.BlockSpec((TM, LANES), lambda i: (i % ntb, 0)),
        ],
        out_specs=[pl.BlockSpec((TM, A_WIDTH), row)] * 3 + [pl.BlockSpec((TM, bw), row)] * 2,
        out_shape=[jax.ShapeDtypeStruct((t, A_WIDTH), BF16)] * 3 + [jax.ShapeDtypeStruct((t, bw), BF16)] * 2,
        compiler_params=_cparams(("parallel",)),
        name="l0_in_proj_rope",
    )(x, modv, g, w, cos_t, sin_t)


def _attn_kernel(lq1_ref, lk1_ref, lq2_ref, lk2_ref, g_ref, q_ref, k_ref, v_ref, o_ref, *, lam_init, n_ctx):
    qi = pl.program_id(2)
    lam = (jnp.exp(jnp.sum(lq1_ref[...] * lk1_ref[...], axis=-1, keepdims=True))
           - jnp.exp(jnp.sum(lq2_ref[...] * lk2_ref[...], axis=-1, keepdims=True)) + lam_init)
    q = q_ref[...]
    lane = lax.broadcasted_iota(I32, q.shape, 1)
    zero = jnp.zeros_like(q)
    q1 = jnp.where(lane < HEAD_DIM, q, zero)
    q2 = jnp.where(lane < HEAD_DIM, zero, q)
    contract_last = (((1,), (1,)), ((), ()))

    def attend(nk):
        k = k_ref[0, :nk, :]
        v = v_ref[0, :nk, :]
        s1 = lax.dot_general(q1, k, contract_last, preferred_element_type=F32)
        s2 = lax.dot_general(q2, k, contract_last, preferred_element_type=F32)
        p1 = jnp.exp2(s1 - jnp.max(s1, axis=-1, keepdims=True))
        p2 = jnp.exp2(s2 - jnp.max(s2, axis=-1, keepdims=True))
        r1 = 1.0 / jnp.sum(p1, axis=-1, keepdims=True)
        r2 = lam / jnp.sum(p2, axis=-1, keepdims=True)
        o = _dot(p1.astype(BF16), v) * r1 - _dot(p2.astype(BF16), v) * r2
        ms = jnp.mean(o * o, axis=-1, keepdims=True)
        o = o * lax.rsqrt(ms + NORM_EPS) * g_ref[...] * (1.0 - lam_init)
        o_ref[...] = o.astype(BF16)

    @pl.when(qi == 0)
    def _():
        attend(n_ctx)

    @pl.when(qi > 0)
    def _():
        attend(k_ref.shape[1])


def _diff_attention(q, k, v, lq1, lk1, lq2, lk2, subln, lam_init, b, l, n_ctx):
    t = q.shape[0]
    ntb = l // TM
    hw = 2 * HEAD_DIM
    k3 = k.reshape(b, l, A_WIDTH)
    v3 = v.reshape(b, l, A_WIDTH)
    vec = lambda n: pl.BlockSpec((1, n), lambda bi, h, qi: (0, 0))
    qspec = pl.BlockSpec((TM, hw), lambda bi, h, qi: (bi * ntb + qi, h))
    kspec = pl.BlockSpec((1, l, hw), lambda bi, h, qi: (bi, 0, h))
    return pl.pallas_call(
        functools.partial(_attn_kernel, lam_init=lam_init, n_ctx=n_ctx),
        grid=(b, N_HEADS, ntb),
        in_specs=[vec(HEAD_DIM)] * 4 + [vec(hw), qspec, kspec, kspec],
        out_specs=qspec,
        out_shape=jax.ShapeDtypeStruct((t, A_WIDTH), BF16),
        compiler_params=_cparams(("parallel", "parallel", "parallel")),
        name="l0_diff_attention",
    )(lq1, lk1, lq2, lk2, subln, q, k3, v3)


def _route_tail(y, x_ref, mod_ref, g_ref, wr_ref, br_ref,
                x1_ref, h2_ref, route_ref, cnt_ref, carry_ref):
    mod = mod_ref[0]
    x1 = x_ref[...] + mod[2:3] * y
    x1_ref[...] = x1
    h2 = _norm_mod(x1, g_ref[...], mod[3:4], mod[4:5])
    nch = h2.shape[1] // LANES
    for s in range(nch):
        h2_ref[pl.ds(s, TM, stride=nch), :] = h2[:, s * LANES:(s + 1) * LANES]

    wr = wr_ref[...]
    h_hi = h2.astype(BF16)
    h_lo = (h2 - h_hi.astype(F32)).astype(BF16)
    w_hi = wr.astype(BF16)
    w_lo = (wr - w_hi.astype(F32)).astype(BF16)
    logits = _dot(h_hi, w_hi) + _dot(h_hi, w_lo) + _dot(h_lo, w_hi) + br_ref[...]

    n_exp = logits.shape[1]
    lane = lax.broadcasted_iota(I32, logits.shape, 1).astype(F32)
    work = logits
    sels, vals, idxs = [], [], []
    for _ in range(TOP_K):
        m = jnp.max(work, axis=-1, keepdims=True)
        idx = jnp.min(jnp.where(work == m, lane, float(n_exp)), axis=-1, keepdims=True)
        sel = lane == idx
        sels.append(sel)
        vals.append(m)
        idxs.append(idx)
        work = jnp.where(sel, -jnp.inf, work)
    exps = [jnp.exp(vv - vals[0]) for vv in vals]
    inv_den = 1.0 / (exps[0] + exps[1] + exps[2] + exps[3])

    chosen = jnp.zeros(logits.shape, F32)
    for sel in sels:
        chosen = chosen + jnp.where(sel, 1.0, 0.0)
    r_i = lax.broadcasted_iota(I32, (TM, TM), 0)
    c_i = lax.broadcasted_iota(I32, (TM, TM), 1)
    earlier = jnp.where(c_i < r_i, 1.0, 0.0).astype(BF16)
    rank = _dot(earlier, chosen.astype(BF16)) + carry_ref[...]
    carry_ref[...] = carry_ref[...] + jnp.sum(chosen, axis=0, keepdims=True)
    cnt_ref[...] = carry_ref[...]

    out_lane = lax.broadcasted_iota(I32, (TM, LANES), 1)
    packed = jnp.zeros((TM, LANES), F32)
    for kk in range(TOP_K):
        rank_k = jnp.sum(jnp.where(sels[kk], rank, 0.0), axis=-1, keepdims=True)
        packed = jnp.where(out_lane == kk, idxs[kk], packed)
        packed = jnp.where(out_lane == TOP_K + kk, rank_k, packed)
        packed = jnp.where(out_lane == 2 * TOP_K + kk, exps[kk] * inv_den, packed)
    route_ref[...] = packed


def _out0_kernel(attn_ref, bg_ref, p_ref, pprev_ref, pnext_ref, cw_ref, wo_ref,
                 x_ref, mod_ref, g_ref, wr_ref, br_ref,
                 x1_ref, h2_ref, route_ref, cnt_ref, carry_ref, *, ntb):
    i = pl.program_id(0)
    seg = i % ntb

    @pl.when(i == 0)
    def _():
        carry_ref[...] = jnp.zeros_like(carry_ref)

    p = p_ref[...].astype(F32)
    row = lax.broadcasted_iota(I32, p.shape, 0)
    has_prev = seg > 1
    has_next = jnp.logical_and(seg > 0, seg < ntb - 1)
    prev_row = jnp.where(has_prev, pprev_ref[SUBLANES - 1:SUBLANES, :].astype(F32), 0.0)
    next_row = jnp.where(has_next, pnext_ref[0:1, :].astype(F32), 0.0)
    before = jnp.where(row == 0, prev_row, pltpu.roll(p, 1, 0))
    after = jnp.where(row == TM - 1, next_row, pltpu.roll(p, TM - 1, 0))
    cw = cw_ref[...]
    conv = bg_ref[...].astype(F32) * (cw[0:1] * before + cw[1:2] * p + cw[2:3] * after)
    aw = attn_ref.shape[1]
    y = _dot(attn_ref[...], wo_ref[:aw, :]) + _dot(conv.astype(BF16), wo_ref[aw:, :])
    _route_tail(y, x_ref, mod_ref, g_ref, wr_ref, br_ref, x1_ref, h2_ref, route_ref, cnt_ref, carry_ref)


def _out1_kernel(hs_ref, gg_ref, wo_ref, x_ref, mod_ref, g_ref, wr_ref, br_ref,
                 x1_ref, h2_ref, route_ref, cnt_ref, carry_ref, *, ntb):
    i = pl.program_id(0)

    @pl.when(i == 0)
    def _():
        carry_ref[...] = jnp.zeros_like(carry_ref)

    @pl.when(i % ntb > 0)
    def _():
        rec = hs_ref[0].astype(F32) + hs_ref[1].astype(F32)
        y = _dot((rec * gg_ref[...].astype(F32)).astype(BF16), wo_ref[...])
        _route_tail(y, x_ref, mod_ref, g_ref, wr_ref, br_ref, x1_ref, h2_ref, route_ref, cnt_ref, carry_ref)


def _tail_specs(d, n_exp, ntb, latent_only):
    row = lambda i: (i, 0)
    const = lambda i: (0, 0)
    out_row = (lambda i: ((i // ntb) * (ntb - 1) + jnp.maximum(i % ntb - 1, 0), 0)) if latent_only else row
    in_specs = [
        pl.BlockSpec((TM, d), row),
        pl.BlockSpec((1, 6, d), lambda i: ((i // ntb) * 2 + jnp.minimum(i % ntb, 1), 0, 0)),
        pl.BlockSpec((1, d), const),
        pl.BlockSpec((d, n_exp), const),
        pl.BlockSpec((1, n_exp), const),
    ]
    out_specs = [
        pl.BlockSpec((TM, d), out_row),
        pl.BlockSpec((TM * (d // LANES), LANES), out_row),
        pl.BlockSpec((TM, LANES), out_row),
        pl.BlockSpec((1, n_exp), const),
    ]
    return in_specs, out_specs


def _tail_shapes(t, d, n_exp):
    return [jax.ShapeDtypeStruct((t, d), F32), jax.ShapeDtypeStruct((t * (d // LANES), LANES), F32),
            jax.ShapeDtypeStruct((t, LANES), F32), jax.ShapeDtypeStruct((1, n_exp), F32)]


def _out_proj0(attn, bg, p, conv_w, w_out, x, modv, g, w_r, b_r, ntb):
    t, d = x.shape
    n_exp = w_r.shape[1]
    bw = bg.shape[1]
    row = lambda i: (i, 0)
    const = lambda i: (0, 0)
    nblk = TM // SUBLANES
    tail_in, tail_out = _tail_specs(d, n_exp, ntb, False)
    return pl.pallas_call(
        functools.partial(_out0_kernel, ntb=ntb),
        grid=(t // TM,),
        in_specs=[
            pl.BlockSpec((TM, attn.shape[1]), row),
            pl.BlockSpec((TM, bw), row),
            pl.BlockSpec((TM, bw), row),
            pl.BlockSpec((SUBLANES, bw), lambda i: (jnp.maximum(i * nblk - 1, 0), 0)),
            pl.BlockSpec((SUBLANES, bw), lambda i: (jnp.minimum((i + 1) * nblk, t // SUBLANES - 1), 0)),
            pl.BlockSpec(conv_w.shape, const),
            pl.BlockSpec(w_out.shape, const),
        ] + tail_in,
        out_specs=tail_out,
        out_shape=_tail_shapes(t, d, n_exp),
        scratch_shapes=[pltpu.VMEM((1, n_exp), F32)],
        compiler_params=_cparams(("arbitrary",)),
        name="l0_out_proj_router",
    )(attn, bg, p, p, p, conv_w, w_out, x, modv, g, w_r, b_r)


def _out_proj1(hs, gg, w_out, x, modv, g, w_r, b_r, ntb):
    t, d = x.shape
    n_exp = w_r.shape[1]
    row = lambda i: (i, 0)
    const = lambda i: (0, 0)
    tail_in, tail_out = _tail_specs(d, n_exp, ntb, True)
    t_lat = t // ntb * (ntb - 1)
    return pl.pallas_call(
        functools.partial(_out1_kernel, ntb=ntb),
        grid=(t // TM,),
        in_specs=[
            pl.BlockSpec((2, TM, hs.shape[2]), lambda i: (0, i, 0)),
            pl.BlockSpec((TM, gg.shape[1]), row),
            pl.BlockSpec(w_out.shape, const),
        ] + tail_in,
        out_specs=tail_out,
        out_shape=_tail_shapes(t_lat, d, n_exp),
        scratch_shapes=[pltpu.VMEM((1, n_exp), F32)],
        compiler_params=_cparams(("arbitrary",)),
        name="l1_out_proj_router",
    )(hs, gg, w_out, x, modv, g, w_r, b_r)


def _inv_kernel(pad_lo_ref, pad_hi_ref, pos_ref, val_ref, inv_ref, pbuf, vbuf, sem, *, spare_row0):
    c = pl.program_id(0)
    n_chunks = pl.num_programs(0)
    slot = c % 2

    def chunk_copies(ci, s):
        src = pl.ds(ci * POS_CHUNK, POS_CHUNK)
        dst = pl.ds(s * POS_CHUNK, POS_CHUNK)
        return (pltpu.make_async_copy(pos_ref.at[src], pbuf.at[dst], sem.at[0, s]),
                pltpu.make_async_copy(val_ref.at[src], vbuf.at[dst], sem.at[1, s]))

    @pl.when(c == 0)
    def _():
        for cp in chunk_copies(0, 0):
            cp.start()

        def fill_range(e, carry):
            def fill(j, cc):
                sp = j & (2 * TME - 1)
                inv_ref[j] = ((spare_row0 + sp) << KEY_SHIFT) | sp
                return cc
            return lax.fori_loop(pad_lo_ref[e], pad_hi_ref[e], fill, carry)

        lax.fori_loop(0, pad_lo_ref.shape[0], fill_range, 0)

    for cp in chunk_copies(c, slot):
        cp.wait()

    @pl.when(c + 1 < n_chunks)
    def _():
        for cp in chunk_copies(c + 1, 1 - slot):
            cp.start()

    off = slot * POS_CHUNK

    def put(jb, carry):
        j0 = off + jb * INV_UNROLL
        slots = [pbuf[j0 + u] for u in range(INV_UNROLL)]
        vals = [vbuf[j0 + u] for u in range(INV_UNROLL)]
        for u in range(INV_UNROLL):
            inv_ref[slots[u]] = vals[u]
        return carry

    lax.fori_loop(0, POS_CHUNK // INV_UNROLL, put, 0)


def _invert_slots(pos, val, pad_lo, pad_hi, n_inv, spare_row0):
    n_pairs = pos.shape[0]
    grid_spec = pltpu.PrefetchScalarGridSpec(
        num_scalar_prefetch=2,
        grid=(n_pairs // POS_CHUNK,),
        in_specs=[pl.BlockSpec(memory_space=pl.ANY), pl.BlockSpec(memory_space=pl.ANY)],
        out_specs=pl.BlockSpec(memory_space=pltpu.SMEM),
        scratch_shapes=[pltpu.SMEM((2 * POS_CHUNK,), I32), pltpu.SMEM((2 * POS_CHUNK,), I32),
                        pltpu.SemaphoreType.DMA((2, 2))],
    )
    return pl.pallas_call(
        functools.partial(_inv_kernel, spare_row0=spare_row0),
        grid_spec=grid_spec,
        out_shape=jax.ShapeDtypeStruct((n_inv,), I32),
        compiler_params=_cparams(("arbitrary",)),
        name="slot_inversion",
    )(pad_lo, pad_hi, pos, val)


def _moe_kernel(te_ref, meta_ref, inv_ref, h2_ref, w1_ref, b1_ref, w2_ref, b2_ref, y_ref,
                xbuf, obuf, w1b, w2b, gsem, ssem, *, n_tok):
    i = pl.program_id(0)
    n_tiles = meta_ref[0]
    slot = i % 2
    other = 1 - slot
    d = w1_ref.shape[2]
    ff = w2_ref.shape[2]
    nch = d // LANES
    spare0 = TOP_K * n_tok * nch

    def gather_row(entry, s, r):
        src = pl.multiple_of((entry & KEY_MASK) * nch, nch)
        return pltpu.make_async_copy(h2_ref.at[pl.ds(src, nch)], xbuf.at[s, pl.ds(r * nch, nch)], gsem.at[s])

    def scatter_row(entry, s, r):
        dst = pl.multiple_of(lax.shift_right_logical(entry, KEY_SHIFT) * nch, nch)
        return pltpu.make_async_copy(obuf.at[s, pl.ds(r * nch, nch)], y_ref.at[pl.ds(dst, nch)], ssem.at[s])

    def for_rows(fn):
        def body(r8, c):
            for u in range(SUBLANES):
                fn(r8 * SUBLANES + u)
            return c
        lax.fori_loop(0, TME // SUBLANES, body, 0)

    def wait_gather(s):
        pltpu.make_async_copy(h2_ref.at[pl.ds(0, TME * nch)], xbuf.at[s], gsem.at[s]).wait()

    def wait_scatter(s):
        pltpu.make_async_copy(obuf.at[s], y_ref.at[pl.ds(0, TME * nch)], ssem.at[s]).wait()

    @pl.when(i == 0)
    def _():
        for_rows(lambda r: gather_row(inv_ref[r], 0, r).start())
        obuf[...] = jnp.zeros(obuf.shape, obuf.dtype)
        pltpu.make_async_copy(obuf.at[0], y_ref.at[pl.ds(spare0, TME * nch)], ssem.at[0]).start()

    @pl.when(i < n_tiles)
    def _():
        wait_gather(slot)
        new_expert = jnp.logical_or(i == 0, te_ref[i] != te_ref[jnp.maximum(i - 1, 0)])

        @pl.when(new_expert)
        def _():
            w1b[...] = w1_ref[0, 0].astype(BF16)
            w2b[...] = w2_ref[0, 0].astype(BF16)

        nxt = jnp.minimum(i + 1, n_tiles - 1) * TME
        prv = jnp.maximum(i - 1, 0) * TME
        first = i == 0
        for r in range(TME):
            gather_row(inv_ref[nxt + r], other, r).start()
            spare_entry = _as_i32(((TOP_K * n_tok + TME + r) << KEY_SHIFT) | (TME + r))
            prev_entry = jnp.where(first, spare_entry, inv_ref[prv + r])
            scatter_row(prev_entry, other, r).start()

        x = jnp.concatenate([xbuf[slot, pl.ds(c, TME, stride=nch), :] for c in range(nch)], axis=1)
        h = _dot(x.astype(BF16), w1b[...]) + b1_ref[0]
        glu = jnp.minimum(h[:, :ff], SWIGLU_LIMIT)
        lin = jnp.clip(h[:, ff:], -SWIGLU_LIMIT, SWIGLU_LIMIT)
        act = glu * jax.nn.sigmoid(SWIGLU_ALPHA * glu) * (lin + 1.0)
        y = _dot(act.astype(BF16), w2b[...]) + b2_ref[0]

        wait_scatter(slot)
        for c in range(nch):
            obuf[slot, pl.ds(c, TME, stride=nch), :] = y[:, c * LANES:(c + 1) * LANES]

    @pl.when(i == n_tiles)
    def _():
        last = n_tiles - 1
        for_rows(lambda r: scatter_row(inv_ref[last * TME + r], last % 2, r).start())
        wait_gather(n_tiles % 2)
        wait_scatter(0)
        wait_scatter(1)


def _routed_experts(tile_expert, meta, inv, h2, w1, b1, w2, b2, layer):
    _, n_exp, d, ff2 = w1.shape
    n_tok = h2.shape[0] // (d // LANES)
    assert n_tok <= KEY_MASK + 1 and 2 * TME <= n_tok and (TOP_K * n_tok + 2 * TME) < (1 << (32 - KEY_SHIFT))
    ff = w2.shape[2]
    nch = d // LANES
    n_grid = tile_expert.shape[0]
    tile = lambda i: jnp.minimum(i, n_grid - 1)
    wmap = lambda i, te, mt, iv: (layer, te[tile(i)], 0, 0)
    bmap = lambda i, te, mt, iv: (layer * n_exp + te[tile(i)], 0, 0)
    grid_spec = pltpu.PrefetchScalarGridSpec(
        num_scalar_prefetch=3,
        grid=(n_grid + 1,),
        in_specs=[
            pl.BlockSpec(memory_space=pl.ANY),
            pl.BlockSpec((1, 1, d, ff2), wmap),
            pl.BlockSpec((1, 1, ff2), bmap),
            pl.BlockSpec((1, 1, ff, d), wmap),
            pl.BlockSpec((1, 1, d), bmap),
        ],
        out_specs=pl.BlockSpec(memory_space=pl.ANY),
        scratch_shapes=[
            pltpu.VMEM((2, TME * nch, LANES), F32),
            pltpu.VMEM((2, TME * nch, LANES), F32),
            pltpu.VMEM((d, ff2), BF16),
            pltpu.VMEM((ff, d), BF16),
            pltpu.SemaphoreType.DMA((2,)),
            pltpu.SemaphoreType.DMA((2,)),
        ],
    )
    return pl.pallas_call(
        functools.partial(_moe_kernel, n_tok=n_tok),
        grid_spec=grid_spec,
        out_shape=jax.ShapeDtypeStruct(((TOP_K * n_tok + 2 * TME) * nch, LANES), F32),
        compiler_params=_cparams(("arbitrary",)),
        name="routed_experts",
    )(tile_expert, meta, inv, h2, w1, b1.reshape(-1, 1, ff2), w2, b2.reshape(-1, 1, d))


def _plan_routes(route, counts, n_grid):
    n_exp = counts.shape[-1]
    cnt = counts.reshape(n_exp).astype(I32)
    tiles_per = (cnt + TME - 1) // TME
    tile_end = jnp.cumsum(tiles_per)
    offset = (tile_end - tiles_per) * TME
    n_tiles = tile_end[-1]
    tile_ids = jnp.minimum(jnp.arange(n_grid, dtype=I32), n_tiles - 1)
    tile_expert = jnp.sum((tile_ids[:, None] >= tile_end[None, :]).astype(I32), axis=1)
    eidx = route[:, :TOP_K].astype(I32)
    rank = route[:, TOP_K:2 * TOP_K].astype(I32)
    expert_ids = jnp.arange(n_exp, dtype=I32)
    pos = jnp.sum(jnp.where(eidx[..., None] == expert_ids, offset, 0), axis=-1) + rank
    n_tok = route.shape[0]
    tok = jnp.arange(n_tok, dtype=jnp.uint32)[:, None]
    choice = jnp.arange(TOP_K, dtype=jnp.uint32)[None, :]
    val = lax.bitcast_convert_type(((choice * n_tok + tok) << KEY_SHIFT) | tok, I32)
    pad_lo = jnp.concatenate([offset + cnt, (n_tiles * TME).reshape(1)])
    pad_hi = jnp.concatenate([tile_end * TME, jnp.full((1,), n_grid * TME, I32)])
    return tile_expert, n_tiles.reshape(1), pos.reshape(-1), val.reshape(-1), pad_lo, pad_hi


def _moe_layer(route, counts, h2, w1, b1, w2, b2, layer):
    n_exp = w1.shape[1]
    n_pairs = route.shape[0] * TOP_K
    n_grid = n_pairs // TME + n_exp
    tile_expert, meta, pos, val, pad_lo, pad_hi = _plan_routes(route, counts, n_grid)
    inv = _invert_slots(pos, val, pad_lo, pad_hi, n_grid * TME, TOP_K * route.shape[0])
    return _routed_experts(tile_expert, meta, inv, h2, w1, b1, w2, b2, layer)


def _combine(y_refs, route_ref, d):
    nch = d // LANES
    gates = route_ref[...]
    gk = [jnp.broadcast_to(gates[:, 2 * TOP_K + kk:2 * TOP_K + kk + 1], (TM, LANES)) for kk in range(TOP_K)]
    chunks = []
    for c in range(nch):
        acc = gk[0] * y_refs[0][pl.ds(c, TM, stride=nch), :]
        for kk in range(1, TOP_K):
            acc = acc + gk[kk] * y_refs[kk][pl.ds(c, TM, stride=nch), :]
        chunks.append(acc)
    return jnp.concatenate(chunks, axis=1)


def _choice_specs(n_tok, d):
    per = n_tok // TM
    return [pl.BlockSpec((TM * (d // LANES), LANES), functools.partial(lambda i, kk: (kk * per + i, 0), kk=kk))
            for kk in range(TOP_K)]


def _in1_kernel(x_ref, y0_ref, y1_ref, y2_ref, y3_ref, route_ref, mod0_ref, mod1_ref, g_ref, w_ref,
                x2_ref, gg_ref, u_ref):
    x2 = x_ref[...] + mod0_ref[0][5:6] * _combine((y0_ref, y1_ref, y2_ref, y3_ref), route_ref, x_ref.shape[1])
    x2_ref[...] = x2
    mod1 = mod1_ref[0]
    h = _norm_mod(x2, g_ref[...], mod1[0:1], mod1[1:2])
    y = _dot(h.astype(BF16), w_ref[...])
    half = y.shape[1] // 2
    gg_ref[...] = jax.nn.gelu(y[:, :half]).astype(BF16)
    u_ref[...] = y[:, half:]


def _in_proj1(x, ybuf, route, modv0, modv1, g, w, ntb):
    t, d = x.shape
    n = w.shape[1]
    row = lambda i: (i, 0)
    mod_map = lambda i: ((i // ntb) * 2 + jnp.minimum(i % ntb, 1), 0, 0)
    return pl.pallas_call(
        _in1_kernel,
        grid=(t // TM,),
        in_specs=[
            pl.BlockSpec((TM, d), row),
        ] + _choice_specs(t, d) + [
            pl.BlockSpec((TM, LANES), row),
            pl.BlockSpec((1, 6, d), mod_map),
            pl.BlockSpec((1, 6, d), mod_map),
            pl.BlockSpec((1, d), lambda i: (0, 0)),
            pl.BlockSpec((d, n), lambda i: (0, 0)),
        ],
        out_specs=[pl.BlockSpec((TM, d), row), pl.BlockSpec((TM, n // 2), row), pl.BlockSpec((TM, n // 2), row)],
        out_shape=[jax.ShapeDtypeStruct((t, d), F32), jax.ShapeDtypeStruct((t, n // 2), BF16),
                   jax.ShapeDtypeStruct((t, n // 2), F32)],
        compiler_params=_cparams(("parallel",)),
        name="l1_combine_in_proj",
    )(x, ybuf, ybuf, ybuf, ybuf, route, modv0, modv1, g, w)


def _rglru_kernel(u_ref, halo_ref, cw_ref, cb_ref, gaw_ref, gab_ref, gxw_ref, gxb_ref, lam_ref, o_ref,
                  ext, a_s, b_s, h_s, state, *, ntb):
    d = pl.program_id(0)
    s = pl.program_id(2)
    nb, tc, cw_ = u_ref.shape
    u = u_ref[...]
    cw = cw_ref[0]

    def finish(xc, reverse):
        xc2 = xc.reshape(nb * tc, cw_) + cb_ref[0]
        xb = xc2.astype(BF16)
        t_r = jnp.tanh(_dot(xb, gaw_ref[0, 0]) + gab_ref[0])
        t_i = jnp.tanh(_dot(xb, gxw_ref[0, 0]) + gxb_ref[0])
        nl = -lam_ref[0]
        softplus = jnp.maximum(nl, 0.0) + jnp.log1p(jnp.exp(-jnp.abs(nl)))
        half_rate = (-0.5 * LRU_C) * softplus
        log_a = half_rate * t_r + half_rate
        a = jnp.exp(log_a)
        half_x = 0.5 * xc2
        bb = jnp.sqrt(1.0 - a * a) * (half_x * t_i + half_x)
        n_lane = cw_ // LANES
        pitch = tc + SCAN_PAD
        for c in range(n_lane):
            for bi in range(nb):
                a_s[c, bi * pitch:bi * pitch + tc, :] = a[bi * tc:(bi + 1) * tc, c * LANES:(c + 1) * LANES]
                b_s[c, bi * pitch:bi * pitch + tc, :] = bb[bi * tc:(bi + 1) * tc, c * LANES:(c + 1) * LANES]

        @pl.when(s == 0)
        def _():
            state[...] = jnp.zeros_like(state)

        def steps(tb, hs):
            hs = list(hs)
            for uu in range(SCAN_UNROLL):
                tt = tb * SCAN_UNROLL + uu
                t = (tc - 1 - tt) if reverse else tt
                rows = pl.ds(t, nb, stride=pitch)
                for c in range(n_lane):
                    hs[c] = a_s[c, rows, :] * hs[c] + b_s[c, rows, :]
                    h_s[c, rows, :] = hs[c]
            return tuple(hs)

        hs = lax.fori_loop(0, tc // SCAN_UNROLL, steps, tuple(state[c] for c in range(n_lane)))
        for c in range(n_lane):
            state[c] = hs[c]
            for bi in range(nb):
                o_ref[0, bi, :, c * LANES:(c + 1) * LANES] = h_s[c, bi * pitch:bi * pitch + tc, :].astype(o_ref.dtype)

    @pl.when(d == 0)
    def _():
        chunk = s
        keep = chunk > 1
        ext[:, 0:SUBLANES, :] = jnp.where(keep, halo_ref[...], 0.0)
        ext[:, SUBLANES:, :] = u
        xc = cw[3:4] * u
        for j in range(1, C_CONV):
            xc = xc + cw[3 - j:4 - j] * ext[:, SUBLANES - j:SUBLANES - j + tc, :]
        finish(xc, False)

    @pl.when(d == 1)
    def _():
        chunk = jnp.where(s == 0, 0, ntb - s)
        keep = jnp.logical_and(chunk > 0, chunk < ntb - 1)
        ext[:, 0:tc, :] = u
        ext[:, tc:, :] = jnp.where(keep, halo_ref[...], 0.0)
        xc = cw[0:1] * u
        for j in range(1, C_CONV):
            xc = xc + cw[j:j + 1] * ext[:, j:j + tc, :]
        finish(xc, True)


def _rglru(u, conv_w, conv_b, ga_w, ga_b, gx_w, gx_b, lam, b, l):
    width = u.shape[1]
    cb = width // C_BLOCKS
    ntb = l // TM
    u3 = u.reshape(b, l, width)
    nblk = TM // SUBLANES

    def chunk_of(d, s):
        return jnp.where(d == 0, s, jnp.where(s == 0, 0, ntb - s))

    def halo_of(d, s):
        c = chunk_of(d, s)
        return jnp.where(d == 0, jnp.maximum(c * nblk - 1, 0), jnp.minimum((c + 1) * nblk, l // SUBLANES - 1))

    vec = pl.BlockSpec((1, 1, cb), lambda d, g, s: (d, 0, g))
    mat = pl.BlockSpec((1, 1, cb, cb), lambda d, g, s: (d, g, 0, 0))
    return pl.pallas_call(
        functools.partial(_rglru_kernel, ntb=ntb),
        grid=(2, C_BLOCKS, ntb),
        in_specs=[
            pl.BlockSpec((b, TM, cb), lambda d, g, s: (0, chunk_of(d, s), g)),
            pl.BlockSpec((b, SUBLANES, cb), lambda d, g, s: (0, halo_of(d, s), g)),
            pl.BlockSpec((1, C_CONV, cb), lambda d, g, s: (d, 0, g)),
            vec, mat, vec, mat, vec, vec,
        ],
        out_specs=pl.BlockSpec((1, b, TM, cb), lambda d, g, s: (d, 0, chunk_of(d, s), g)),
        out_shape=jax.ShapeDtypeStruct((2, b, l, width), BF16),
        scratch_shapes=[
            pltpu.VMEM((b, TM + SUBLANES, cb), F32),
            pltpu.VMEM((cb // LANES, b * (TM + SCAN_PAD), LANES), F32),
            pltpu.VMEM((cb // LANES, b * (TM + SCAN_PAD), LANES), F32),
            pltpu.VMEM((cb // LANES, b * (TM + SCAN_PAD), LANES), F32),
            pltpu.VMEM((cb // LANES, b, LANES), F32),
        ],
        compiler_params=_cparams(("arbitrary", "arbitrary", "arbitrary")),
        name="l1_rglru",
    )(u3, u3, conv_w, conv_b.reshape(2, 1, width), (0.5 * ga_w).astype(BF16), (0.5 * ga_b).reshape(2, 1, width),
      (0.5 * gx_w).astype(BF16), (0.5 * gx_b).reshape(2, 1, width), lam.reshape(2, 1, width))


def _final_kernel(x_ref, y0_ref, y1_ref, y2_ref, y3_ref, route_ref, mod_ref, g_ref, o_ref):
    x = x_ref[...] + mod_ref[0][5:6] * _combine((y0_ref, y1_ref, y2_ref, y3_ref), route_ref, x_ref.shape[1])
    ms = jnp.mean(x * x, axis=-1, keepdims=True)
    o_ref[...] = x * lax.rsqrt(ms + NORM_EPS) * g_ref[...]


def _final(x, ybuf, route, modv, g, b, s_len):
    t, d = x.shape
    per_b = s_len // TM
    row = lambda i: (i, 0)
    return pl.pallas_call(
        _final_kernel,
        grid=(t // TM,),
        in_specs=[
            pl.BlockSpec((TM, d), row),
        ] + _choice_specs(t, d) + [
            pl.BlockSpec((TM, LANES), row),
            pl.BlockSpec((1, 6, d), lambda i: ((i // per_b) * 2 + 1, 0, 0)),
            pl.BlockSpec((1, d), lambda i: (0, 0)),
        ],
        out_specs=pl.BlockSpec((TM, d), row),
        out_shape=jax.ShapeDtypeStruct((t, d), F32),
        compiler_params=_cparams(("parallel",)),
        name="final_combine_norm",
    )(x, ybuf, ybuf, ybuf, ybuf, route, modv, g)


def _rope_tables(s_len, n_ctx):
    n_rows = s_len // GRID_W
    rows, cols = jnp.meshgrid(jnp.arange(n_rows), jnp.arange(GRID_W), indexing="ij")
    pos = jnp.stack([rows.reshape(-1), cols.reshape(-1)], axis=-1).astype(F32)
    n_freq = HEAD_DIM // 4
    inv = ROPE_THETA ** (-jnp.arange(n_freq, dtype=F32) / n_freq)
    ang = pos[:, :, None] * inv
    cos, sin = jnp.cos(ang), jnp.sin(ang)
    cos64 = jnp.stack([cos, cos], axis=2).reshape(s_len, HEAD_DIM)
    sin64 = jnp.stack([-sin, sin], axis=2).reshape(s_len, HEAD_DIM)
    cos_l = jnp.tile(cos64, (1, LANES // HEAD_DIM))
    sin_l = jnp.tile(sin64, (1, LANES // HEAD_DIM))
    cos_t = jnp.concatenate([jnp.ones((n_ctx, LANES), F32), cos_l], axis=0)
    sin_t = jnp.concatenate([jnp.zeros((n_ctx, LANES), F32), sin_l], axis=0)
    return cos_t, sin_t


def kernel(x, c, ctx, c_ctx, w_mod, b_mod, norm_mix, norm_ffn, ev_w_in, ev_w_out, ev_lambda_q1, ev_lambda_k1, ev_lambda_q2, ev_lambda_k2, ev_subln, ev_conv_w, od_w_in, od_w_out, od_conv_w, od_conv_b, od_gate_a_w, od_gate_a_b, od_gate_x_w, od_gate_x_b, od_lru_lambda, moe_w_router, moe_b_router, moe_w1, moe_b1, moe_w2, moe_b2, final_norm):
    b, s_len, d = x.shape
    n_ctx = ctx.shape[1]
    l = n_ctx + s_len
    t = b * l
    ntb = l // TM
    assert n_ctx == TM and s_len % TM == 0 and w_mod.shape[0] == 2
    assert (t * TOP_K) % POS_CHUNK == 0 and (b * s_len * TOP_K) % POS_CHUNK == 0

    xs = jnp.concatenate([ctx, x], axis=1).reshape(t, d)

    n_rows = -(-(b + 1) // SUBLANES) * SUBLANES
    cs = jnp.concatenate([c, c_ctx[None, :], jnp.zeros((n_rows - b - 1, d), F32)], axis=0)
    mod = _modulation(cs, w_mod, b_mod)

    def mod_table(i):
        lat = mod[i, :b]
        cx = jnp.broadcast_to(mod[i, b][None, :], lat.shape)
        return jnp.stack([cx, lat], axis=1).reshape(b * 2, 6, d)

    modv0, modv1 = mod_table(0), mod_table(1)

    cos_t, sin_t = _rope_tables(s_len, n_ctx)
    q, k, v, bg, p = _in_proj0(xs, modv0, norm_mix[0:1], ev_w_in[0].astype(BF16), cos_t, sin_t, ntb)
    lam_init = 0.8 - 0.6 * math.exp(-0.3 * 0)
    attn = _diff_attention(q, k, v, ev_lambda_q1[0:1], ev_lambda_k1[0:1], ev_lambda_q2[0:1], ev_lambda_k2[0:1],
                           ev_subln[0:1], lam_init, b, l, n_ctx)
    x1, h2, route0, counts0 = _out_proj0(attn, bg, p, ev_conv_w[0], ev_w_out[0].astype(BF16), xs, modv0,
                                         norm_ffn[0:1], moe_w_router[0], moe_b_router[0:1], ntb)
    ybuf0 = _moe_layer(route0, counts0, h2, moe_w1, moe_b1, moe_w2, moe_b2, 0)

    x2, gg, u = _in_proj1(x1, ybuf0, route0, modv0, modv1, norm_mix[1:2], od_w_in[0].astype(BF16), ntb)
    hs = _rglru(u, od_conv_w[0], od_conv_b[0], od_gate_a_w[0], od_gate_a_b[0], od_gate_x_w[0], od_gate_x_b[0],
                od_lru_lambda[0], b, l)
    x3, h3, route1, counts1 = _out_proj1(hs.reshape(2, t, hs.shape[-1]), gg, od_w_out[0].astype(BF16), x2, modv1,
                                         norm_ffn[1:2], moe_w_router[1], moe_b_router[1:2], ntb)
    ybuf1 = _moe_layer(route1, counts1, h3, moe_w1, moe_b1, moe_w2, moe_b2, 1)

    out = _final(x3, ybuf1, route1, modv1, final_norm[None, :], b, s_len)
    return out.reshape(b, s_len, d)
```

```python
import functools
import math

import jax
import jax.numpy as jnp
from jax import lax
from jax.experimental import pallas as pl
from jax.experimental.pallas import tpu as pltpu

F32 = jnp.float32
BF16 = jnp.bfloat16
I32 = jnp.int32

NORM_EPS = 1e-6
ROPE_THETA = 10000.0
GRID_W = 64
N_HEADS = 4
HEAD_DIM = 64
A_WIDTH = 2 * N_HEADS * HEAD_DIM
B_CONV = 3
C_CONV = 4
C_BLOCKS = 4
LRU_C = 8.0
TOP_K = 4
SWIGLU_ALPHA = 1.702
SWIGLU_LIMIT = 7.0

LANES = 128
SUBLANES = 8
TM = 256
TME = 256
VMEM_LIMIT = 56 * 1024 * 1024
SCAN_PAD = 8
SCAN_UNROLL = 8
LOG2_E = 1.4426950408889634


def _cparams(sem, vmem=VMEM_LIMIT):
    return pltpu.CompilerParams(dimension_semantics=sem, vmem_limit_bytes=vmem)


def _norm_mod(x, g, shift, scale):
    ms = jnp.mean(x * x, axis=-1, keepdims=True)
    return (x * lax.rsqrt(ms + NORM_EPS) * g) * (1.0 + scale) + shift


def _dot(a, b):
    return jnp.dot(a, b, preferred_element_type=F32)


def _sigmoid(x):
    return 0.5 * jnp.tanh(0.5 * x) + 0.5


def _mod_kernel(cs_ref, w_ref, b_ref, o_ref):
    s = cs_ref[...]
    s = s * jax.nn.sigmoid(s)
    o_ref[0] = _dot(s.astype(BF16), w_ref[0].astype(BF16)) + b_ref[0]


def _modulation(cs, w_mod, b_mod):
    depth, d, n = w_mod.shape
    rows = cs.shape[0]
    tn = 1536
    return pl.pallas_call(
        _mod_kernel,
        grid=(depth, n // tn),
        in_specs=[
            pl.BlockSpec((rows, d), lambda i, j: (0, 0)),
            pl.BlockSpec((1, d, tn), lambda i, j: (i, 0, j)),
            pl.BlockSpec((1, 1, tn), lambda i, j: (i, 0, j)),
        ],
        out_specs=pl.BlockSpec((1, rows, tn), lambda i, j: (i, 0, j)),
        out_shape=jax.ShapeDtypeStruct((depth, rows, n), F32),
        compiler_params=_cparams(("parallel", "parallel")),
        name="adaln_modulation",
    )(cs, w_mod, b_mod.reshape(depth, 1, n))


def _in0_kernel(x_ref, mod_ref, g_ref, w_ref, cos_ref, sin_ref,
                q_ref, k_ref, v_ref, bg_ref, p_ref):
    mod = mod_ref[0]
    h = _norm_mod(x_ref[...], g_ref[...], mod[0:1], mod[1:2])
    y = _dot(h.astype(BF16), w_ref[...])
    cosv = cos_ref[...]
    sinv = sin_ref[...]
    lane = lax.broadcasted_iota(I32, (TM, LANES), 1)
    first_half = (lane & 16) == 0

    def rope(z):
        outs = []
        for g in range(A_WIDTH // LANES):
            zg = z[:, g * LANES:(g + 1) * LANES]
            partner = jnp.where(first_half, pltpu.roll(zg, LANES - 16, 1), pltpu.roll(zg, 16, 1))
            outs.append(zg * cosv + partner * sinv)
        return jnp.concatenate(outs, axis=1)

    aw = A_WIDTH
    q_ref[...] = (rope(y[:, :aw]) * (HEAD_DIM ** -0.5 * LOG2_E)).astype(BF16)
    k_ref[...] = rope(y[:, aw:2 * aw]).astype(BF16)
    v_ref[...] = y[:, 2 * aw:3 * aw].astype(BF16)
    bw = (y.shape[1] - 3 * aw) // 3
    bg_ref[...] = y[:, 3 * aw:3 * aw + bw].astype(BF16)
    p_ref[...] = (y[:, 3 * aw + bw:3 * aw + 2 * bw] * y[:, 3 * aw + 2 * bw:]).astype(BF16)


def _in_proj0(x, modv, g, w, cos_t, sin_t, ntb):
    t, d = x.shape
    n = w.shape[1]
    bw = (n - 3 * A_WIDTH) // 3
    row = lambda i: (i, 0)
    return pl.pallas_call(
        _in0_kernel,
        grid=(t // TM,),
        in_specs=[
            pl.BlockSpec((TM, d), row),
            pl.BlockSpec((1, 6, d), lambda i: ((i // ntb) * 2 + jnp.minimum(i % ntb, 1), 0, 0)),
            pl.BlockSpec((1, d), lambda i: (0, 0)),
            pl.BlockSpec((d, n), lambda i: (0, 0)),
            pl.BlockSpec((TM, LANES), lambda i: (i % ntb, 0)),
            pl.BlockSpec((TM, LANES), lambda i: (i % ntb, 0)),
        ],
        out_specs=[pl.BlockSpec((TM, A_WIDTH), row)] * 3 + [pl.BlockSpec((TM, bw), row)] * 2,
        out_shape=[jax.ShapeDtypeStruct((t, A_WIDTH), BF16)] * 3 + [jax.ShapeDtypeStruct((t, bw), BF16)] * 2,
        compiler_params=_cparams(("parallel",)),
        name="l0_in_proj_rope",
    )(x, modv, g, w, cos_t, sin_t)


def _attn_kernel(lq1_ref, lk1_ref, lq2_ref, lk2_ref, g_ref, q_ref, k_ref, v_ref, o_ref, *, lam_init, n_ctx):
    qi = pl.program_id(2)
    lam = (jnp.exp(jnp.sum(lq1_ref[...] * lk1_ref[...], axis=-1, keepdims=True))
           - jnp.exp(jnp.sum(lq2_ref[...] * lk2_ref[...], axis=-1, keepdims=True)) + lam_init)
    q = q_ref[...]
    lane = lax.broadcasted_iota(I32, q.shape, 1)
    zero = jnp.zeros_like(q)
    q1 = jnp.where(lane < HEAD_DIM, q, zero)
    q2 = jnp.where(lane < HEAD_DIM, zero, q)
    contract_last = (((1,), (1,)), ((), ()))

    def attend(nk):
        k = k_ref[0, :nk, :]
        v = v_ref[0, :nk, :]
        s1 = lax.dot_general(q1, k, contract_last, preferred_element_type=F32)
        s2 = lax.dot_general(q2, k, contract_last, preferred_element_type=F32)
        p1 = jnp.exp2(s1 - jnp.max(s1, axis=-1, keepdims=True))
        p2 = jnp.exp2(s2 - jnp.max(s2, axis=-1, keepdims=True))
        r1 = 1.0 / jnp.sum(p1, axis=-1, keepdims=True)
        r2 = lam / jnp.sum(p2, axis=-1, keepdims=True)
        o = _dot(p1.astype(BF16), v) * r1 - _dot(p2.astype(BF16), v) * r2
        ms = jnp.mean(o * o, axis=-1, keepdims=True)
        o = o * lax.rsqrt(ms + NORM_EPS) * g_ref[...] * (1.0 - lam_init)
        o_ref[...] = o.astype(BF16)

    @pl.when(qi == 0)
    def _():
        attend(n_ctx)

    @pl.when(qi > 0)
    def _():
        attend(k_ref.shape[1])


def _diff_attention(q, k, v, lq1, lk1, lq2, lk2, subln, lam_init, b, l, n_ctx):
    t = q.shape[0]
    ntb = l // TM
    hw = 2 * HEAD_DIM
    k3 = k.reshape(b, l, A_WIDTH)
    v3 = v.reshape(b, l, A_WIDTH)
    vec = lambda n: pl.BlockSpec((1, n), lambda bi, h, qi: (0, 0))
    qspec = pl.BlockSpec((TM, hw), lambda bi, h, qi: (bi * ntb + qi, h))
    kspec = pl.BlockSpec((1, l, hw), lambda bi, h, qi: (bi, 0, h))
    return pl.pallas_call(
        functools.partial(_attn_kernel, lam_init=lam_init, n_ctx=n_ctx),
        grid=(b, N_HEADS, ntb),
        in_specs=[vec(HEAD_DIM)] * 4 + [vec(hw), qspec, kspec, kspec],
        out_specs=qspec,
        out_shape=jax.ShapeDtypeStruct((t, A_WIDTH), BF16),
        compiler_params=_cparams(("parallel", "parallel", "parallel")),
        name="l0_diff_attention",
    )(lq1, lk1, lq2, lk2, subln, q, k3, v3)


def _route_tail(y, x_ref, mod_ref, g_ref, wr_ref, br_ref,
                x1_ref, h2_ref, route_ref, cnt_ref, carry_ref):
    mod = mod_ref[0]
    x1 = x_ref[...] + mod[2:3] * y
    x1_ref[...] = x1
    h2 = _norm_mod(x1, g_ref[...], mod[3:4], mod[4:5])
    nch = h2.shape[1] // LANES
    for s in range(nch):
        h2_ref[pl.ds(s, TM, stride=nch), :] = h2[:, s * LANES:(s + 1) * LANES]

    wr = wr_ref[...]
    h_hi = h2.astype(BF16)
    h_lo = (h2 - h_hi.astype(F32)).astype(BF16)
    w_hi = wr.astype(BF16)
    w_lo = (wr - w_hi.astype(F32)).astype(BF16)
    logits = _dot(h_hi, w_hi) + _dot(h_hi, w_lo) + _dot(h_lo, w_hi) + br_ref[...]

    n_exp = logits.shape[1]
    lane = lax.broadcasted_iota(I32, logits.shape, 1).astype(F32)
    work = logits
    sels, vals, idxs = [], [], []
    for _ in range(TOP_K):
        m = jnp.max(work, axis=-1, keepdims=True)
        idx = jnp.min(jnp.where(work == m, lane, float(n_exp)), axis=-1, keepdims=True)
        sel = lane == idx
        sels.append(sel)
        vals.append(m)
        idxs.append(idx)
        work = jnp.where(sel, -jnp.inf, work)
    exps = [jnp.exp(vv - vals[0]) for vv in vals]
    inv_den = 1.0 / (exps[0] + exps[1] + exps[2] + exps[3])

    chosen = jnp.zeros(logits.shape, F32)
    for sel in sels:
        chosen = chosen + jnp.where(sel, 1.0, 0.0)
    r_i = lax.broadcasted_iota(I32, (TM, TM), 0)
    c_i = lax.broadcasted_iota(I32, (TM, TM), 1)
    earlier = jnp.where(c_i < r_i, 1.0, 0.0).astype(BF16)
    rank = _dot(earlier, chosen.astype(BF16)) + carry_ref[...]
    carry_ref[...] = carry_ref[...] + jnp.sum(chosen, axis=0, keepdims=True)
    cnt_ref[...] = carry_ref[...]

    out_lane = lax.broadcasted_iota(I32, (TM, LANES), 1)
    packed = jnp.zeros((TM, LANES), F32)
    for kk in range(TOP_K):
        rank_k = jnp.sum(jnp.where(sels[kk], rank, 0.0), axis=-1, keepdims=True)
        packed = jnp.where(out_lane == kk, idxs[kk], packed)
        packed = jnp.where(out_lane == TOP_K + kk, rank_k, packed)
        packed = jnp.where(out_lane == 2 * TOP_K + kk, exps[kk] * inv_den, packed)
    route_ref[...] = packed


def _out0_kernel(attn_ref, bg_ref, p_ref, pprev_ref, pnext_ref, cw_ref, wo_ref,
                 x_ref, mod_ref, g_ref, wr_ref, br_ref,
                 x1_ref, h2_ref, route_ref, cnt_ref, carry_ref, *, ntb):
    i = pl.program_id(0)
    seg = i % ntb

    @pl.when(i == 0)
    def _():
        carry_ref[...] = jnp.zeros_like(carry_ref)

    p = p_ref[...].astype(F32)
    row = lax.broadcasted_iota(I32, p.shape, 0)
    has_prev = seg > 1
    has_next = jnp.logical_and(seg > 0, seg < ntb - 1)
    prev_row = jnp.where(has_prev, pprev_ref[SUBLANES - 1:SUBLANES, :].astype(F32), 0.0)
    next_row = jnp.where(has_next, pnext_ref[0:1, :].astype(F32), 0.0)
    before = jnp.where(row == 0, prev_row, pltpu.roll(p, 1, 0))
    after = jnp.where(row == TM - 1, next_row, pltpu.roll(p, TM - 1, 0))
    cw = cw_ref[...]
    conv = bg_ref[...].astype(F32) * (cw[0:1] * before + cw[1:2] * p + cw[2:3] * after)
    aw = attn_ref.shape[1]
    y = _dot(attn_ref[...], wo_ref[:aw, :]) + _dot(conv.astype(BF16), wo_ref[aw:, :])
    _route_tail(y, x_ref, mod_ref, g_ref, wr_ref, br_ref, x1_ref, h2_ref, route_ref, cnt_ref, carry_ref)


def _out1_kernel(hs_ref, gg_ref, wo_ref, x_ref, mod_ref, g_ref, wr_ref, br_ref,
                 x1_ref, h2_ref, route_ref, cnt_ref, carry_ref, *, ntb):
    i = pl.program_id(0)

    @pl.when(i == 0)
    def _():
        carry_ref[...] = jnp.zeros_like(carry_ref)

    @pl.when(i % ntb > 0)
    def _():
        rec = hs_ref[0].astype(F32) + hs_ref[1].astype(F32)
        y = _dot((rec * gg_ref[...].astype(F32)).astype(BF16), wo_ref[...])
        _route_tail(y, x_ref, mod_ref, g_ref, wr_ref, br_ref, x1_ref, h2_ref, route_ref, cnt_ref, carry_ref)


def _tail_specs(d, n_exp, ntb, latent_only):
    row = lambda i: (i, 0)
    const = lambda i: (0, 0)
    out_row = (lambda i: ((i // ntb) * (ntb - 1) + jnp.maximum(i % ntb - 1, 0), 0)) if latent_only else row
    in_specs = [
        pl.BlockSpec((TM, d), row),
        pl.BlockSpec((1, 6, d), lambda i: ((i // ntb) * 2 + jnp.minimum(i % ntb, 1), 0, 0)),
        pl.BlockSpec((1, d), const),
        pl.BlockSpec((d, n_exp), const),
        pl.BlockSpec((1, n_exp), const),
    ]
    out_specs = [
        pl.BlockSpec((TM, d), out_row),
        pl.BlockSpec((TM * (d // LANES), LANES), out_row),
        pl.BlockSpec((TM, LANES), out_row),
        pl.BlockSpec((1, n_exp), const),
    ]
    return in_specs, out_specs


def _tail_shapes(t, d, n_exp):
    return [jax.ShapeDtypeStruct((t, d), F32), jax.ShapeDtypeStruct((t * (d // LANES), LANES), F32),
            jax.ShapeDtypeStruct((t, LANES), F32), jax.ShapeDtypeStruct((1, n_exp), F32)]


def _out_proj0(attn, bg, p, conv_w, w_out, x, modv, g, w_r, b_r, ntb):
    t, d = x.shape
    n_exp = w_r.shape[1]
    bw = bg.shape[1]
    row = lambda i: (i, 0)
    const = lambda i: (0, 0)
    nblk = TM // SUBLANES
    tail_in, tail_out = _tail_specs(d, n_exp, ntb, False)
    return pl.pallas_call(
        functools.partial(_out0_kernel, ntb=ntb),
        grid=(t // TM,),
        in_specs=[
            pl.BlockSpec((TM, attn.shape[1]), row),
            pl.BlockSpec((TM, bw), row),
            pl.BlockSpec((TM, bw), row),
            pl.BlockSpec((SUBLANES, bw), lambda i: (jnp.maximum(i * nblk - 1, 0), 0)),
            pl.BlockSpec((SUBLANES, bw), lambda i: (jnp.minimum((i + 1) * nblk, t // SUBLANES - 1), 0)),
            pl.BlockSpec(conv_w.shape, const),
            pl.BlockSpec(w_out.shape, const),
        ] + tail_in,
        out_specs=tail_out,
        out_shape=_tail_shapes(t, d, n_exp),
        scratch_shapes=[pltpu.VMEM((1, n_exp), F32)],
        compiler_params=_cparams(("arbitrary",)),
        name="l0_out_proj_router",
    )(attn, bg, p, p, p, conv_w, w_out, x, modv, g, w_r, b_r)


def _out_proj1(hs, gg, w_out, x, modv, g, w_r, b_r, ntb):
    t, d = x.shape
    n_exp = w_r.shape[1]
    row = lambda i: (i, 0)
    const = lambda i: (0, 0)
    tail_in, tail_out = _tail_specs(d, n_exp, ntb, True)
    t_lat = t // ntb * (ntb - 1)
    return pl.pallas_call(
        functools.partial(_out1_kernel, ntb=ntb),
        grid=(t // TM,),
        in_specs=[
            pl.BlockSpec((2, TM, hs.shape[2]), lambda i: (0, i, 0)),
            pl.BlockSpec((TM, gg.shape[1]), row),
            pl.BlockSpec(w_out.shape, const),
        ] + tail_in,
        out_specs=tail_out,
        out_shape=_tail_shapes(t_lat, d, n_exp),
        scratch_shapes=[pltpu.VMEM((1, n_exp), F32)],
        compiler_params=_cparams(("arbitrary",)),
        name="l1_out_proj_router",
    )(hs, gg, w_out, x, modv, g, w_r, b_r)


PAIRS_PER_TILE = TM * TOP_K


def _stage_pos(pos_ref, pbuf, psem):
    i = pl.program_id(0)
    slot = i % 2

    def chunk(ti, s):
        return pltpu.make_async_copy(pos_ref.at[pl.ds(ti * PAIRS_PER_TILE, PAIRS_PER_TILE)],
                                     pbuf.at[pl.ds(s * PAIRS_PER_TILE, PAIRS_PER_TILE)], psem.at[s])

    @pl.when(i == 0)
    def _():
        chunk(0, 0).start()

    chunk(i, slot).wait()

    @pl.when(i + 1 < pl.num_programs(0))
    def _():
        chunk(i + 1, 1 - slot).start()

    return slot * PAIRS_PER_TILE


def _dispatch_kernel(pad_lo_ref, pad_hi_ref, pos_ref, h2_ref, xs_ref, pbuf, zbuf, psem, dsem, zsem):
    i = pl.program_id(0)
    nch = h2_ref.shape[0] // TM
    n_tails = pad_lo_ref.shape[0] - 1

    def zero_row(j):
        return pltpu.make_async_copy(zbuf.at[pl.ds(0, nch)], xs_ref.at[pl.ds(pl.multiple_of(j * nch, nch), nch)], zsem)

    def zero_tile(j):
        return pltpu.make_async_copy(zbuf, xs_ref.at[pl.ds(pl.multiple_of(j * TME * nch, TME * nch), TME * nch)], zsem)

    def over_padding(act):
        def tail(e, carry):
            def body(j, c):
                act(zero_row(j))
                return c
            return lax.fori_loop(pad_lo_ref[e], pad_hi_ref[e], body, carry)

        lax.fori_loop(0, n_tails, tail, 0)

        def tile(j, c):
            act(zero_tile(j))
            return c

        lax.fori_loop(pad_lo_ref[n_tails] // TME, pad_hi_ref[n_tails] // TME, tile, 0)

    @pl.when(i == 0)
    def _():
        zbuf[...] = jnp.zeros(zbuf.shape, zbuf.dtype)
        over_padding(lambda cp: cp.start())
        over_padding(lambda cp: cp.wait())

    off = _stage_pos(pos_ref, pbuf, psem)
    for r in range(TM):
        for kk in range(TOP_K):
            dst = pl.multiple_of(pbuf[off + r * TOP_K + kk] * nch, nch)
            pltpu.make_async_copy(h2_ref.at[pl.ds(r * nch, nch)], xs_ref.at[pl.ds(dst, nch)], dsem).start(
                priority=kk % 2)
    for _ in range(TOP_K):
        pltpu.make_async_copy(h2_ref, xs_ref.at[pl.ds(0, TM * nch)], dsem).wait()


def _dispatch(pos, pad_lo, pad_hi, h2, n_slots, d):
    nch = d // LANES
    n_tok = h2.shape[0] // nch
    grid_spec = pltpu.PrefetchScalarGridSpec(
        num_scalar_prefetch=2,
        grid=(n_tok // TM,),
        in_specs=[pl.BlockSpec(memory_space=pl.ANY),
                  pl.BlockSpec((TM * nch, LANES), lambda i, lo, hi: (i, 0))],
        out_specs=pl.BlockSpec(memory_space=pl.ANY),
        scratch_shapes=[pltpu.SMEM((2 * PAIRS_PER_TILE,), I32), pltpu.VMEM((TME * nch, LANES), F32),
                        pltpu.SemaphoreType.DMA((2,)), pltpu.SemaphoreType.DMA, pltpu.SemaphoreType.DMA],
    )
    return pl.pallas_call(
        _dispatch_kernel,
        grid_spec=grid_spec,
        out_shape=jax.ShapeDtypeStruct((n_slots * nch, LANES), F32),
        compiler_params=_cparams(("arbitrary",)),
        name="expert_dispatch",
    )(pad_lo, pad_hi, pos, h2)


def _moe_kernel(te_ref, meta_ref, x_ref, w1_ref, b1_ref, w2_ref, b2_ref, y_ref, w1b, w2b):
    i = pl.program_id(0)
    n_tiles = meta_ref[0]
    d = w1_ref.shape[2]
    ff = w2_ref.shape[2]
    nch = d // LANES

    @pl.when(i < n_tiles)
    def _():
        new_expert = jnp.logical_or(i == 0, te_ref[i] != te_ref[jnp.maximum(i - 1, 0)])

        @pl.when(new_expert)
        def _():
            w1b[...] = w1_ref[0, 0].astype(BF16)
            w2b[...] = w2_ref[0, 0].astype(BF16)

        x = jnp.concatenate([x_ref[pl.ds(c, TME, stride=nch), :] for c in range(nch)], axis=1)
        h = _dot(x.astype(BF16), w1b[...]) + b1_ref[0]
        glu = jnp.minimum(h[:, :ff], SWIGLU_LIMIT)
        lin = jnp.clip(h[:, ff:], -SWIGLU_LIMIT, SWIGLU_LIMIT)
        act = glu * _sigmoid(SWIGLU_ALPHA * glu) * (lin + 1.0)
        y = _dot(act.astype(BF16), w2b[...]) + b2_ref[0]
        for c in range(nch):
            y_ref[pl.ds(c, TME, stride=nch), :] = y[:, c * LANES:(c + 1) * LANES]

    @pl.when(i >= n_tiles)
    def _():
        y_ref[...] = jnp.zeros(y_ref.shape, y_ref.dtype)


def _routed_experts(tile_expert, meta, xs, w1, b1, w2, b2, layer):
    _, n_exp, d, ff2 = w1.shape
    ff = w2.shape[2]
    nch = d // LANES
    n_grid = tile_expert.shape[0]
    used = lambda i, mt: jnp.minimum(i, mt[0] - 1)
    wmap = lambda i, te, mt: (layer, te[used(i, mt)], 0, 0)
    bmap = lambda i, te, mt: (layer * n_exp + te[used(i, mt)], 0, 0)
    grid_spec = pltpu.PrefetchScalarGridSpec(
        num_scalar_prefetch=2,
        grid=(n_grid,),
        in_specs=[
            pl.BlockSpec((TME * nch, LANES), lambda i, te, mt: (used(i, mt), 0)),
            pl.BlockSpec((1, 1, d, ff2), wmap),
            pl.BlockSpec((1, 1, ff2), bmap),
            pl.BlockSpec((1, 1, ff, d), wmap),
            pl.BlockSpec((1, 1, d), bmap),
        ],
        out_specs=pl.BlockSpec((TME * nch, LANES), lambda i, te, mt: (i, 0)),
        scratch_shapes=[pltpu.VMEM((d, ff2), BF16), pltpu.VMEM((ff, d), BF16)],
    )
    return pl.pallas_call(
        _moe_kernel,
        grid_spec=grid_spec,
        out_shape=jax.ShapeDtypeStruct(xs.shape, F32),
        compiler_params=_cparams(("arbitrary",)),
        name="routed_experts",
    )(tile_expert, meta, xs, w1, b1.reshape(-1, 1, ff2), w2, b2.reshape(-1, 1, d))


def _collect_kernel(pos_ref, ys_ref, out_ref, pbuf, psem, csem):
    nch = out_ref.shape[2] // TM
    off = _stage_pos(pos_ref, pbuf, psem)
    for r in range(TM):
        for kk in range(TOP_K):
            src = pl.multiple_of(pbuf[off + r * TOP_K + kk] * nch, nch)
            pltpu.make_async_copy(ys_ref.at[pl.ds(src, nch)], out_ref.at[0, kk, pl.ds(r * nch, nch)], csem).start(
                priority=kk % 2)
    for kk in range(TOP_K):
        pltpu.make_async_copy(ys_ref.at[pl.ds(0, TM * nch)], out_ref.at[0, kk], csem).wait()


def _collect(pos, ys, n_tok, d):
    nch = d // LANES
    return pl.pallas_call(
        _collect_kernel,
        grid=(n_tok // TM,),
        in_specs=[pl.BlockSpec(memory_space=pl.ANY), pl.BlockSpec(memory_space=pl.ANY)],
        out_specs=pl.BlockSpec((1, TOP_K, TM * nch, LANES), lambda i: (i, 0, 0, 0)),
        out_shape=jax.ShapeDtypeStruct((n_tok // TM, TOP_K, TM * nch, LANES), F32),
        scratch_shapes=[pltpu.SMEM((2 * PAIRS_PER_TILE,), I32), pltpu.SemaphoreType.DMA((2,)),
                        pltpu.SemaphoreType.DMA],
        compiler_params=_cparams(("arbitrary",)),
        name="expert_collect",
    )(pos, ys)


def _plan_routes(route, counts, n_grid):
    n_exp = counts.shape[-1]
    cnt = counts.reshape(n_exp).astype(I32)
    tiles_per = (cnt + TME - 1) // TME
    tile_end = jnp.cumsum(tiles_per)
    offset = (tile_end - tiles_per) * TME
    n_tiles = tile_end[-1]
    tile_ids = jnp.minimum(jnp.arange(n_grid, dtype=I32), n_tiles - 1)
    tile_expert = jnp.sum((tile_ids[:, None] >= tile_end[None, :]).astype(I32), axis=1)
    eidx = route[:, :TOP_K].astype(I32)
    rank = route[:, TOP_K:2 * TOP_K].astype(I32)
    expert_ids = jnp.arange(n_exp, dtype=I32)
    pos = jnp.sum(jnp.where(eidx[..., None] == expert_ids, offset, 0), axis=-1) + rank
    pad_lo = jnp.concatenate([offset + cnt, (n_tiles * TME).reshape(1)])
    pad_hi = jnp.concatenate([tile_end * TME, jnp.full((1,), n_grid * TME, I32)])
    return tile_expert, n_tiles.reshape(1), pos.reshape(-1), pad_lo, pad_hi


def _moe_layer(route, counts, h2, w1, b1, w2, b2, layer):
    n_exp, d = w1.shape[1], w1.shape[2]
    n_tok = route.shape[0]
    n_grid = n_tok * TOP_K // TME + n_exp
    tile_expert, meta, pos, pad_lo, pad_hi = _plan_routes(route, counts, n_grid)
    xs = _dispatch(pos, pad_lo, pad_hi, h2, n_grid * TME, d)
    ys = _routed_experts(tile_expert, meta, xs, w1, b1, w2, b2, layer)
    return _collect(pos, ys, n_tok, d)


def _combine(y_ref, route_ref, d):
    nch = d // LANES
    gates = route_ref[...]
    gk = [jnp.broadcast_to(gates[:, 2 * TOP_K + kk:2 * TOP_K + kk + 1], (TM, LANES)) for kk in range(TOP_K)]
    chunks = []
    for c in range(nch):
        acc = gk[0] * y_ref[0, 0, pl.ds(c, TM, stride=nch), :]
        for kk in range(1, TOP_K):
            acc = acc + gk[kk] * y_ref[0, kk, pl.ds(c, TM, stride=nch), :]
        chunks.append(acc)
    return jnp.concatenate(chunks, axis=1)


def _choice_spec(d):
    return pl.BlockSpec((1, TOP_K, TM * (d // LANES), LANES), lambda i: (i, 0, 0, 0))


def _in1_kernel(x_ref, y_ref, route_ref, mod0_ref, mod1_ref, g_ref, w_ref, x2_ref, gg_ref, u_ref):
    x2 = x_ref[...] + mod0_ref[0][5:6] * _combine(y_ref, route_ref, x_ref.shape[1])
    x2_ref[...] = x2
    mod1 = mod1_ref[0]
    h = _norm_mod(x2, g_ref[...], mod1[0:1], mod1[1:2])
    y = _dot(h.astype(BF16), w_ref[...])
    half = y.shape[1] // 2
    gg_ref[...] = jax.nn.gelu(y[:, :half]).astype(BF16)
    u_ref[...] = y[:, half:]


def _in_proj1(x, ybuf, route, modv0, modv1, g, w, ntb):
    t, d = x.shape
    n = w.shape[1]
    row = lambda i: (i, 0)
    mod_map = lambda i: ((i // ntb) * 2 + jnp.minimum(i % ntb, 1), 0, 0)
    return pl.pallas_call(
        _in1_kernel,
        grid=(t // TM,),
        in_specs=[
            pl.BlockSpec((TM, d), row),
            _choice_spec(d),
            pl.BlockSpec((TM, LANES), row),
            pl.BlockSpec((1, 6, d), mod_map),
            pl.BlockSpec((1, 6, d), mod_map),
            pl.BlockSpec((1, d), lambda i: (0, 0)),
            pl.BlockSpec((d, n), lambda i: (0, 0)),
        ],
        out_specs=[pl.BlockSpec((TM, d), row), pl.BlockSpec((TM, n // 2), row), pl.BlockSpec((TM, n // 2), row)],
        out_shape=[jax.ShapeDtypeStruct((t, d), F32), jax.ShapeDtypeStruct((t, n // 2), BF16),
                   jax.ShapeDtypeStruct((t, n // 2), F32)],
        compiler_params=_cparams(("parallel",)),
        name="l1_combine_in_proj",
    )(x, ybuf, route, modv0, modv1, g, w)


def _rglru_kernel(u_ref, halo_ref, cw_ref, cb_ref, gaw_ref, gab_ref, gxw_ref, gxb_ref, lam_ref, o_ref,
                  ext, a_s, b_s, h_s, state, *, ntb):
    d = pl.program_id(0)
    s = pl.program_id(2)
    nb, tc, cw_ = u_ref.shape
    u = u_ref[...]
    cw = cw_ref[0]

    def finish(xc, reverse):
        xc2 = xc.reshape(nb * tc, cw_) + cb_ref[0]
        xb = xc2.astype(BF16)
        t_r = jnp.tanh(_dot(xb, gaw_ref[0, 0]) + gab_ref[0])
        t_i = jnp.tanh(_dot(xb, gxw_ref[0, 0]) + gxb_ref[0])
        nl = -lam_ref[0]
        softplus = jnp.maximum(nl, 0.0) + jnp.log1p(jnp.exp(-jnp.abs(nl)))
        half_rate = (-0.5 * LRU_C) * softplus
        log_a = half_rate * t_r + half_rate
        a = jnp.exp(log_a)
        half_x = 0.5 * xc2
        bb = jnp.sqrt(1.0 - a * a) * (half_x * t_i + half_x)
        n_lane = cw_ // LANES
        pitch = tc + SCAN_PAD
        for c in range(n_lane):
            for bi in range(nb):
                a_s[c, bi * pitch:bi * pitch + tc, :] = a[bi * tc:(bi + 1) * tc, c * LANES:(c + 1) * LANES]
                b_s[c, bi * pitch:bi * pitch + tc, :] = bb[bi * tc:(bi + 1) * tc, c * LANES:(c + 1) * LANES]

        @pl.when(s == 0)
        def _():
            state[...] = jnp.zeros_like(state)

        def steps(tb, hs):
            hs = list(hs)
            for uu in range(SCAN_UNROLL):
                tt = tb * SCAN_UNROLL + uu
                t = (tc - 1 - tt) if reverse else tt
                rows = pl.ds(t, nb, stride=pitch)
                for c in range(n_lane):
                    hs[c] = a_s[c, rows, :] * hs[c] + b_s[c, rows, :]
                    h_s[c, rows, :] = hs[c]
            return tuple(hs)

        hs = lax.fori_loop(0, tc // SCAN_UNROLL, steps, tuple(state[c] for c in range(n_lane)))
        for c in range(n_lane):
            state[c] = hs[c]
            for bi in range(nb):
                o_ref[0, bi, :, c * LANES:(c + 1) * LANES] = h_s[c, bi * pitch:bi * pitch + tc, :].astype(o_ref.dtype)

    @pl.when(d == 0)
    def _():
        chunk = s
        keep = chunk > 1
        ext[:, 0:SUBLANES, :] = jnp.where(keep, halo_ref[...], 0.0)
        ext[:, SUBLANES:, :] = u
        xc = cw[3:4] * u
        for j in range(1, C_CONV):
            xc = xc + cw[3 - j:4 - j] * ext[:, SUBLANES - j:SUBLANES - j + tc, :]
        finish(xc, False)

    @pl.when(d == 1)
    def _():
        chunk = jnp.where(s == 0, 0, ntb - s)
        keep = jnp.logical_and(chunk > 0, chunk < ntb - 1)
        ext[:, 0:tc, :] = u
        ext[:, tc:, :] = jnp.where(keep, halo_ref[...], 0.0)
        xc = cw[0:1] * u
        for j in range(1, C_CONV):
            xc = xc + cw[j:j + 1] * ext[:, j:j + tc, :]
        finish(xc, True)


def _rglru(u, conv_w, conv_b, ga_w, ga_b, gx_w, gx_b, lam, b, l):
    width = u.shape[1]
    cb = width // C_BLOCKS
    ntb = l // TM
    u3 = u.reshape(b, l, width)
    nblk = TM // SUBLANES

    def chunk_of(d, s):
        return jnp.where(d == 0, s, jnp.where(s == 0, 0, ntb - s))

    def halo_of(d, s):
        c = chunk_of(d, s)
        return jnp.where(d == 0, jnp.maximum(c * nblk - 1, 0), jnp.minimum((c + 1) * nblk, l // SUBLANES - 1))

    vec = pl.BlockSpec((1, 1, cb), lambda d, g, s: (d, 0, g))
    mat = pl.BlockSpec((1, 1, cb, cb), lambda d, g, s: (d, g, 0, 0))
    return pl.pallas_call(
        functools.partial(_rglru_kernel, ntb=ntb),
        grid=(2, C_BLOCKS, ntb),
        in_specs=[
            pl.BlockSpec((b, TM, cb), lambda d, g, s: (0, chunk_of(d, s), g)),
            pl.BlockSpec((b, SUBLANES, cb), lambda d, g, s: (0, halo_of(d, s), g)),
            pl.BlockSpec((1, C_CONV, cb), lambda d, g, s: (d, 0, g)),
            vec, mat, vec, mat, vec, vec,
        ],
        out_specs=pl.BlockSpec((1, b, TM, cb), lambda d, g, s: (d, 0, chunk_of(d, s), g)),
        out_shape=jax.ShapeDtypeStruct((2, b, l, width), BF16),
        scratch_shapes=[
            pltpu.VMEM((b, TM + SUBLANES, cb), F32),
            pltpu.VMEM((cb // LANES, b * (TM + SCAN_PAD), LANES), F32),
            pltpu.VMEM((cb // LANES, b * (TM + SCAN_PAD), LANES), F32),
            pltpu.VMEM((cb // LANES, b * (TM + SCAN_PAD), LANES), F32),
            pltpu.VMEM((cb // LANES, b, LANES), F32),
        ],
        compiler_params=_cparams(("arbitrary", "arbitrary", "arbitrary")),
        name="l1_rglru",
    )(u3, u3, conv_w, conv_b.reshape(2, 1, width), (0.5 * ga_w).astype(BF16), (0.5 * ga_b).reshape(2, 1, width),
      (0.5 * gx_w).astype(BF16), (0.5 * gx_b).reshape(2, 1, width), lam.reshape(2, 1, width))


def _final_kernel(x_ref, y_ref, route_ref, mod_ref, g_ref, o_ref):
    x = x_ref[...] + mod_ref[0][5:6] * _combine(y_ref, route_ref, x_ref.shape[1])
    ms = jnp.mean(x * x, axis=-1, keepdims=True)
    o_ref[...] = x * lax.rsqrt(ms + NORM_EPS) * g_ref[...]


def _final(x, ybuf, route, modv, g, b, s_len):
    t, d = x.shape
    per_b = s_len // TM
    row = lambda i: (i, 0)
    return pl.pallas_call(
        _final_kernel,
        grid=(t // TM,),
        in_specs=[
            pl.BlockSpec((TM, d), row),
            _choice_spec(d),
            pl.BlockSpec((TM, LANES), row),
            pl.BlockSpec((1, 6, d), lambda i: ((i // per_b) * 2 + 1, 0, 0)),
            pl.BlockSpec((1, d), lambda i: (0, 0)),
        ],
        out_specs=pl.BlockSpec((TM, d), row),
        out_shape=jax.ShapeDtypeStruct((t, d), F32),
        compiler_params=_cparams(("parallel",)),
        name="final_combine_norm",
    )(x, ybuf, route, modv, g)


def _rope_tables(s_len, n_ctx):
    n_rows = s_len // GRID_W
    rows, cols = jnp.meshgrid(jnp.arange(n_rows), jnp.arange(GRID_W), indexing="ij")
    pos = jnp.stack([rows.reshape(-1), cols.reshape(-1)], axis=-1).astype(F32)
    n_freq = HEAD_DIM // 4
    inv = ROPE_THETA ** (-jnp.arange(n_freq, dtype=F32) / n_freq)
    ang = pos[:, :, None] * inv
    cos, sin = jnp.cos(ang), jnp.sin(ang)
    cos64 = jnp.stack([cos, cos], axis=2).reshape(s_len, HEAD_DIM)
    sin64 = jnp.stack([-sin, sin], axis=2).reshape(s_len, HEAD_DIM)
    cos_l = jnp.tile(cos64, (1, LANES // HEAD_DIM))
    sin_l = jnp.tile(sin64, (1, LANES // HEAD_DIM))
    cos_t = jnp.concatenate([jnp.ones((n_ctx, LANES), F32), cos_l], axis=0)
    sin_t = jnp.concatenate([jnp.zeros((n_ctx, LANES), F32), sin_l], axis=0)
    return cos_t, sin_t


def kernel(x, c, ctx, c_ctx, w_mod, b_mod, norm_mix, norm_ffn, ev_w_in, ev_w_out, ev_lambda_q1, ev_lambda_k1, ev_lambda_q2, ev_lambda_k2, ev_subln, ev_conv_w, od_w_in, od_w_out, od_conv_w, od_conv_b, od_gate_a_w, od_gate_a_b, od_gate_x_w, od_gate_x_b, od_lru_lambda, moe_w_router, moe_b_router, moe_w1, moe_b1, moe_w2, moe_b2, final_norm):
    b, s_len, d = x.shape
    n_ctx = ctx.shape[1]
    l = n_ctx + s_len
    t = b * l
    ntb = l // TM
    assert n_ctx == TM and s_len % TM == 0 and w_mod.shape[0] == 2

    xs = jnp.concatenate([ctx, x], axis=1).reshape(t, d)

    n_rows = -(-(b + 1) // SUBLANES) * SUBLANES
    cs = jnp.concatenate([c, c_ctx[None, :], jnp.zeros((n_rows - b - 1, d), F32)], axis=0)
    mod = _modulation(cs, w_mod, b_mod)

    def mod_table(i):
        lat = mod[i, :b]
        cx = jnp.broadcast_to(mod[i, b][None, :], lat.shape)
        return jnp.stack([cx, lat], axis=1).reshape(b * 2, 6, d)

    modv0, modv1 = mod_table(0), mod_table(1)

    cos_t, sin_t = _rope_tables(s_len, n_ctx)
    q, k, v, bg, p = _in_proj0(xs, modv0, norm_mix[0:1], ev_w_in[0].astype(BF16), cos_t, sin_t, ntb)
    lam_init = 0.8 - 0.6 * math.exp(-0.3 * 0)
    attn = _diff_attention(q, k, v, ev_lambda_q1[0:1], ev_lambda_k1[0:1], ev_lambda_q2[0:1], ev_lambda_k2[0:1],
                           ev_subln[0:1], lam_init, b, l, n_ctx)
    x1, h2, route0, counts0 = _out_proj0(attn, bg, p, ev_conv_w[0], ev_w_out[0].astype(BF16), xs, modv0,
                                         norm_ffn[0:1], moe_w_router[0], moe_b_router[0:1], ntb)
    ybuf0 = _moe_layer(route0, counts0, h2, moe_w1, moe_b1, moe_w2, moe_b2, 0)

    x2, gg, u = _in_proj1(x1, ybuf0, route0, modv0, modv1, norm_mix[1:2], od_w_in[0].astype(BF16), ntb)
    hs = _rglru(u, od_conv_w[0], od_conv_b[0], od_gate_a_w[0], od_gate_a_b[0], od_gate_x_w[0], od_gate_x_b[0],
                od_lru_lambda[0], b, l)
    x3, h3, route1, counts1 = _out_proj1(hs.reshape(2, t, hs.shape[-1]), gg, od_w_out[0].astype(BF16), x2, modv1,
                                         norm_ffn[1:2], moe_w_router[1], moe_b_router[1:2], ntb)
    ybuf1 = _moe_layer(route1, counts1, h3, moe_w1, moe_b1, moe_w2, moe_b2, 1)

    out = _final(x3, ybuf1, route1, modv1, final_norm[None, :], b, s_len)
    return out.reshape(b, s_len, d)
```

```python
import functools
import math

import jax
import jax.numpy as jnp
from jax import lax
from jax.experimental import pallas as pl
from jax.experimental.pallas import tpu as pltpu

F32 = jnp.float32
BF16 = jnp.bfloat16
I32 = jnp.int32

NORM_EPS = 1e-6
ROPE_THETA = 10000.0
GRID_W = 64
N_HEADS = 4
HEAD_DIM = 64
A_WIDTH = 2 * N_HEADS * HEAD_DIM
B_CONV = 3
C_CONV = 4
C_BLOCKS = 4
LRU_C = 8.0
TOP_K = 4
SWIGLU_ALPHA = 1.702
SWIGLU_LIMIT = 7.0

LANES = 128
SUBLANES = 8
TM = 256
TME = 256
VMEM_LIMIT = 56 * 1024 * 1024
SCAN_PAD = 8
SCAN_UNROLL = 8
LOG2_E = 1.4426950408889634


def _cparams(sem, vmem=VMEM_LIMIT):
    return pltpu.CompilerParams(dimension_semantics=sem, vmem_limit_bytes=vmem)


def _norm_mod(x, g, shift, scale):
    ms = jnp.mean(x * x, axis=-1, keepdims=True)
    return (x * lax.rsqrt(ms + NORM_EPS) * g) * (1.0 + scale) + shift


def _dot(a, b):
    return jnp.dot(a, b, preferred_element_type=F32)


def _sigmoid(x):
    return 0.5 * jnp.tanh(0.5 * x) + 0.5


def _mod_kernel(cs_ref, w_ref, b_ref, o_ref):
    s = cs_ref[...]
    s = s * jax.nn.sigmoid(s)
    o_ref[0] = _dot(s.astype(BF16), w_ref[0].astype(BF16)) + b_ref[0]


def _modulation(cs, w_mod, b_mod):
    depth, d, n = w_mod.shape
    rows = cs.shape[0]
    tn = 1536
    return pl.pallas_call(
        _mod_kernel,
        grid=(depth, n // tn),
        in_specs=[
            pl.BlockSpec((rows, d), lambda i, j: (0, 0)),
            pl.BlockSpec((1, d, tn), lambda i, j: (i, 0, j)),
            pl.BlockSpec((1, 1, tn), lambda i, j: (i, 0, j)),
        ],
        out_specs=pl.BlockSpec((1, rows, tn), lambda i, j: (i, 0, j)),
        out_shape=jax.ShapeDtypeStruct((depth, rows, n), F32),
        compiler_params=_cparams(("parallel", "parallel")),
        name="adaln_modulation",
    )(cs, w_mod, b_mod.reshape(depth, 1, n))


def _in0_kernel(x_ref, mod_ref, g_ref, w_ref, cos_ref, sin_ref,
                q_ref, k_ref, v_ref, bg_ref, p_ref):
    mod = mod_ref[0]
    h = _norm_mod(x_ref[...], g_ref[...], mod[0:1], mod[1:2])
    y = _dot(h.astype(BF16), w_ref[...])
    cosv = cos_ref[...]
    sinv = sin_ref[...]
    lane = lax.broadcasted_iota(I32, (TM, LANES), 1)
    first_half = (lane & 16) == 0

    def rope(z):
        outs = []
        for g in range(A_WIDTH // LANES):
            zg = z[:, g * LANES:(g + 1) * LANES]
            partner = jnp.where(first_half, pltpu.roll(zg, LANES - 16, 1), pltpu.roll(zg, 16, 1))
            outs.append(zg * cosv + partner * sinv)
        return jnp.concatenate(outs, axis=1)

    aw = A_WIDTH
    q_ref[...] = (rope(y[:, :aw]) * (HEAD_DIM ** -0.5 * LOG2_E)).astype(BF16)
    k_ref[...] = rope(y[:, aw:2 * aw]).astype(BF16)
    v_ref[...] = y[:, 2 * aw:3 * aw].astype(BF16)
    bw = (y.shape[1] - 3 * aw) // 3
    bg_ref[...] = y[:, 3 * aw:3 * aw + bw].astype(BF16)
    p_ref[...] = (y[:, 3 * aw + bw:3 * aw + 2 * bw] * y[:, 3 * aw + 2 * bw:]).astype(BF16)


def _in_proj0(x, modv, g, w, cos_t, sin_t, ntb):
    t, d = x.shape
    n = w.shape[1]
    bw = (n - 3 * A_WIDTH) // 3
    row = lambda i: (i, 0)
    return pl.pallas_call(
        _in0_kernel,
        grid=(t // TM,),
        in_specs=[
            pl.BlockSpec((TM, d), row),
            pl.BlockSpec((1, 6, d), lambda i: ((i // ntb) * 2 + jnp.minimum(i % ntb, 1), 0, 0)),
            pl.BlockSpec((1, d), lambda i: (0, 0)),
            pl.BlockSpec((d, n), lambda i: (0, 0)),
            pl.BlockSpec((TM, LANES), lambda i: (i % ntb, 0)),
            pl.BlockSpec((TM, LANES), lambda i: (i % ntb, 0)),
        ],
        out_specs=[pl.BlockSpec((TM, A_WIDTH), row)] * 3 + [pl.BlockSpec((TM, bw), row)] * 2,
        out_shape=[jax.ShapeDtypeStruct((t, A_WIDTH), BF16)] * 3 + [jax.ShapeDtypeStruct((t, bw), BF16)] * 2,
        compiler_params=_cparams(("parallel",)),
        name="l0_in_proj_rope",
    )(x, modv, g, w, cos_t, sin_t)


def _attn_kernel(lq1_ref, lk1_ref, lq2_ref, lk2_ref, g_ref, q_ref, k_ref, v_ref, o_ref, *, lam_init, n_ctx):
    qi = pl.program_id(2)
    lam = (jnp.exp(jnp.sum(lq1_ref[...] * lk1_ref[...], axis=-1, keepdims=True))
           - jnp.exp(jnp.sum(lq2_ref[...] * lk2_ref[...], axis=-1, keepdims=True)) + lam_init)
    q = q_ref[...]
    lane = lax.broadcasted_iota(I32, q.shape, 1)
    zero = jnp.zeros_like(q)
    q1 = jnp.where(lane < HEAD_DIM, q, zero)
    q2 = jnp.where(lane < HEAD_DIM, zero, q)
    contract_last = (((1,), (1,)), ((), ()))

    def attend(nk):
        k = k_ref[0, :nk, :]
        v = v_ref[0, :nk, :]
        s1 = lax.dot_general(q1, k, contract_last, preferred_element_type=F32)
        s2 = lax.dot_general(q2, k, contract_last, preferred_element_type=F32)
        p1 = jnp.exp2(s1 - jnp.max(s1, axis=-1, keepdims=True))
        p2 = jnp.exp2(s2 - jnp.max(s2, axis=-1, keepdims=True))
        r1 = 1.0 / jnp.sum(p1, axis=-1, keepdims=True)
        r2 = lam / jnp.sum(p2, axis=-1, keepdims=True)
        o = _dot(p1.astype(BF16), v) * r1 - _dot(p2.astype(BF16), v) * r2
        ms = jnp.mean(o * o, axis=-1, keepdims=True)
        o = o * lax.rsqrt(ms + NORM_EPS) * g_ref[...] * (1.0 - lam_init)
        o_ref[...] = o.astype(BF16)

    @pl.when(qi == 0)
    def _():
        attend(n_ctx)

    @pl.when(qi > 0)
    def _():
        attend(k_ref.shape[1])


def _diff_attention(q, k, v, lq1, lk1, lq2, lk2, subln, lam_init, b, l, n_ctx):
    t = q.shape[0]
    ntb = l // TM
    hw = 2 * HEAD_DIM
    k3 = k.reshape(b, l, A_WIDTH)
    v3 = v.reshape(b, l, A_WIDTH)
    vec = lambda n: pl.BlockSpec((1, n), lambda bi, h, qi: (0, 0))
    qspec = pl.BlockSpec((TM, hw), lambda bi, h, qi: (bi * ntb + qi, h))
    kspec = pl.BlockSpec((1, l, hw), lambda bi, h, qi: (bi, 0, h))
    return pl.pallas_call(
        functools.partial(_attn_kernel, lam_init=lam_init, n_ctx=n_ctx),
        grid=(b, N_HEADS, ntb),
        in_specs=[vec(HEAD_DIM)] * 4 + [vec(hw), qspec, kspec, kspec],
        out_specs=qspec,
        out_shape=jax.ShapeDtypeStruct((t, A_WIDTH), BF16),
        compiler_params=_cparams(("parallel", "parallel", "parallel")),
        name="l0_diff_attention",
    )(lq1, lk1, lq2, lk2, subln, q, k3, v3)


def _route_tail(y, x_ref, mod_ref, g_ref, wr_ref, br_ref,
                x1_ref, h2_ref, route_ref, cnt_ref, carry_ref):
    mod = mod_ref[0]
    x1 = x_ref[...] + mod[2:3] * y
    x1_ref[...] = x1
    h2 = _norm_mod(x1, g_ref[...], mod[3:4], mod[4:5])
    nch = h2.shape[1] // LANES
    for s in range(nch):
        h2_ref[pl.ds(s, TM, stride=nch), :] = h2[:, s * LANES:(s + 1) * LANES]

    wr = wr_ref[...]
    h_hi = h2.astype(BF16)
    h_lo = (h2 - h_hi.astype(F32)).astype(BF16)
    w_hi = wr.astype(BF16)
    w_lo = (wr - w_hi.astype(F32)).astype(BF16)
    logits = _dot(h_hi, w_hi) + _dot(h_hi, w_lo) + _dot(h_lo, w_hi) + br_ref[...]

    n_exp = logits.shape[1]
    lane = lax.broadcasted_iota(I32, logits.shape, 1).astype(F32)
    work = logits
    sels, vals, idxs = [], [], []
    for _ in range(TOP_K):
        m = jnp.max(work, axis=-1, keepdims=True)
        idx = jnp.min(jnp.where(work == m, lane, float(n_exp)), axis=-1, keepdims=True)
        sel = lane == idx
        sels.append(sel)
        vals.append(m)
        idxs.append(idx)
        work = jnp.where(sel, -jnp.inf, work)
    exps = [jnp.exp(vv - vals[0]) for vv in vals]
    inv_den = 1.0 / (exps[0] + exps[1] + exps[2] + exps[3])

    chosen = jnp.zeros(logits.shape, F32)
    for sel in sels:
        chosen = chosen + jnp.where(sel, 1.0, 0.0)
    r_i = lax.broadcasted_iota(I32, (TM, TM), 0)
    c_i = lax.broadcasted_iota(I32, (TM, TM), 1)
    earlier = jnp.where(c_i < r_i, 1.0, 0.0).astype(BF16)
    rank = _dot(earlier, chosen.astype(BF16)) + carry_ref[...]
    carry_ref[...] = carry_ref[...] + jnp.sum(chosen, axis=0, keepdims=True)
    cnt_ref[...] = carry_ref[...]

    out_lane = lax.broadcasted_iota(I32, (TM, LANES), 1)
    packed = jnp.zeros((TM, LANES), F32)
    for kk in range(TOP_K):
        rank_k = jnp.sum(jnp.where(sels[kk], rank, 0.0), axis=-1, keepdims=True)
        packed = jnp.where(out_lane == kk, idxs[kk], packed)
        packed = jnp.where(out_lane == TOP_K + kk, rank_k, packed)
        packed = jnp.where(out_lane == 2 * TOP_K + kk, exps[kk] * inv_den, packed)
    route_ref[...] = packed


def _out0_kernel(attn_ref, bg_ref, p_ref, pprev_ref, pnext_ref, cw_ref, wo_ref,
                 x_ref, mod_ref, g_ref, wr_ref, br_ref,
                 x1_ref, h2_ref, route_ref, cnt_ref, carry_ref, *, ntb):
    i = pl.program_id(0)
    seg = i % ntb

    @pl.when(i == 0)
    def _():
        carry_ref[...] = jnp.zeros_like(carry_ref)

    p = p_ref[...].astype(F32)
    row = lax.broadcasted_iota(I32, p.shape, 0)
    has_prev = seg > 1
    has_next = jnp.logical_and(seg > 0, seg < ntb - 1)
    prev_row = jnp.where(has_prev, pprev_ref[SUBLANES - 1:SUBLANES, :].astype(F32), 0.0)
    next_row = jnp.where(has_next, pnext_ref[0:1, :].astype(F32), 0.0)
    before = jnp.where(row == 0, prev_row, pltpu.roll(p, 1, 0))
    after = jnp.where(row == TM - 1, next_row, pltpu.roll(p, TM - 1, 0))
    cw = cw_ref[...]
    conv = bg_ref[...].astype(F32) * (cw[0:1] * before + cw[1:2] * p + cw[2:3] * after)
    aw = attn_ref.shape[1]
    y = _dot(attn_ref[...], wo_ref[:aw, :]) + _dot(conv.astype(BF16), wo_ref[aw:, :])
    _route_tail(y, x_ref, mod_ref, g_ref, wr_ref, br_ref, x1_ref, h2_ref, route_ref, cnt_ref, carry_ref)


def _out1_kernel(hs_ref, gg_ref, wo_ref, x_ref, mod_ref, g_ref, wr_ref, br_ref,
                 x1_ref, h2_ref, route_ref, cnt_ref, carry_ref, *, ntb):
    i = pl.program_id(0)

    @pl.when(i == 0)
    def _():
        carry_ref[...] = jnp.zeros_like(carry_ref)

    @pl.when(i % ntb > 0)
    def _():
        rec = hs_ref[0].astype(F32) + hs_ref[1].astype(F32)
        y = _dot((rec * gg_ref[...].astype(F32)).astype(BF16), wo_ref[...])
        _route_tail(y, x_ref, mod_ref, g_ref, wr_ref, br_ref, x1_ref, h2_ref, route_ref, cnt_ref, carry_ref)


def _tail_specs(d, n_exp, ntb, latent_only):
    row = lambda i: (i, 0)
    const = lambda i: (0, 0)
    out_row = (lambda i: ((i // ntb) * (ntb - 1) + jnp.maximum(i % ntb - 1, 0), 0)) if latent_only else row
    in_specs = [
        pl.BlockSpec((TM, d), row),
        pl.BlockSpec((1, 6, d), lambda i: ((i // ntb) * 2 + jnp.minimum(i % ntb, 1), 0, 0)),
        pl.BlockSpec((1, d), const),
        pl.BlockSpec((d, n_exp), const),
        pl.BlockSpec((1, n_exp), const),
    ]
    out_specs = [
        pl.BlockSpec((TM, d), out_row),
        pl.BlockSpec((TM * (d // LANES), LANES), out_row),
        pl.BlockSpec((TM, LANES), out_row),
        pl.BlockSpec((1, n_exp), const),
    ]
    return in_specs, out_specs


def _tail_shapes(t, d, n_exp):
    return [jax.ShapeDtypeStruct((t, d), F32), jax.ShapeDtypeStruct((t * (d // LANES), LANES), F32),
            jax.ShapeDtypeStruct((t, LANES), F32), jax.ShapeDtypeStruct((1, n_exp), F32)]


def _out_proj0(attn, bg, p, conv_w, w_out, x, modv, g, w_r, b_r, ntb):
    t, d = x.shape
    n_exp = w_r.shape[1]
    bw = bg.shape[1]
    row = lambda i: (i, 0)
    const = lambda i: (0, 0)
    nblk = TM // SUBLANES
    tail_in, tail_out = _tail_specs(d, n_exp, ntb, False)
    return pl.pallas_call(
        functools.partial(_out0_kernel, ntb=ntb),
        grid=(t // TM,),
        in_specs=[
            pl.BlockSpec((TM, attn.shape[1]), row),
            pl.BlockSpec((TM, bw), row),
            pl.BlockSpec((TM, bw), row),
            pl.BlockSpec((SUBLANES, bw), lambda i: (jnp.maximum(i * nblk - 1, 0), 0)),
            pl.BlockSpec((SUBLANES, bw), lambda i: (jnp.minimum((i + 1) * nblk, t // SUBLANES - 1), 0)),
            pl.BlockSpec(conv_w.shape, const),
            pl.BlockSpec(w_out.shape, const),
        ] + tail_in,
        out_specs=tail_out,
        out_shape=_tail_shapes(t, d, n_exp),
        scratch_shapes=[pltpu.VMEM((1, n_exp), F32)],
        compiler_params=_cparams(("arbitrary",)),
        name="l0_out_proj_router",
    )(attn, bg, p, p, p, conv_w, w_out, x, modv, g, w_r, b_r)


def _out_proj1(hs, gg, w_out, x, modv, g, w_r, b_r, ntb):
    t, d = x.shape
    n_exp = w_r.shape[1]
    row = lambda i: (i, 0)
    const = lambda i: (0, 0)
    tail_in, tail_out = _tail_specs(d, n_exp, ntb, True)
    t_lat = t // ntb * (ntb - 1)
    return pl.pallas_call(
        functools.partial(_out1_kernel, ntb=ntb),
        grid=(t // TM,),
        in_specs=[
            pl.BlockSpec((2, TM, hs.shape[2]), lambda i: (0, i, 0)),
            pl.BlockSpec((TM, gg.shape[1]), row),
            pl.BlockSpec(w_out.shape, const),
        ] + tail_in,
        out_specs=tail_out,
        out_shape=_tail_shapes(t_lat, d, n_exp),
        scratch_shapes=[pltpu.VMEM((1, n_exp), F32)],
        compiler_params=_cparams(("arbitrary",)),
        name="l1_out_proj_router",
    )(hs, gg, w_out, x, modv, g, w_r, b_r)


PAIRS_PER_TILE = TM * TOP_K


def _stage_pos(pos_ref, pbuf, psem):
    i = pl.program_id(0)
    slot = i % 2

    def chunk(ti, s):
        return pltpu.make_async_copy(pos_ref.at[pl.ds(ti * PAIRS_PER_TILE, PAIRS_PER_TILE)],
                                     pbuf.at[pl.ds(s * PAIRS_PER_TILE, PAIRS_PER_TILE)], psem.at[s])

    @pl.when(i == 0)
    def _():
        chunk(0, 0).start()

    chunk(i, slot).wait()

    @pl.when(i + 1 < pl.num_programs(0))
    def _():
        chunk(i + 1, 1 - slot).start()

    return slot * PAIRS_PER_TILE


def _dispatch_kernel(pad_lo_ref, pad_hi_ref, pos_ref, h2_ref, xs_ref, pbuf, zbuf, psem, dsem, zsem):
    i = pl.program_id(0)
    nch = h2_ref.shape[0] // TM
    n_tails = pad_lo_ref.shape[0] - 1

    def zero_row(j):
        return pltpu.make_async_copy(zbuf.at[pl.ds(0, nch)], xs_ref.at[pl.ds(pl.multiple_of(j * nch, nch), nch)], zsem)

    def zero_tile(j):
        return pltpu.make_async_copy(zbuf, xs_ref.at[pl.ds(pl.multiple_of(j * TME * nch, TME * nch), TME * nch)], zsem)

    def over_padding(act):
        def tail(e, carry):
            def body(j, c):
                act(zero_row(j))
                return c
            return lax.fori_loop(pad_lo_ref[e], pad_hi_ref[e], body, carry)

        lax.fori_loop(0, n_tails, tail, 0)

        def tile(j, c):
            act(zero_tile(j))
            return c

        lax.fori_loop(pad_lo_ref[n_tails] // TME, pad_hi_ref[n_tails] // TME, tile, 0)

    @pl.when(i == 0)
    def _():
        zbuf[...] = jnp.zeros(zbuf.shape, zbuf.dtype)
        over_padding(lambda cp: cp.start())
        over_padding(lambda cp: cp.wait())

    off = _stage_pos(pos_ref, pbuf, psem)
    for r in range(TM):
        for kk in range(TOP_K):
            dst = pl.multiple_of(pbuf[off + r * TOP_K + kk] * nch, nch)
            pltpu.make_async_copy(h2_ref.at[pl.ds(r * nch, nch)], xs_ref.at[pl.ds(dst, nch)], dsem).start(
                priority=kk % 2)
    for _ in range(TOP_K):
        pltpu.make_async_copy(h2_ref, xs_ref.at[pl.ds(0, TM * nch)], dsem).wait()


def _dispatch(pos, pad_lo, pad_hi, h2, n_slots, d):
    nch = d // LANES
    n_tok = h2.shape[0] // nch
    grid_spec = pltpu.PrefetchScalarGridSpec(
        num_scalar_prefetch=2,
        grid=(n_tok // TM,),
        in_specs=[pl.BlockSpec(memory_space=pl.ANY),
                  pl.BlockSpec((TM * nch, LANES), lambda i, lo, hi: (i, 0))],
        out_specs=pl.BlockSpec(memory_space=pl.ANY),
        scratch_shapes=[pltpu.SMEM((2 * PAIRS_PER_TILE,), I32), pltpu.VMEM((TME * nch, LANES), F32),
                        pltpu.SemaphoreType.DMA((2,)), pltpu.SemaphoreType.DMA, pltpu.SemaphoreType.DMA],
    )
    return pl.pallas_call(
        _dispatch_kernel,
        grid_spec=grid_spec,
        out_shape=jax.ShapeDtypeStruct((n_slots * nch, LANES), F32),
        compiler_params=_cparams(("arbitrary",)),
        name="expert_dispatch",
    )(pad_lo, pad_hi, pos, h2)


def _moe_kernel(te_ref, meta_ref, x_ref, w1_ref, b1_ref, w2_ref, b2_ref, y_ref, w1b, w2b):
    i = pl.program_id(0)
    n_tiles = meta_ref[0]
    d = w1_ref.shape[2]
    ff = w2_ref.shape[2]
    nch = d // LANES

    @pl.when(i < n_tiles)
    def _():
        new_expert = jnp.logical_or(i == 0, te_ref[i] != te_ref[jnp.maximum(i - 1, 0)])

        @pl.when(new_expert)
        def _():
            w1b[...] = w1_ref[0, 0].astype(BF16)
            w2b[...] = w2_ref[0, 0].astype(BF16)

        x = jnp.concatenate([x_ref[pl.ds(c, TME, stride=nch), :] for c in range(nch)], axis=1)
        h = _dot(x.astype(BF16), w1b[...]) + b1_ref[0]
        glu = jnp.minimum(h[:, :ff], SWIGLU_LIMIT)
        lin = jnp.clip(h[:, ff:], -SWIGLU_LIMIT, SWIGLU_LIMIT)
        act = glu * _sigmoid(SWIGLU_ALPHA * glu) * (lin + 1.0)
        y = _dot(act.astype(BF16), w2b[...]) + b2_ref[0]
        for c in range(nch):
            y_ref[pl.ds(c, TME, stride=nch), :] = y[:, c * LANES:(c + 1) * LANES]

    @pl.when(i >= n_tiles)
    def _():
        y_ref[...] = jnp.zeros(y_ref.shape, y_ref.dtype)


def _routed_experts(tile_expert, meta, xs, w1, b1, w2, b2, layer):
    _, n_exp, d, ff2 = w1.shape
    ff = w2.shape[2]
    nch = d // LANES
    n_grid = tile_expert.shape[0]
    used = lambda i, mt: jnp.minimum(i, mt[0] - 1)
    wmap = lambda i, te, mt: (layer, te[used(i, mt)], 0, 0)
    bmap = lambda i, te, mt: (layer * n_exp + te[used(i, mt)], 0, 0)
    grid_spec = pltpu.PrefetchScalarGridSpec(
        num_scalar_prefetch=2,
        grid=(n_grid,),
        in_specs=[
            pl.BlockSpec((TME * nch, LANES), lambda i, te, mt: (used(i, mt), 0)),
            pl.BlockSpec((1, 1, d, ff2), wmap),
            pl.BlockSpec((1, 1, ff2), bmap),
            pl.BlockSpec((1, 1, ff, d), wmap),
            pl.BlockSpec((1, 1, d), bmap),
        ],
        out_specs=pl.BlockSpec((TME * nch, LANES), lambda i, te, mt: (i, 0)),
        scratch_shapes=[pltpu.VMEM((d, ff2), BF16), pltpu.VMEM((ff, d), BF16)],
    )
    return pl.pallas_call(
        _moe_kernel,
        grid_spec=grid_spec,
        out_shape=jax.ShapeDtypeStruct(xs.shape, F32),
        compiler_params=_cparams(("arbitrary",)),
        name="routed_experts",
    )(tile_expert, meta, xs, w1, b1.reshape(-1, 1, ff2), w2, b2.reshape(-1, 1, d))


def _gather_choices(pos_ref, ys_ref, ybuf, pbuf, psem, csem):
    i = pl.program_id(0)
    n = pl.num_programs(0)
    slot = i % 2
    nch = ybuf.shape[2] // TM

    def chunk(ti, s):
        return pltpu.make_async_copy(pos_ref.at[pl.ds(ti * PAIRS_PER_TILE, PAIRS_PER_TILE)],
                                     pbuf.at[pl.ds(s * PAIRS_PER_TILE, PAIRS_PER_TILE)], psem.at[s])

    def row_copy(s, r, kk):
        src = pl.multiple_of(pbuf[s * PAIRS_PER_TILE + r * TOP_K + kk] * nch, nch)
        return pltpu.make_async_copy(ys_ref.at[pl.ds(src, nch)], ybuf.at[s, kk, pl.ds(r * nch, nch)], csem.at[s])

    @pl.when(i == 0)
    def _():
        chunk(0, 0).start()
        chunk(0, 0).wait()

        def first_rows(r, c):
            for kk in range(TOP_K):
                row_copy(0, r, kk).start(priority=kk % 2)
            return c

        lax.fori_loop(0, TM, first_rows, 0)

        @pl.when(n > 1)
        def _():
            chunk(1, 1).start()

    @pl.when(i + 1 < n)
    def _():
        chunk(i + 1, 1 - slot).wait()
        for r in range(TM):
            for kk in range(TOP_K):
                row_copy(1 - slot, r, kk).start(priority=kk % 2)

        @pl.when(i + 2 < n)
        def _():
            chunk(i + 2, slot).start()

    for kk in range(TOP_K):
        pltpu.make_async_copy(ys_ref.at[pl.ds(0, TM * nch)], ybuf.at[slot, kk], csem.at[slot]).wait()
    return slot


def _gather_scratch(d):
    nch = d // LANES
    return [pltpu.VMEM((2, TOP_K, TM * nch, LANES), F32), pltpu.SMEM((2 * PAIRS_PER_TILE,), I32),
            pltpu.SemaphoreType.DMA((2,)), pltpu.SemaphoreType.DMA((2,))]


def _plan_routes(route, counts, n_grid):
    n_exp = counts.shape[-1]
    cnt = counts.reshape(n_exp).astype(I32)
    tiles_per = (cnt + TME - 1) // TME
    tile_end = jnp.cumsum(tiles_per)
    offset = (tile_end - tiles_per) * TME
    n_tiles = tile_end[-1]
    tile_ids = jnp.minimum(jnp.arange(n_grid, dtype=I32), n_tiles - 1)
    tile_expert = jnp.sum((tile_ids[:, None] >= tile_end[None, :]).astype(I32), axis=1)
    eidx = route[:, :TOP_K].astype(I32)
    rank = route[:, TOP_K:2 * TOP_K].astype(I32)
    expert_ids = jnp.arange(n_exp, dtype=I32)
    pos = jnp.sum(jnp.where(eidx[..., None] == expert_ids, offset, 0), axis=-1) + rank
    pad_lo = jnp.concatenate([offset + cnt, (n_tiles * TME).reshape(1)])
    pad_hi = jnp.concatenate([tile_end * TME, jnp.full((1,), n_grid * TME, I32)])
    return tile_expert, n_tiles.reshape(1), pos.reshape(-1), pad_lo, pad_hi


def _moe_layer(route, counts, h2, w1, b1, w2, b2, layer):
    n_exp, d = w1.shape[1], w1.shape[2]
    n_tok = route.shape[0]
    n_grid = n_tok * TOP_K // TME + n_exp
    tile_expert, meta, pos, pad_lo, pad_hi = _plan_routes(route, counts, n_grid)
    xs = _dispatch(pos, pad_lo, pad_hi, h2, n_grid * TME, d)
    return pos, _routed_experts(tile_expert, meta, xs, w1, b1, w2, b2, layer)


def _combine(ybuf, slot, route_ref, d):
    nch = d // LANES
    gates = route_ref[...]
    gk = [jnp.broadcast_to(gates[:, 2 * TOP_K + kk:2 * TOP_K + kk + 1], (TM, LANES)) for kk in range(TOP_K)]
    chunks = []
    for c in range(nch):
        acc = gk[0] * ybuf[slot, 0, pl.ds(c, TM, stride=nch), :]
        for kk in range(1, TOP_K):
            acc = acc + gk[kk] * ybuf[slot, kk, pl.ds(c, TM, stride=nch), :]
        chunks.append(acc)
    return jnp.concatenate(chunks, axis=1)


def _in1_kernel(x_ref, pos_ref, ys_ref, route_ref, mod0_ref, mod1_ref, g_ref, w_ref, x2_ref, gg_ref, u_ref,
                ybuf, pbuf, psem, csem):
    slot = _gather_choices(pos_ref, ys_ref, ybuf, pbuf, psem, csem)
    x2 = x_ref[...] + mod0_ref[0][5:6] * _combine(ybuf, slot, route_ref, x_ref.shape[1])
    x2_ref[...] = x2
    mod1 = mod1_ref[0]
    h = _norm_mod(x2, g_ref[...], mod1[0:1], mod1[1:2])
    y = _dot(h.astype(BF16), w_ref[...])
    half = y.shape[1] // 2
    gg_ref[...] = jax.nn.gelu(y[:, :half]).astype(BF16)
    u_ref[...] = y[:, half:]


def _in_proj1(x, pos, ys, route, modv0, modv1, g, w, ntb):
    t, d = x.shape
    n = w.shape[1]
    row = lambda i: (i, 0)
    mod_map = lambda i: ((i // ntb) * 2 + jnp.minimum(i % ntb, 1), 0, 0)
    return pl.pallas_call(
        _in1_kernel,
        grid=(t // TM,),
        in_specs=[
            pl.BlockSpec((TM, d), row),
            pl.BlockSpec(memory_space=pl.ANY),
            pl.BlockSpec(memory_space=pl.ANY),
            pl.BlockSpec((TM, LANES), row),
            pl.BlockSpec((1, 6, d), mod_map),
            pl.BlockSpec((1, 6, d), mod_map),
            pl.BlockSpec((1, d), lambda i: (0, 0)),
            pl.BlockSpec((d, n), lambda i: (0, 0)),
        ],
        out_specs=[pl.BlockSpec((TM, d), row), pl.BlockSpec((TM, n // 2), row), pl.BlockSpec((TM, n // 2), row)],
        out_shape=[jax.ShapeDtypeStruct((t, d), F32), jax.ShapeDtypeStruct((t, n // 2), BF16),
                   jax.ShapeDtypeStruct((t, n // 2), F32)],
        scratch_shapes=_gather_scratch(d),
        compiler_params=_cparams(("arbitrary",)),
        name="l1_combine_in_proj",
    )(x, pos, ys, route, modv0, modv1, g, w)


def _rglru_kernel(u_ref, halo_ref, cw_ref, cb_ref, gaw_ref, gab_ref, gxw_ref, gxb_ref, lam_ref, o_ref,
                  ext, a_s, b_s, h_s, state, *, ntb):
    d = pl.program_id(0)
    s = pl.program_id(2)
    nb, tc, cw_ = u_ref.shape
    u = u_ref[...]
    cw = cw_ref[0]

    def finish(xc, reverse):
        xc2 = xc.reshape(nb * tc, cw_) + cb_ref[0]
        xb = xc2.astype(BF16)
        t_r = jnp.tanh(_dot(xb, gaw_ref[0, 0]) + gab_ref[0])
        t_i = jnp.tanh(_dot(xb, gxw_ref[0, 0]) + gxb_ref[0])
        nl = -lam_ref[0]
        softplus = jnp.maximum(nl, 0.0) + jnp.log1p(jnp.exp(-jnp.abs(nl)))
        half_rate = (-0.5 * LRU_C) * softplus
        log_a = half_rate * t_r + half_rate
        a = jnp.exp(log_a)
        half_x = 0.5 * xc2
        bb = jnp.sqrt(1.0 - a * a) * (half_x * t_i + half_x)
        n_lane = cw_ // LANES
        pitch = tc + SCAN_PAD
        for c in range(n_lane):
            for bi in range(nb):
                a_s[c, bi * pitch:bi * pitch + tc, :] = a[bi * tc:(bi + 1) * tc, c * LANES:(c + 1) * LANES]
                b_s[c, bi * pitch:bi * pitch + tc, :] = bb[bi * tc:(bi + 1) * tc, c * LANES:(c + 1) * LANES]

        @pl.when(s == 0)
        def _():
            state[...] = jnp.zeros_like(state)

        def steps(tb, hs):
            hs = list(hs)
            for uu in range(SCAN_UNROLL):
                tt = tb * SCAN_UNROLL + uu
                t = (tc - 1 - tt) if reverse else tt
                rows = pl.ds(t, nb, stride=pitch)
                for c in range(n_lane):
                    hs[c] = a_s[c, rows, :] * hs[c] + b_s[c, rows, :]
                    h_s[c, rows, :] = hs[c]
            return tuple(hs)

        hs = lax.fori_loop(0, tc // SCAN_UNROLL, steps, tuple(state[c] for c in range(n_lane)))
        for c in range(n_lane):
            state[c] = hs[c]
            for bi in range(nb):
                o_ref[0, bi, :, c * LANES:(c + 1) * LANES] = h_s[c, bi * pitch:bi * pitch + tc, :].astype(o_ref.dtype)

    @pl.when(d == 0)
    def _():
        chunk = s
        keep = chunk > 1
        ext[:, 0:SUBLANES, :] = jnp.where(keep, halo_ref[...], 0.0)
        ext[:, SUBLANES:, :] = u
        xc = cw[3:4] * u
        for j in range(1, C_CONV):
            xc = xc + cw[3 - j:4 - j] * ext[:, SUBLANES - j:SUBLANES - j + tc, :]
        finish(xc, False)

    @pl.when(d == 1)
    def _():
        chunk = jnp.where(s == 0, 0, ntb - s)
        keep = jnp.logical_and(chunk > 0, chunk < ntb - 1)
        ext[:, 0:tc, :] = u
        ext[:, tc:, :] = jnp.where(keep, halo_ref[...], 0.0)
        xc = cw[0:1] * u
        for j in range(1, C_CONV):
            xc = xc + cw[j:j + 1] * ext[:, j:j + tc, :]
        finish(xc, True)


def _rglru(u, conv_w, conv_b, ga_w, ga_b, gx_w, gx_b, lam, b, l):
    width = u.shape[1]
    cb = width // C_BLOCKS
    ntb = l // TM
    u3 = u.reshape(b, l, width)
    nblk = TM // SUBLANES

    def chunk_of(d, s):
        return jnp.where(d == 0, s, jnp.where(s == 0, 0, ntb - s))

    def halo_of(d, s):
        c = chunk_of(d, s)
        return jnp.where(d == 0, jnp.maximum(c * nblk - 1, 0), jnp.minimum((c + 1) * nblk, l // SUBLANES - 1))

    vec = pl.BlockSpec((1, 1, cb), lambda d, g, s: (d, 0, g))
    mat = pl.BlockSpec((1, 1, cb, cb), lambda d, g, s: (d, g, 0, 0))
    return pl.pallas_call(
        functools.partial(_rglru_kernel, ntb=ntb),
        grid=(2, C_BLOCKS, ntb),
        in_specs=[
            pl.BlockSpec((b, TM, cb), lambda d, g, s: (0, chunk_of(d, s), g)),
            pl.BlockSpec((b, SUBLANES, cb), lambda d, g, s: (0, halo_of(d, s), g)),
            pl.BlockSpec((1, C_CONV, cb), lambda d, g, s: (d, 0, g)),
            vec, mat, vec, mat, vec, vec,
        ],
        out_specs=pl.BlockSpec((1, b, TM, cb), lambda d, g, s: (d, 0, chunk_of(d, s), g)),
        out_shape=jax.ShapeDtypeStruct((2, b, l, width), BF16),
        scratch_shapes=[
            pltpu.VMEM((b, TM + SUBLANES, cb), F32),
            pltpu.VMEM((cb // LANES, b * (TM + SCAN_PAD), LANES), F32),
            pltpu.VMEM((cb // LANES, b * (TM + SCAN_PAD), LANES), F32),
            pltpu.VMEM((cb // LANES, b * (TM + SCAN_PAD), LANES), F32),
            pltpu.VMEM((cb // LANES, b, LANES), F32),
        ],
        compiler_params=_cparams(("arbitrary", "arbitrary", "arbitrary")),
        name="l1_rglru",
    )(u3, u3, conv_w, conv_b.reshape(2, 1, width), (0.5 * ga_w).astype(BF16), (0.5 * ga_b).reshape(2, 1, width),
      (0.5 * gx_w).astype(BF16), (0.5 * gx_b).reshape(2, 1, width), lam.reshape(2, 1, width))


def _final_kernel(x_ref, pos_ref, ys_ref, route_ref, mod_ref, g_ref, o_ref, ybuf, pbuf, psem, csem):
    slot = _gather_choices(pos_ref, ys_ref, ybuf, pbuf, psem, csem)
    x = x_ref[...] + mod_ref[0][5:6] * _combine(ybuf, slot, route_ref, x_ref.shape[1])
    ms = jnp.mean(x * x, axis=-1, keepdims=True)
    o_ref[...] = x * lax.rsqrt(ms + NORM_EPS) * g_ref[...]


def _final(x, pos, ys, route, modv, g, b, s_len):
    t, d = x.shape
    per_b = s_len // TM
    row = lambda i: (i, 0)
    return pl.pallas_call(
        _final_kernel,
        grid=(t // TM,),
        in_specs=[
            pl.BlockSpec((TM, d), row),
            pl.BlockSpec(memory_space=pl.ANY),
            pl.BlockSpec(memory_space=pl.ANY),
            pl.BlockSpec((TM, LANES), row),
            pl.BlockSpec((1, 6, d), lambda i: ((i // per_b) * 2 + 1, 0, 0)),
            pl.BlockSpec((1, d), lambda i: (0, 0)),
        ],
        out_specs=pl.BlockSpec((TM, d), row),
        out_shape=jax.ShapeDtypeStruct((t, d), F32),
        scratch_shapes=_gather_scratch(d),
        compiler_params=_cparams(("arbitrary",)),
        name="final_combine_norm",
    )(x, pos, ys, route, modv, g)


def _rope_tables(s_len, n_ctx):
    n_rows = s_len // GRID_W
    rows, cols = jnp.meshgrid(jnp.arange(n_rows), jnp.arange(GRID_W), indexing="ij")
    pos = jnp.stack([rows.reshape(-1), cols.reshape(-1)], axis=-1).astype(F32)
    n_freq = HEAD_DIM // 4
    inv = ROPE_THETA ** (-jnp.arange(n_freq, dtype=F32) / n_freq)
    ang = pos[:, :, None] * inv
    cos, sin = jnp.cos(ang), jnp.sin(ang)
    cos64 = jnp.stack([cos, cos], axis=2).reshape(s_len, HEAD_DIM)
    sin64 = jnp.stack([-sin, sin], axis=2).reshape(s_len, HEAD_DIM)
    cos_l = jnp.tile(cos64, (1, LANES // HEAD_DIM))
    sin_l = jnp.tile(sin64, (1, LANES // HEAD_DIM))
    cos_t = jnp.concatenate([jnp.ones((n_ctx, LANES), F32), cos_l], axis=0)
    sin_t = jnp.concatenate([jnp.zeros((n_ctx, LANES), F32), sin_l], axis=0)
    return cos_t, sin_t


def kernel(x, c, ctx, c_ctx, w_mod, b_mod, norm_mix, norm_ffn, ev_w_in, ev_w_out, ev_lambda_q1, ev_lambda_k1, ev_lambda_q2, ev_lambda_k2, ev_subln, ev_conv_w, od_w_in, od_w_out, od_conv_w, od_conv_b, od_gate_a_w, od_gate_a_b, od_gate_x_w, od_gate_x_b, od_lru_lambda, moe_w_router, moe_b_router, moe_w1, moe_b1, moe_w2, moe_b2, final_norm):
    b, s_len, d = x.shape
    n_ctx = ctx.shape[1]
    l = n_ctx + s_len
    t = b * l
    ntb = l // TM
    assert n_ctx == TM and s_len % TM == 0 and w_mod.shape[0] == 2

    xs = jnp.concatenate([ctx, x], axis=1).reshape(t, d)

    n_rows = -(-(b + 1) // SUBLANES) * SUBLANES
    cs = jnp.concatenate([c, c_ctx[None, :], jnp.zeros((n_rows - b - 1, d), F32)], axis=0)
    mod = _modulation(cs, w_mod, b_mod)

    def mod_table(i):
        lat = mod[i, :b]
        cx = jnp.broadcast_to(mod[i, b][None, :], lat.shape)
        return jnp.stack([cx, lat], axis=1).reshape(b * 2, 6, d)

    modv0, modv1 = mod_table(0), mod_table(1)

    cos_t, sin_t = _rope_tables(s_len, n_ctx)
    q, k, v, bg, p = _in_proj0(xs, modv0, norm_mix[0:1], ev_w_in[0].astype(BF16), cos_t, sin_t, ntb)
    lam_init = 0.8 - 0.6 * math.exp(-0.3 * 0)
    attn = _diff_attention(q, k, v, ev_lambda_q1[0:1], ev_lambda_k1[0:1], ev_lambda_q2[0:1], ev_lambda_k2[0:1],
                           ev_subln[0:1], lam_init, b, l, n_ctx)
    x1, h2, route0, counts0 = _out_proj0(attn, bg, p, ev_conv_w[0], ev_w_out[0].astype(BF16), xs, modv0,
                                         norm_ffn[0:1], moe_w_router[0], moe_b_router[0:1], ntb)
    pos0, ys0 = _moe_layer(route0, counts0, h2, moe_w1, moe_b1, moe_w2, moe_b2, 0)

    x2, gg, u = _in_proj1(x1, pos0, ys0, route0, modv0, modv1, norm_mix[1:2], od_w_in[0].astype(BF16), ntb)
    hs = _rglru(u, od_conv_w[0], od_conv_b[0], od_gate_a_w[0], od_gate_a_b[0], od_gate_x_w[0], od_gate_x_b[0],
                od_lru_lambda[0], b, l)
    x3, h3, route1, counts1 = _out_proj1(hs.reshape(2, t, hs.shape[-1]), gg, od_w_out[0].astype(BF16), x2, modv1,
                                         norm_ffn[1:2], moe_w_router[1], moe_b_router[1:2], ntb)
    pos1, ys1 = _moe_layer(route1, counts1, h3, moe_w1, moe_b1, moe_w2, moe_b2, 1)

    out = _final(x3, pos1, ys1, route1, modv1, final_norm[None, :], b, s_len)
    return out.reshape(b, s_len, d)
```

```python
import functools
import math

import jax
import jax.numpy as jnp
from jax import lax
from jax.experimental import pallas as pl
from jax.experimental.pallas import tpu as pltpu

F32 = jnp.float32
BF16 = jnp.bfloat16
I32 = jnp.int32

NORM_EPS = 1e-6
ROPE_THETA = 10000.0
GRID_W = 64
N_HEADS = 4
HEAD_DIM = 64
A_WIDTH = 2 * N_HEADS * HEAD_DIM
B_CONV = 3
C_CONV = 4
C_BLOCKS = 4
LRU_C = 8.0
TOP_K = 4
SWIGLU_ALPHA = 1.702
SWIGLU_LIMIT = 7.0

LANES = 128
SUBLANES = 8
TM = 256
TME = 256
VMEM_LIMIT = 56 * 1024 * 1024
SCAN_PAD = 8
SCAN_UNROLL = 8
LOG2_E = 1.4426950408889634


def _cparams(sem, vmem=VMEM_LIMIT):
    return pltpu.CompilerParams(dimension_semantics=sem, vmem_limit_bytes=vmem)


def _norm_mod(x, g, shift, scale):
    ms = jnp.mean(x * x, axis=-1, keepdims=True)
    return (x * lax.rsqrt(ms + NORM_EPS) * g) * (1.0 + scale) + shift


def _dot(a, b):
    return jnp.dot(a, b, preferred_element_type=F32)


def _sigmoid(x):
    return 0.5 * jnp.tanh(0.5 * x) + 0.5


def _mod_kernel(cs_ref, w_ref, b_ref, o_ref):
    s = cs_ref[...]
    s = s * jax.nn.sigmoid(s)
    o_ref[0] = _dot(s.astype(BF16), w_ref[0].astype(BF16)) + b_ref[0]


def _modulation(cs, w_mod, b_mod):
    depth, d, n = w_mod.shape
    rows = cs.shape[0]
    tn = 1536
    return pl.pallas_call(
        _mod_kernel,
        grid=(depth, n // tn),
        in_specs=[
            pl.BlockSpec((rows, d), lambda i, j: (0, 0)),
            pl.BlockSpec((1, d, tn), lambda i, j: (i, 0, j)),
            pl.BlockSpec((1, 1, tn), lambda i, j: (i, 0, j)),
        ],
        out_specs=pl.BlockSpec((1, rows, tn), lambda i, j: (i, 0, j)),
        out_shape=jax.ShapeDtypeStruct((depth, rows, n), F32),
        compiler_params=_cparams(("parallel", "parallel")),
        name="adaln_modulation",
    )(cs, w_mod, b_mod.reshape(depth, 1, n))


def _stream_tile(ctx_ref, x_ref, ntb):
    return jnp.where(pl.program_id(0) % ntb == 0, ctx_ref[...], x_ref[...])


def _stream_specs(d, ntb):
    return [pl.BlockSpec((TM, d), lambda i: (i // ntb, 0)),
            pl.BlockSpec((TM, d), lambda i: ((i // ntb) * (ntb - 1) + jnp.maximum(i % ntb - 1, 0), 0))]


def _in0_kernel(ctx_ref, x_ref, mod_ref, g_ref, w_ref, cos_ref, sin_ref,
                q_ref, k_ref, v_ref, bg_ref, p_ref, *, ntb):
    mod = mod_ref[0]
    h = _norm_mod(_stream_tile(ctx_ref, x_ref, ntb), g_ref[...], mod[0:1], mod[1:2])
    y = _dot(h.astype(BF16), w_ref[...])
    cosv = cos_ref[...]
    sinv = sin_ref[...]
    lane = lax.broadcasted_iota(I32, (TM, LANES), 1)
    first_half = (lane & 16) == 0

    def rope(z):
        outs = []
        for g in range(A_WIDTH // LANES):
            zg = z[:, g * LANES:(g + 1) * LANES]
            partner = jnp.where(first_half, pltpu.roll(zg, LANES - 16, 1), pltpu.roll(zg, 16, 1))
            outs.append(zg * cosv + partner * sinv)
        return jnp.concatenate(outs, axis=1)

    aw = A_WIDTH
    q_ref[...] = (rope(y[:, :aw]) * (HEAD_DIM ** -0.5 * LOG2_E)).astype(BF16)
    k_ref[...] = rope(y[:, aw:2 * aw]).astype(BF16)
    v_ref[...] = y[:, 2 * aw:3 * aw].astype(BF16)
    bw = (y.shape[1] - 3 * aw) // 3
    bg_ref[...] = y[:, 3 * aw:3 * aw + bw].astype(BF16)
    p_ref[...] = (y[:, 3 * aw + bw:3 * aw + 2 * bw] * y[:, 3 * aw + 2 * bw:]).astype(BF16)


def _in_proj0(ctx, x, modv, g, w, cos_t, sin_t, ntb):
    d = x.shape[1]
    t = ctx.shape[0] + x.shape[0]
    n = w.shape[1]
    bw = (n - 3 * A_WIDTH) // 3
    row = lambda i: (i, 0)
    return pl.pallas_call(
        functools.partial(_in0_kernel, ntb=ntb),
        grid=(t // TM,),
        in_specs=_stream_specs(d, ntb) + [
            pl.BlockSpec((1, 6, d), lambda i: ((i // ntb) * 2 + jnp.minimum(i % ntb, 1), 0, 0)),
            pl.BlockSpec((1, d), lambda i: (0, 0)),
            pl.BlockSpec((d, n), lambda i: (0, 0)),
            pl.BlockSpec((TM, LANES), lambda i: (i % ntb, 0)),
            pl.BlockSpec((TM, LANES), lambda i: (i % ntb, 0)),
        ],
        out_specs=[pl.BlockSpec((TM, A_WIDTH), row)] * 3 + [pl.BlockSpec((TM, bw), row)] * 2,
        out_shape=[jax.ShapeDtypeStruct((t, A_WIDTH), BF16)] * 3 + [jax.ShapeDtypeStruct((t, bw), BF16)] * 2,
        compiler_params=_cparams(("parallel",)),
        name="l0_in_proj_rope",
    )(ctx, x, modv, g, w, cos_t, sin_t)


def _attn_kernel(lq1_ref, lk1_ref, lq2_ref, lk2_ref, g_ref, q_ref, k_ref, v_ref, o_ref, *, lam_init, n_ctx):
    qi = pl.program_id(2)
    lam = (jnp.exp(jnp.sum(lq1_ref[...] * lk1_ref[...], axis=-1, keepdims=True))
           - jnp.exp(jnp.sum(lq2_ref[...] * lk2_ref[...], axis=-1, keepdims=True)) + lam_init)
    q = q_ref[...]
    lane = lax.broadcasted_iota(I32, q.shape, 1)
    zero = jnp.zeros_like(q)
    q1 = jnp.where(lane < HEAD_DIM, q, zero)
    q2 = jnp.where(lane < HEAD_DIM, zero, q)
    contract_last = (((1,), (1,)), ((), ()))

    def attend(nk):
        k = k_ref[0, :nk, :]
        v = v_ref[0, :nk, :]
        s1 = lax.dot_general(q1, k, contract_last, preferred_element_type=F32)
        s2 = lax.dot_general(q2, k, contract_last, preferred_element_type=F32)
        p1 = jnp.exp2(s1 - jnp.max(s1, axis=-1, keepdims=True))
        p2 = jnp.exp2(s2 - jnp.max(s2, axis=-1, keepdims=True))
        r1 = 1.0 / jnp.sum(p1, axis=-1, keepdims=True)
        r2 = lam / jnp.sum(p2, axis=-1, keepdims=True)
        o = _dot(p1.astype(BF16), v) * r1 - _dot(p2.astype(BF16), v) * r2
        ms = jnp.mean(o * o, axis=-1, keepdims=True)
        o = o * lax.rsqrt(ms + NORM_EPS) * g_ref[...] * (1.0 - lam_init)
        o_ref[...] = o.astype(BF16)

    @pl.when(qi == 0)
    def _():
        attend(n_ctx)

    @pl.when(qi > 0)
    def _():
        attend(k_ref.shape[1])


def _diff_attention(q, k, v, lq1, lk1, lq2, lk2, subln, lam_init, b, l, n_ctx):
    t = q.shape[0]
    ntb = l // TM
    hw = 2 * HEAD_DIM
    k3 = k.reshape(b, l, A_WIDTH)
    v3 = v.reshape(b, l, A_WIDTH)
    vec = lambda n: pl.BlockSpec((1, n), lambda bi, h, qi: (0, 0))
    qspec = pl.BlockSpec((TM, hw), lambda bi, h, qi: (bi * ntb + qi, h))
    kspec = pl.BlockSpec((1, l, hw), lambda bi, h, qi: (bi, 0, h))
    return pl.pallas_call(
        functools.partial(_attn_kernel, lam_init=lam_init, n_ctx=n_ctx),
        grid=(b, N_HEADS, ntb),
        in_specs=[vec(HEAD_DIM)] * 4 + [vec(hw), qspec, kspec, kspec],
        out_specs=qspec,
        out_shape=jax.ShapeDtypeStruct((t, A_WIDTH), BF16),
        compiler_params=_cparams(("parallel", "parallel", "parallel")),
        name="l0_diff_attention",
    )(lq1, lk1, lq2, lk2, subln, q, k3, v3)


def _route_tail(y, x_in, mod_ref, g_ref, wr_ref, br_ref,
                x1_ref, h2_ref, route_ref, cnt_ref, carry_ref):
    mod = mod_ref[0]
    x1 = x_in + mod[2:3] * y
    x1_ref[...] = x1
    h2 = _norm_mod(x1, g_ref[...], mod[3:4], mod[4:5])
    nch = h2.shape[1] // LANES
    for s in range(nch):
        h2_ref[pl.ds(s, TM, stride=nch), :] = h2[:, s * LANES:(s + 1) * LANES]

    wr = wr_ref[...]
    h_hi = h2.astype(BF16)
    h_lo = (h2 - h_hi.astype(F32)).astype(BF16)
    w_hi = wr.astype(BF16)
    w_lo = (wr - w_hi.astype(F32)).astype(BF16)
    logits = _dot(h_hi, w_hi) + _dot(h_hi, w_lo) + _dot(h_lo, w_hi) + br_ref[...]

    n_exp = logits.shape[1]
    lane = lax.broadcasted_iota(I32, logits.shape, 1).astype(F32)
    work = logits
    sels, vals, idxs = [], [], []
    for _ in range(TOP_K):
        m = jnp.max(work, axis=-1, keepdims=True)
        idx = jnp.min(jnp.where(work == m, lane, float(n_exp)), axis=-1, keepdims=True)
        sel = lane == idx
        sels.append(sel)
        vals.append(m)
        idxs.append(idx)
        work = jnp.where(sel, -jnp.inf, work)
    exps = [jnp.exp(vv - vals[0]) for vv in vals]
    inv_den = 1.0 / (exps[0] + exps[1] + exps[2] + exps[3])

    chosen = jnp.zeros(logits.shape, F32)
    for sel in sels:
        chosen = chosen + jnp.where(sel, 1.0, 0.0)
    r_i = lax.broadcasted_iota(I32, (TM, TM), 0)
    c_i = lax.broadcasted_iota(I32, (TM, TM), 1)
    earlier = jnp.where(c_i < r_i, 1.0, 0.0).astype(BF16)
    rank = _dot(earlier, chosen.astype(BF16)) + carry_ref[...]
    carry_ref[...] = carry_ref[...] + jnp.sum(chosen, axis=0, keepdims=True)
    cnt_ref[...] = carry_ref[...]

    out_lane = lax.broadcasted_iota(I32, (TM, LANES), 1)
    packed = jnp.zeros((TM, LANES), F32)
    for kk in range(TOP_K):
        rank_k = jnp.sum(jnp.where(sels[kk], rank, 0.0), axis=-1, keepdims=True)
        packed = jnp.where(out_lane == kk, idxs[kk], packed)
        packed = jnp.where(out_lane == TOP_K + kk, rank_k, packed)
        packed = jnp.where(out_lane == 2 * TOP_K + kk, exps[kk] * inv_den, packed)
    route_ref[...] = packed


def _out0_kernel(attn_ref, bg_ref, p_ref, pprev_ref, pnext_ref, cw_ref, wo_ref,
                 ctx_ref, x_ref, mod_ref, g_ref, wr_ref, br_ref,
                 x1_ref, h2_ref, route_ref, cnt_ref, carry_ref, *, ntb):
    i = pl.program_id(0)
    seg = i % ntb

    @pl.when(i == 0)
    def _():
        carry_ref[...] = jnp.zeros_like(carry_ref)

    p = p_ref[...].astype(F32)
    row = lax.broadcasted_iota(I32, p.shape, 0)
    has_prev = seg > 1
    has_next = jnp.logical_and(seg > 0, seg < ntb - 1)
    prev_row = jnp.where(has_prev, pprev_ref[SUBLANES - 1:SUBLANES, :].astype(F32), 0.0)
    next_row = jnp.where(has_next, pnext_ref[0:1, :].astype(F32), 0.0)
    before = jnp.where(row == 0, prev_row, pltpu.roll(p, 1, 0))
    after = jnp.where(row == TM - 1, next_row, pltpu.roll(p, TM - 1, 0))
    cw = cw_ref[...]
    conv = bg_ref[...].astype(F32) * (cw[0:1] * before + cw[1:2] * p + cw[2:3] * after)
    aw = attn_ref.shape[1]
    y = _dot(attn_ref[...], wo_ref[:aw, :]) + _dot(conv.astype(BF16), wo_ref[aw:, :])
    _route_tail(y, _stream_tile(ctx_ref, x_ref, ntb), mod_ref, g_ref, wr_ref, br_ref,
                x1_ref, h2_ref, route_ref, cnt_ref, carry_ref)


def _out1_kernel(hs_ref, gg_ref, wo_ref, x_ref, mod_ref, g_ref, wr_ref, br_ref,
                 x1_ref, h2_ref, route_ref, cnt_ref, carry_ref, *, ntb):
    i = pl.program_id(0)

    @pl.when(i == 0)
    def _():
        carry_ref[...] = jnp.zeros_like(carry_ref)

    @pl.when(i % ntb > 0)
    def _():
        rec = hs_ref[0].astype(F32) + hs_ref[1].astype(F32)
        y = _dot((rec * gg_ref[...].astype(F32)).astype(BF16), wo_ref[...])
        _route_tail(y, x_ref[...], mod_ref, g_ref, wr_ref, br_ref, x1_ref, h2_ref, route_ref, cnt_ref, carry_ref)


def _tail_specs(d, n_exp, ntb, latent_only):
    row = lambda i: (i, 0)
    const = lambda i: (0, 0)
    out_row = (lambda i: ((i // ntb) * (ntb - 1) + jnp.maximum(i % ntb - 1, 0), 0)) if latent_only else row
    in_specs = [
        pl.BlockSpec((TM, d), row),
        pl.BlockSpec((1, 6, d), lambda i: ((i // ntb) * 2 + jnp.minimum(i % ntb, 1), 0, 0)),
        pl.BlockSpec((1, d), const),
        pl.BlockSpec((d, n_exp), const),
        pl.BlockSpec((1, n_exp), const),
    ]
    out_specs = [
        pl.BlockSpec((TM, d), out_row),
        pl.BlockSpec((TM * (d // LANES), LANES), out_row),
        pl.BlockSpec((TM, LANES), out_row),
        pl.BlockSpec((1, n_exp), const),
    ]
    return in_specs, out_specs


def _tail_shapes(t, d, n_exp):
    return [jax.ShapeDtypeStruct((t, d), F32), jax.ShapeDtypeStruct((t * (d // LANES), LANES), F32),
            jax.ShapeDtypeStruct((t, LANES), F32), jax.ShapeDtypeStruct((1, n_exp), F32)]


def _out_proj0(attn, bg, p, conv_w, w_out, ctx, x, modv, g, w_r, b_r, ntb):
    d = x.shape[1]
    t = ctx.shape[0] + x.shape[0]
    n_exp = w_r.shape[1]
    bw = bg.shape[1]
    row = lambda i: (i, 0)
    const = lambda i: (0, 0)
    nblk = TM // SUBLANES
    tail_in, tail_out = _tail_specs(d, n_exp, ntb, False)
    return pl.pallas_call(
        functools.partial(_out0_kernel, ntb=ntb),
        grid=(t // TM,),
        in_specs=[
            pl.BlockSpec((TM, attn.shape[1]), row),
            pl.BlockSpec((TM, bw), row),
            pl.BlockSpec((TM, bw), row),
            pl.BlockSpec((SUBLANES, bw), lambda i: (jnp.maximum(i * nblk - 1, 0), 0)),
            pl.BlockSpec((SUBLANES, bw), lambda i: (jnp.minimum((i + 1) * nblk, t // SUBLANES - 1), 0)),
            pl.BlockSpec(conv_w.shape, const),
            pl.BlockSpec(w_out.shape, const),
        ] + _stream_specs(d, ntb) + tail_in[1:],
        out_specs=tail_out,
        out_shape=_tail_shapes(t, d, n_exp),
        scratch_shapes=[pltpu.VMEM((1, n_exp), F32)],
        compiler_params=_cparams(("arbitrary",)),
        name="l0_out_proj_router",
    )(attn, bg, p, p, p, conv_w, w_out, ctx, x, modv, g, w_r, b_r)


def _out_proj1(hs, gg, w_out, x, modv, g, w_r, b_r, ntb):
    t, d = x.shape
    n_exp = w_r.shape[1]
    row = lambda i: (i, 0)
    const = lambda i: (0, 0)
    tail_in, tail_out = _tail_specs(d, n_exp, ntb, True)
    t_lat = t // ntb * (ntb - 1)
    return pl.pallas_call(
        functools.partial(_out1_kernel, ntb=ntb),
        grid=(t // TM,),
        in_specs=[
            pl.BlockSpec((2, TM, hs.shape[2]), lambda i: (0, i, 0)),
            pl.BlockSpec((TM, gg.shape[1]), row),
            pl.BlockSpec(w_out.shape, const),
        ] + tail_in,
        out_specs=tail_out,
        out_shape=_tail_shapes(t_lat, d, n_exp),
        scratch_shapes=[pltpu.VMEM((1, n_exp), F32)],
        compiler_params=_cparams(("arbitrary",)),
        name="l1_out_proj_router",
    )(hs, gg, w_out, x, modv, g, w_r, b_r)


PAIRS_PER_TILE = TM * TOP_K


def _stage_pos(pos_ref, pbuf, psem):
    i = pl.program_id(0)
    slot = i % 2

    def chunk(ti, s):
        return pltpu.make_async_copy(pos_ref.at[pl.ds(ti * PAIRS_PER_TILE, PAIRS_PER_TILE)],
                                     pbuf.at[pl.ds(s * PAIRS_PER_TILE, PAIRS_PER_TILE)], psem.at[s])

    @pl.when(i == 0)
    def _():
        chunk(0, 0).start()

    chunk(i, slot).wait()

    @pl.when(i + 1 < pl.num_programs(0))
    def _():
        chunk(i + 1, 1 - slot).start()

    return slot * PAIRS_PER_TILE


def _dispatch_kernel(pad_lo_ref, pad_hi_ref, pos_ref, h2_ref, xs_ref, pbuf, zbuf, psem, dsem, zsem):
    i = pl.program_id(0)
    nch = h2_ref.shape[0] // TM
    n_tails = pad_lo_ref.shape[0] - 1

    def zero_row(j):
        return pltpu.make_async_copy(zbuf.at[pl.ds(0, nch)], xs_ref.at[pl.ds(pl.multiple_of(j * nch, nch), nch)], zsem)

    def zero_tile(j):
        return pltpu.make_async_copy(zbuf, xs_ref.at[pl.ds(pl.multiple_of(j * TME * nch, TME * nch), TME * nch)], zsem)

    def over_padding(act):
        def tail(e, carry):
            def body(j, c):
                act(zero_row(j))
                return c
            return lax.fori_loop(pad_lo_ref[e], pad_hi_ref[e], body, carry)

        lax.fori_loop(0, n_tails, tail, 0)

        def tile(j, c):
            act(zero_tile(j))
            return c

        lax.fori_loop(pad_lo_ref[n_tails] // TME, pad_hi_ref[n_tails] // TME, tile, 0)

    @pl.when(i == 0)
    def _():
        zbuf[...] = jnp.zeros(zbuf.shape, zbuf.dtype)
        over_padding(lambda cp: cp.start())
        over_padding(lambda cp: cp.wait())

    off = _stage_pos(pos_ref, pbuf, psem)
    for r in range(TM):
        for kk in range(TOP_K):
            dst = pl.multiple_of(pbuf[off + r * TOP_K + kk] * nch, nch)
            pltpu.make_async_copy(h2_ref.at[pl.ds(r * nch, nch)], xs_ref.at[pl.ds(dst, nch)], dsem).start(
                priority=kk % 2)
    for _ in range(TOP_K):
        pltpu.make_async_copy(h2_ref, xs_ref.at[pl.ds(0, TM * nch)], dsem).wait()


def _dispatch(pos, pad_lo, pad_hi, h2, n_slots, d):
    nch = d // LANES
    n_tok = h2.shape[0] // nch
    grid_spec = pltpu.PrefetchScalarGridSpec(
        num_scalar_prefetch=2,
        grid=(n_tok // TM,),
        in_specs=[pl.BlockSpec(memory_space=pl.ANY),
                  pl.BlockSpec((TM * nch, LANES), lambda i, lo, hi: (i, 0))],
        out_specs=pl.BlockSpec(memory_space=pl.ANY),
        scratch_shapes=[pltpu.SMEM((2 * PAIRS_PER_TILE,), I32), pltpu.VMEM((TME * nch, LANES), F32),
                        pltpu.SemaphoreType.DMA((2,)), pltpu.SemaphoreType.DMA, pltpu.SemaphoreType.DMA],
    )
    return pl.pallas_call(
        _dispatch_kernel,
        grid_spec=grid_spec,
        out_shape=jax.ShapeDtypeStruct((n_slots * nch, LANES), F32),
        compiler_params=_cparams(("arbitrary",)),
        name="expert_dispatch",
    )(pad_lo, pad_hi, pos, h2)


def _moe_kernel(te_ref, nx_ref, par_ref, meta_ref, x_ref, w1_hbm, b1_ref, w2_hbm, b2_ref, y_ref,
                w1f, w2f, w1b, w2b, wsem, *, layer):
    i = pl.program_id(0)
    n_tiles = meta_ref[0]
    d = w1b.shape[0]
    ff = w2b.shape[0]
    nch = d // LANES

    def weight_copies(e, s):
        return (pltpu.make_async_copy(w1_hbm.at[layer, e], w1f.at[s], wsem.at[0, s]),
                pltpu.make_async_copy(w2_hbm.at[layer, e], w2f.at[s], wsem.at[1, s]))

    @pl.when(i == 0)
    def _():
        for cp in weight_copies(te_ref[0], par_ref[0]):
            cp.start()

    @pl.when(i < n_tiles)
    def _():
        new_expert = jnp.logical_or(i == 0, te_ref[i] != te_ref[jnp.maximum(i - 1, 0)])

        @pl.when(new_expert)
        def _():
            s = par_ref[i]
            for cp in weight_copies(te_ref[i], s):
                cp.wait()
            w1b[...] = w1f[s].astype(BF16)
            w2b[...] = w2f[s].astype(BF16)

            @pl.when(nx_ref[i] >= 0)
            def _():
                for cp in weight_copies(nx_ref[i], 1 - s):
                    cp.start()

        x = jnp.concatenate([x_ref[pl.ds(c, TME, stride=nch), :] for c in range(nch)], axis=1)
        h = _dot(x.astype(BF16), w1b[...]) + b1_ref[0]
        glu = jnp.minimum(h[:, :ff], SWIGLU_LIMIT)
        lin = jnp.clip(h[:, ff:], -SWIGLU_LIMIT, SWIGLU_LIMIT)
        act = glu * _sigmoid(SWIGLU_ALPHA * glu) * (lin + 1.0)
        y = _dot(act.astype(BF16), w2b[...]) + b2_ref[0]
        for c in range(nch):
            y_ref[pl.ds(c, TME, stride=nch), :] = y[:, c * LANES:(c + 1) * LANES]

    @pl.when(i >= n_tiles)
    def _():
        y_ref[...] = jnp.zeros(y_ref.shape, y_ref.dtype)


def _routed_experts(tile_expert, next_expert, parity, meta, xs, w1, b1, w2, b2, layer):
    _, n_exp, d, ff2 = w1.shape
    ff = w2.shape[2]
    nch = d // LANES
    n_grid = tile_expert.shape[0]
    used = lambda i, mt: jnp.minimum(i, mt[0] - 1)
    bmap = lambda i, te, nx, pr, mt: (layer * n_exp + te[used(i, mt)], 0, 0)
    grid_spec = pltpu.PrefetchScalarGridSpec(
        num_scalar_prefetch=4,
        grid=(n_grid,),
        in_specs=[
            pl.BlockSpec((TME * nch, LANES), lambda i, te, nx, pr, mt: (used(i, mt), 0)),
            pl.BlockSpec(memory_space=pl.ANY),
            pl.BlockSpec((1, 1, ff2), bmap),
            pl.BlockSpec(memory_space=pl.ANY),
            pl.BlockSpec((1, 1, d), bmap),
        ],
        out_specs=pl.BlockSpec((TME * nch, LANES), lambda i, te, nx, pr, mt: (i, 0)),
        scratch_shapes=[pltpu.VMEM((2, d, ff2), F32), pltpu.VMEM((2, ff, d), F32),
                        pltpu.VMEM((d, ff2), BF16), pltpu.VMEM((ff, d), BF16), pltpu.SemaphoreType.DMA((2, 2))],
    )
    return pl.pallas_call(
        functools.partial(_moe_kernel, layer=layer),
        grid_spec=grid_spec,
        out_shape=jax.ShapeDtypeStruct(xs.shape, F32),
        compiler_params=_cparams(("arbitrary",)),
        name="routed_experts",
    )(tile_expert, next_expert, parity, meta, xs, w1, b1.reshape(-1, 1, ff2), w2, b2.reshape(-1, 1, d))


def _gather_choices(pos_ref, ys_ref, ybuf, pbuf, psem, csem):
    i = pl.program_id(0)
    n = pl.num_programs(0)
    slot = i % 2
    nch = ybuf.shape[2] // TM

    def chunk(ti, s):
        return pltpu.make_async_copy(pos_ref.at[pl.ds(ti * PAIRS_PER_TILE, PAIRS_PER_TILE)],
                                     pbuf.at[pl.ds(s * PAIRS_PER_TILE, PAIRS_PER_TILE)], psem.at[s])

    def row_copy(s, r, kk):
        src = pl.multiple_of(pbuf[s * PAIRS_PER_TILE + r * TOP_K + kk] * nch, nch)
        return pltpu.make_async_copy(ys_ref.at[pl.ds(src, nch)], ybuf.at[s, kk, pl.ds(r * nch, nch)], csem.at[s])

    @pl.when(i == 0)
    def _():
        chunk(0, 0).start()
        chunk(0, 0).wait()

        def first_rows(r, c):
            for kk in range(TOP_K):
                row_copy(0, r, kk).start(priority=kk % 2)
            return c

        lax.fori_loop(0, TM, first_rows, 0)

        @pl.when(n > 1)
        def _():
            chunk(1, 1).start()

    @pl.when(i + 1 < n)
    def _():
        chunk(i + 1, 1 - slot).wait()
        for r in range(TM):
            for kk in range(TOP_K):
                row_copy(1 - slot, r, kk).start(priority=kk % 2)

        @pl.when(i + 2 < n)
        def _():
            chunk(i + 2, slot).start()

    for kk in range(TOP_K):
        pltpu.make_async_copy(ys_ref.at[pl.ds(0, TM * nch)], ybuf.at[slot, kk], csem.at[slot]).wait()
    return slot


def _gather_scratch(d):
    nch = d // LANES
    return [pltpu.VMEM((2, TOP_K, TM * nch, LANES), F32), pltpu.SMEM((2 * PAIRS_PER_TILE,), I32),
            pltpu.SemaphoreType.DMA((2,)), pltpu.SemaphoreType.DMA((2,))]


def _plan_routes(route, counts, n_grid):
    n_exp = counts.shape[-1]
    cnt = counts.reshape(n_exp).astype(I32)
    tiles_per = (cnt + TME - 1) // TME
    tile_end = jnp.cumsum(tiles_per)
    offset = (tile_end - tiles_per) * TME
    n_tiles = tile_end[-1]
    tile_ids = jnp.minimum(jnp.arange(n_grid, dtype=I32), n_tiles - 1)
    tile_expert = jnp.sum((tile_ids[:, None] >= tile_end[None, :]).astype(I32), axis=1)
    group_end = jnp.take(tile_end, tile_expert)
    next_expert = jnp.where(group_end < n_tiles, jnp.take(tile_expert, jnp.minimum(group_end, n_grid - 1)), -1)
    parity = jnp.take(jnp.cumsum((tiles_per > 0).astype(I32)), tile_expert) % 2
    eidx = route[:, :TOP_K].astype(I32)
    rank = route[:, TOP_K:2 * TOP_K].astype(I32)
    expert_ids = jnp.arange(n_exp, dtype=I32)
    pos = jnp.sum(jnp.where(eidx[..., None] == expert_ids, offset, 0), axis=-1) + rank
    pad_lo = jnp.concatenate([offset + cnt, (n_tiles * TME).reshape(1)])
    pad_hi = jnp.concatenate([tile_end * TME, jnp.full((1,), n_grid * TME, I32)])
    return (tile_expert, next_expert, parity, n_tiles.reshape(1)), pos.reshape(-1), pad_lo, pad_hi


def _moe_layer(route, counts, h2, w1, b1, w2, b2, layer):
    n_exp, d = w1.shape[1], w1.shape[2]
    n_tok = route.shape[0]
    n_grid = n_tok * TOP_K // TME + n_exp
    tiles, pos, pad_lo, pad_hi = _plan_routes(route, counts, n_grid)
    xs = _dispatch(pos, pad_lo, pad_hi, h2, n_grid * TME, d)
    return pos, _routed_experts(*tiles, xs, w1, b1, w2, b2, layer)


def _combine(ybuf, slot, route_ref, d):
    nch = d // LANES
    gates = route_ref[...]
    gk = [jnp.broadcast_to(gates[:, 2 * TOP_K + kk:2 * TOP_K + kk + 1], (TM, LANES)) for kk in range(TOP_K)]
    chunks = []
    for c in range(nch):
        acc = gk[0] * ybuf[slot, 0, pl.ds(c, TM, stride=nch), :]
        for kk in range(1, TOP_K):
            acc = acc + gk[kk] * ybuf[slot, kk, pl.ds(c, TM, stride=nch), :]
        chunks.append(acc)
    return jnp.concatenate(chunks, axis=1)


def _in1_kernel(x_ref, pos_ref, ys_ref, route_ref, mod0_ref, mod1_ref, g_ref, w_ref, x2_ref, gg_ref, u_ref,
                ybuf, pbuf, psem, csem):
    slot = _gather_choices(pos_ref, ys_ref, ybuf, pbuf, psem, csem)
    x2 = x_ref[...] + mod0_ref[0][5:6] * _combine(ybuf, slot, route_ref, x_ref.shape[1])
    x2_ref[...] = x2
    mod1 = mod1_ref[0]
    h = _norm_mod(x2, g_ref[...], mod1[0:1], mod1[1:2])
    y = _dot(h.astype(BF16), w_ref[...])
    half = y.shape[1] // 2
    gg_ref[...] = jax.nn.gelu(y[:, :half]).astype(BF16)
    u_ref[...] = y[:, half:]


def _in_proj1(x, pos, ys, route, modv0, modv1, g, w, ntb):
    t, d = x.shape
    n = w.shape[1]
    row = lambda i: (i, 0)
    mod_map = lambda i: ((i // ntb) * 2 + jnp.minimum(i % ntb, 1), 0, 0)
    return pl.pallas_call(
        _in1_kernel,
        grid=(t // TM,),
        in_specs=[
            pl.BlockSpec((TM, d), row),
            pl.BlockSpec(memory_space=pl.ANY),
            pl.BlockSpec(memory_space=pl.ANY),
            pl.BlockSpec((TM, LANES), row),
            pl.BlockSpec((1, 6, d), mod_map),
            pl.BlockSpec((1, 6, d), mod_map),
            pl.BlockSpec((1, d), lambda i: (0, 0)),
            pl.BlockSpec((d, n), lambda i: (0, 0)),
        ],
        out_specs=[pl.BlockSpec((TM, d), row), pl.BlockSpec((TM, n // 2), row), pl.BlockSpec((TM, n // 2), row)],
        out_shape=[jax.ShapeDtypeStruct((t, d), F32), jax.ShapeDtypeStruct((t, n // 2), BF16),
                   jax.ShapeDtypeStruct((t, n // 2), F32)],
        scratch_shapes=_gather_scratch(d),
        compiler_params=_cparams(("arbitrary",)),
        name="l1_combine_in_proj",
    )(x, pos, ys, route, modv0, modv1, g, w)


def _rglru_kernel(u_ref, halo_ref, cw_ref, cb_ref, gaw_ref, gab_ref, gxw_ref, gxb_ref, lam_ref, o_ref,
                  ext, a_s, b_s, h_s, state, *, ntb):
    d = pl.program_id(0)
    s = pl.program_id(2)
    nb, tc, cw_ = u_ref.shape
    u = u_ref[...]
    cw = cw_ref[0]

    def finish(xc, reverse):
        xc2 = xc.reshape(nb * tc, cw_) + cb_ref[0]
        xb = xc2.astype(BF16)
        t_r = jnp.tanh(_dot(xb, gaw_ref[0, 0]) + gab_ref[0])
        t_i = jnp.tanh(_dot(xb, gxw_ref[0, 0]) + gxb_ref[0])
        nl = -lam_ref[0]
        softplus = jnp.maximum(nl, 0.0) + jnp.log1p(jnp.exp(-jnp.abs(nl)))
        half_rate = (-0.5 * LRU_C) * softplus
        log_a = half_rate * t_r + half_rate
        a = jnp.exp(log_a)
        half_x = 0.5 * xc2
        bb = jnp.sqrt(1.0 - a * a) * (half_x * t_i + half_x)
        n_lane = cw_ // LANES
        pitch = tc + SCAN_PAD
        for c in range(n_lane):
            for bi in range(nb):
                a_s[c, bi * pitch:bi * pitch + tc, :] = a[bi * tc:(bi + 1) * tc, c * LANES:(c + 1) * LANES]
                b_s[c, bi * pitch:bi * pitch + tc, :] = bb[bi * tc:(bi + 1) * tc, c * LANES:(c + 1) * LANES]

        @pl.when(s == 0)
        def _():
            state[...] = jnp.zeros_like(state)

        def steps(tb, hs):
            hs = list(hs)
            for uu in range(SCAN_UNROLL):
                tt = tb * SCAN_UNROLL + uu
                t = (tc - 1 - tt) if reverse else tt
                rows = pl.ds(t, nb, stride=pitch)
                for c in range(n_lane):
                    hs[c] = a_s[c, rows, :] * hs[c] + b_s[c, rows, :]
                    h_s[c, rows, :] = hs[c]
            return tuple(hs)

        hs = lax.fori_loop(0, tc // SCAN_UNROLL, steps, tuple(state[c] for c in range(n_lane)))
        for c in range(n_lane):
            state[c] = hs[c]
            for bi in range(nb):
                o_ref[0, bi, :, c * LANES:(c + 1) * LANES] = h_s[c, bi * pitch:bi * pitch + tc, :].astype(o_ref.dtype)

    @pl.when(d == 0)
    def _():
        chunk = s
        keep = chunk > 1
        ext[:, 0:SUBLANES, :] = jnp.where(keep, halo_ref[...], 0.0)
        ext[:, SUBLANES:, :] = u
        xc = cw[3:4] * u
        for j in range(1, C_CONV):
            xc = xc + cw[3 - j:4 - j] * ext[:, SUBLANES - j:SUBLANES - j + tc, :]
        finish(xc, False)

    @pl.when(d == 1)
    def _():
        chunk = jnp.where(s == 0, 0, ntb - s)
        keep = jnp.logical_and(chunk > 0, chunk < ntb - 1)
        ext[:, 0:tc, :] = u
        ext[:, tc:, :] = jnp.where(keep, halo_ref[...], 0.0)
        xc = cw[0:1] * u
        for j in range(1, C_CONV):
            xc = xc + cw[j:j + 1] * ext[:, j:j + tc, :]
        finish(xc, True)


def _rglru(u, conv_w, conv_b, ga_w, ga_b, gx_w, gx_b, lam, b, l):
    width = u.shape[1]
    cb = width // C_BLOCKS
    ntb = l // TM
    u3 = u.reshape(b, l, width)
    nblk = TM // SUBLANES

    def chunk_of(d, s):
        return jnp.where(d == 0, s, jnp.where(s == 0, 0, ntb - s))

    def halo_of(d, s):
        c = chunk_of(d, s)
        return jnp.where(d == 0, jnp.maximum(c * nblk - 1, 0), jnp.minimum((c + 1) * nblk, l // SUBLANES - 1))

    vec = pl.BlockSpec((1, 1, cb), lambda d, g, s: (d, 0, g))
    mat = pl.BlockSpec((1, 1, cb, cb), lambda d, g, s: (d, g, 0, 0))
    return pl.pallas_call(
        functools.partial(_rglru_kernel, ntb=ntb),
        grid=(2, C_BLOCKS, ntb),
        in_specs=[
            pl.BlockSpec((b, TM, cb), lambda d, g, s: (0, chunk_of(d, s), g)),
            pl.BlockSpec((b, SUBLANES, cb), lambda d, g, s: (0, halo_of(d, s), g)),
            pl.BlockSpec((1, C_CONV, cb), lambda d, g, s: (d, 0, g)),
            vec, mat, vec, mat, vec, vec,
        ],
        out_specs=pl.BlockSpec((1, b, TM, cb), lambda d, g, s: (d, 0, chunk_of(d, s), g)),
        out_shape=jax.ShapeDtypeStruct((2, b, l, width), BF16),
        scratch_shapes=[
            pltpu.VMEM((b, TM + SUBLANES, cb), F32),
            pltpu.VMEM((cb // LANES, b * (TM + SCAN_PAD), LANES), F32),
            pltpu.VMEM((cb // LANES, b * (TM + SCAN_PAD), LANES), F32),
            pltpu.VMEM((cb // LANES, b * (TM + SCAN_PAD), LANES), F32),
            pltpu.VMEM((cb // LANES, b, LANES), F32),
        ],
        compiler_params=_cparams(("arbitrary", "arbitrary", "arbitrary")),
        name="l1_rglru",
    )(u3, u3, conv_w, conv_b.reshape(2, 1, width), (0.5 * ga_w).astype(BF16), (0.5 * ga_b).reshape(2, 1, width),
      (0.5 * gx_w).astype(BF16), (0.5 * gx_b).reshape(2, 1, width), lam.reshape(2, 1, width))


def _final_kernel(x_ref, pos_ref, ys_ref, route_ref, mod_ref, g_ref, o_ref, ybuf, pbuf, psem, csem):
    slot = _gather_choices(pos_ref, ys_ref, ybuf, pbuf, psem, csem)
    x = x_ref[...] + mod_ref[0][5:6] * _combine(ybuf, slot, route_ref, x_ref.shape[1])
    ms = jnp.mean(x * x, axis=-1, keepdims=True)
    o_ref[...] = x * lax.rsqrt(ms + NORM_EPS) * g_ref[...]


def _final(x, pos, ys, route, modv, g, b, s_len):
    t, d = x.shape
    per_b = s_len // TM
    row = lambda i: (i, 0)
    return pl.pallas_call(
        _final_kernel,
        grid=(t // TM,),
        in_specs=[
            pl.BlockSpec((TM, d), row),
            pl.BlockSpec(memory_space=pl.ANY),
            pl.BlockSpec(memory_space=pl.ANY),
            pl.BlockSpec((TM, LANES), row),
            pl.BlockSpec((1, 6, d), lambda i: ((i // per_b) * 2 + 1, 0, 0)),
            pl.BlockSpec((1, d), lambda i: (0, 0)),
        ],
        out_specs=pl.BlockSpec((TM, d), row),
        out_shape=jax.ShapeDtypeStruct((t, d), F32),
        scratch_shapes=_gather_scratch(d),
        compiler_params=_cparams(("arbitrary",)),
        name="final_combine_norm",
    )(x, pos, ys, route, modv, g)


def _rope_tables(s_len, n_ctx):
    n_rows = s_len // GRID_W
    rows, cols = jnp.meshgrid(jnp.arange(n_rows), jnp.arange(GRID_W), indexing="ij")
    pos = jnp.stack([rows.reshape(-1), cols.reshape(-1)], axis=-1).astype(F32)
    n_freq = HEAD_DIM // 4
    inv = ROPE_THETA ** (-jnp.arange(n_freq, dtype=F32) / n_freq)
    ang = pos[:, :, None] * inv
    cos, sin = jnp.cos(ang), jnp.sin(ang)
    cos64 = jnp.stack([cos, cos], axis=2).reshape(s_len, HEAD_DIM)
    sin64 = jnp.stack([-sin, sin], axis=2).reshape(s_len, HEAD_DIM)
    cos_l = jnp.tile(cos64, (1, LANES // HEAD_DIM))
    sin_l = jnp.tile(sin64, (1, LANES // HEAD_DIM))
    cos_t = jnp.concatenate([jnp.ones((n_ctx, LANES), F32), cos_l], axis=0)
    sin_t = jnp.concatenate([jnp.zeros((n_ctx, LANES), F32), sin_l], axis=0)
    return cos_t, sin_t


def kernel(x, c, ctx, c_ctx, w_mod, b_mod, norm_mix, norm_ffn, ev_w_in, ev_w_out, ev_lambda_q1, ev_lambda_k1, ev_lambda_q2, ev_lambda_k2, ev_subln, ev_conv_w, od_w_in, od_w_out, od_conv_w, od_conv_b, od_gate_a_w, od_gate_a_b, od_gate_x_w, od_gate_x_b, od_lru_lambda, moe_w_router, moe_b_router, moe_w1, moe_b1, moe_w2, moe_b2, final_norm):
    b, s_len, d = x.shape
    n_ctx = ctx.shape[1]
    l = n_ctx + s_len
    t = b * l
    ntb = l // TM
    assert n_ctx == TM and s_len % TM == 0 and w_mod.shape[0] == 2

    ctx2 = ctx.reshape(b * n_ctx, d)
    x2d = x.reshape(b * s_len, d)

    n_rows = -(-(b + 1) // SUBLANES) * SUBLANES
    cs = jnp.concatenate([c, c_ctx[None, :], jnp.zeros((n_rows - b - 1, d), F32)], axis=0)
    mod = _modulation(cs, w_mod, b_mod)

    def mod_table(i):
        lat = mod[i, :b]
        cx = jnp.broadcast_to(mod[i, b][None, :], lat.shape)
        return jnp.stack([cx, lat], axis=1).reshape(b * 2, 6, d)

    modv0, modv1 = mod_table(0), mod_table(1)

    cos_t, sin_t = _rope_tables(s_len, n_ctx)
    q, k, v, bg, p = _in_proj0(ctx2, x2d, modv0, norm_mix[0:1], ev_w_in[0].astype(BF16), cos_t, sin_t, ntb)
    lam_init = 0.8 - 0.6 * math.exp(-0.3 * 0)
    attn = _diff_attention(q, k, v, ev_lambda_q1[0:1], ev_lambda_k1[0:1], ev_lambda_q2[0:1], ev_lambda_k2[0:1],
                           ev_subln[0:1], lam_init, b, l, n_ctx)
    x1, h2, route0, counts0 = _out_proj0(attn, bg, p, ev_conv_w[0], ev_w_out[0].astype(BF16), ctx2, x2d, modv0,
                                         norm_ffn[0:1], moe_w_router[0], moe_b_router[0:1], ntb)
    pos0, ys0 = _moe_layer(route0, counts0, h2, moe_w1, moe_b1, moe_w2, moe_b2, 0)

    x2, gg, u = _in_proj1(x1, pos0, ys0, route0, modv0, modv1, norm_mix[1:2], od_w_in[0].astype(BF16), ntb)
    hs = _rglru(u, od_conv_w[0], od_conv_b[0], od_gate_a_w[0], od_gate_a_b[0], od_gate_x_w[0], od_gate_x_b[0],
                od_lru_lambda[0], b, l)
    x3, h3, route1, counts1 = _out_proj1(hs.reshape(2, t, hs.shape[-1]), gg, od_w_out[0].astype(BF16), x2, modv1,
                                         norm_ffn[1:2], moe_w_router[1], moe_b_router[1:2], ntb)
    pos1, ys1 = _moe_layer(route1, counts1, h3, moe_w1, moe_b1, moe_w2, moe_b2, 1)

    out = _final(x3, pos1, ys1, route1, modv1, final_norm[None, :], b, s_len)
    return out.reshape(b, s_len, d)
```

```python
import functools
import math

import jax
import jax.numpy as jnp
from jax import lax
from jax.experimental import pallas as pl
from jax.experimental.pallas import tpu as pltpu

F32 = jnp.float32
BF16 = jnp.bfloat16
I32 = jnp.int32

NORM_EPS = 1e-6
ROPE_THETA = 10000.0
GRID_W = 64
N_HEADS = 4
HEAD_DIM = 64
A_WIDTH = 2 * N_HEADS * HEAD_DIM
B_CONV = 3
C_CONV = 4
C_BLOCKS = 4
LRU_C = 8.0
TOP_K = 4
SWIGLU_ALPHA = 1.702
SWIGLU_LIMIT = 7.0

LANES = 128
SUBLANES = 8
TM = 256
TME = 256
VMEM_LIMIT = 56 * 1024 * 1024
SCAN_PAD = 8
SCAN_UNROLL = 8
LOG2_E = 1.4426950408889634


def _cparams(sem, vmem=VMEM_LIMIT):
    return pltpu.CompilerParams(dimension_semantics=sem, vmem_limit_bytes=vmem)


def _norm_mod(x, g, shift, scale):
    ms = jnp.mean(x * x, axis=-1, keepdims=True)
    return (x * lax.rsqrt(ms + NORM_EPS) * g) * (1.0 + scale) + shift


def _dot(a, b):
    return jnp.dot(a, b, preferred_element_type=F32)


def _sigmoid(x):
    return 0.5 * jnp.tanh(0.5 * x) + 0.5


def _mod_kernel(cs_ref, w_ref, b_ref, o_ref):
    s = cs_ref[...]
    s = s * jax.nn.sigmoid(s)
    o_ref[0] = _dot(s.astype(BF16), w_ref[0].astype(BF16)) + b_ref[0]


def _modulation(cs, w_mod, b_mod):
    depth, d, n = w_mod.shape
    rows = cs.shape[0]
    tn = 1536
    return pl.pallas_call(
        _mod_kernel,
        grid=(depth, n // tn),
        in_specs=[
            pl.BlockSpec((rows, d), lambda i, j: (0, 0)),
            pl.BlockSpec((1, d, tn), lambda i, j: (i, 0, j)),
            pl.BlockSpec((1, 1, tn), lambda i, j: (i, 0, j)),
        ],
        out_specs=pl.BlockSpec((1, rows, tn), lambda i, j: (i, 0, j)),
        out_shape=jax.ShapeDtypeStruct((depth, rows, n), F32),
        compiler_params=_cparams(("parallel", "parallel")),
        name="adaln_modulation",
    )(cs, w_mod, b_mod.reshape(depth, 1, n))


def _stream_tile(ctx_ref, x_ref, ntb):
    return jnp.where(pl.program_id(0) % ntb == 0, ctx_ref[...], x_ref[...])


def _stream_specs(d, ntb):
    return [pl.BlockSpec((TM, d), lambda i: (i // ntb, 0)),
            pl.BlockSpec((TM, d), lambda i: ((i // ntb) * (ntb - 1) + jnp.maximum(i % ntb - 1, 0), 0))]


def _in0_kernel(ctx_ref, x_ref, mod_ref, g_ref, w_ref, cos_ref, sin_ref,
                q_ref, k_ref, v_ref, bg_ref, p_ref, *, ntb):
    mod = mod_ref[0]
    h = _norm_mod(_stream_tile(ctx_ref, x_ref, ntb), g_ref[...], mod[0:1], mod[1:2])
    y = _dot(h.astype(BF16), w_ref[...])
    cosv = cos_ref[...]
    sinv = sin_ref[...]
    lane = lax.broadcasted_iota(I32, (TM, LANES), 1)
    first_half = (lane & 16) == 0

    def rope(z):
        outs = []
        for g in range(A_WIDTH // LANES):
            zg = z[:, g * LANES:(g + 1) * LANES]
            partner = jnp.where(first_half, pltpu.roll(zg, LANES - 16, 1), pltpu.roll(zg, 16, 1))
            outs.append(zg * cosv + partner * sinv)
        return jnp.concatenate(outs, axis=1)

    aw = A_WIDTH
    q_ref[...] = (rope(y[:, :aw]) * (HEAD_DIM ** -0.5 * LOG2_E)).astype(BF16)
    k_ref[...] = rope(y[:, aw:2 * aw]).astype(BF16)
    v_ref[...] = y[:, 2 * aw:3 * aw].astype(BF16)
    bw = (y.shape[1] - 3 * aw) // 3
    bg_ref[...] = y[:, 3 * aw:3 * aw + bw].astype(BF16)
    p_ref[...] = (y[:, 3 * aw + bw:3 * aw + 2 * bw] * y[:, 3 * aw + 2 * bw:]).astype(BF16)


def _in_proj0(ctx, x, modv, g, w, cos_t, sin_t, ntb):
    d = x.shape[1]
    t = ctx.shape[0] + x.shape[0]
    n = w.shape[1]
    bw = (n - 3 * A_WIDTH) // 3
    row = lambda i: (i, 0)
    return pl.pallas_call(
        functools.partial(_in0_kernel, ntb=ntb),
        grid=(t // TM,),
        in_specs=_stream_specs(d, ntb) + [
            pl.BlockSpec((1, 6, d), lambda i: ((i // ntb) * 2 + jnp.minimum(i % ntb, 1), 0, 0)),
            pl.BlockSpec((1, d), lambda i: (0, 0)),
            pl.BlockSpec((d, n), lambda i: (0, 0)),
            pl.BlockSpec((TM, LANES), lambda i: (i % ntb, 0)),
            pl.BlockSpec((TM, LANES), lambda i: (i % ntb, 0)),
        ],
        out_specs=[pl.BlockSpec((TM, A_WIDTH), row)] * 3 + [pl.BlockSpec((TM, bw), row)] * 2,
        out_shape=[jax.ShapeDtypeStruct((t, A_WIDTH), BF16)] * 3 + [jax.ShapeDtypeStruct((t, bw), BF16)] * 2,
        compiler_params=_cparams(("parallel",)),
        name="l0_in_proj_rope",
    )(ctx, x, modv, g, w, cos_t, sin_t)


def _attn_kernel(lq1_ref, lk1_ref, lq2_ref, lk2_ref, g_ref, q_ref, k_ref, v_ref, o_ref, *, lam_init, n_ctx):
    qi = pl.program_id(2)
    lam = (jnp.exp(jnp.sum(lq1_ref[...] * lk1_ref[...], axis=-1, keepdims=True))
           - jnp.exp(jnp.sum(lq2_ref[...] * lk2_ref[...], axis=-1, keepdims=True)) + lam_init)
    q = q_ref[...]
    lane = lax.broadcasted_iota(I32, q.shape, 1)
    zero = jnp.zeros_like(q)
    q1 = jnp.where(lane < HEAD_DIM, q, zero)
    q2 = jnp.where(lane < HEAD_DIM, zero, q)
    contract_last = (((1,), (1,)), ((), ()))

    def attend(nk):
        k = k_ref[0, :nk, :]
        v = v_ref[0, :nk, :]
        s1 = lax.dot_general(q1, k, contract_last, preferred_element_type=F32)
        s2 = lax.dot_general(q2, k, contract_last, preferred_element_type=F32)
        p1 = jnp.exp2(s1 - jnp.max(s1, axis=-1, keepdims=True))
        p2 = jnp.exp2(s2 - jnp.max(s2, axis=-1, keepdims=True))
        r1 = 1.0 / jnp.sum(p1, axis=-1, keepdims=True)
        r2 = lam / jnp.sum(p2, axis=-1, keepdims=True)
        o = _dot(p1.astype(BF16), v) * r1 - _dot(p2.astype(BF16), v) * r2
        ms = jnp.mean(o * o, axis=-1, keepdims=True)
        o = o * lax.rsqrt(ms + NORM_EPS) * g_ref[...] * (1.0 - lam_init)
        o_ref[...] = o.astype(BF16)

    @pl.when(qi == 0)
    def _():
        attend(n_ctx)

    @pl.when(qi > 0)
    def _():
        attend(k_ref.shape[1])


def _diff_attention(q, k, v, lq1, lk1, lq2, lk2, subln, lam_init, b, l, n_ctx):
    t = q.shape[0]
    ntb = l // TM
    hw = 2 * HEAD_DIM
    k3 = k.reshape(b, l, A_WIDTH)
    v3 = v.reshape(b, l, A_WIDTH)
    vec = lambda n: pl.BlockSpec((1, n), lambda bi, h, qi: (0, 0))
    qspec = pl.BlockSpec((TM, hw), lambda bi, h, qi: (bi * ntb + qi, h))
    kspec = pl.BlockSpec((1, l, hw), lambda bi, h, qi: (bi, 0, h))
    return pl.pallas_call(
        functools.partial(_attn_kernel, lam_init=lam_init, n_ctx=n_ctx),
        grid=(b, N_HEADS, ntb),
        in_specs=[vec(HEAD_DIM)] * 4 + [vec(hw), qspec, kspec, kspec],
        out_specs=qspec,
        out_shape=jax.ShapeDtypeStruct((t, A_WIDTH), BF16),
        compiler_params=_cparams(("parallel", "parallel", "parallel")),
        name="l0_diff_attention",
    )(lq1, lk1, lq2, lk2, subln, q, k3, v3)


def _route_tail(y, x_in, mod_ref, g_ref, wr_ref, br_ref,
                x1_ref, h2_ref, route_ref, cnt_ref, carry_ref):
    mod = mod_ref[0]
    x1 = x_in + mod[2:3] * y
    x1_ref[...] = x1
    h2 = _norm_mod(x1, g_ref[...], mod[3:4], mod[4:5])
    nch = h2.shape[1] // LANES
    for s in range(nch):
        h2_ref[pl.ds(s, TM, stride=nch), :] = h2[:, s * LANES:(s + 1) * LANES]

    wr = wr_ref[...]
    h_hi = h2.astype(BF16)
    h_lo = (h2 - h_hi.astype(F32)).astype(BF16)
    w_hi = wr.astype(BF16)
    w_lo = (wr - w_hi.astype(F32)).astype(BF16)
    logits = _dot(h_hi, w_hi) + _dot(h_hi, w_lo) + _dot(h_lo, w_hi) + br_ref[...]

    n_exp = logits.shape[1]
    lane = lax.broadcasted_iota(I32, logits.shape, 1).astype(F32)
    work = logits
    sels, vals, idxs = [], [], []
    for _ in range(TOP_K):
        m = jnp.max(work, axis=-1, keepdims=True)
        idx = jnp.min(jnp.where(work == m, lane, float(n_exp)), axis=-1, keepdims=True)
        sel = lane == idx
        sels.append(sel)
        vals.append(m)
        idxs.append(idx)
        work = jnp.where(sel, -jnp.inf, work)
    exps = [jnp.exp(vv - vals[0]) for vv in vals]
    inv_den = 1.0 / (exps[0] + exps[1] + exps[2] + exps[3])

    chosen = jnp.zeros(logits.shape, F32)
    for sel in sels:
        chosen = chosen + jnp.where(sel, 1.0, 0.0)
    r_i = lax.broadcasted_iota(I32, (TM, TM), 0)
    c_i = lax.broadcasted_iota(I32, (TM, TM), 1)
    earlier = jnp.where(c_i < r_i, 1.0, 0.0).astype(BF16)
    rank = _dot(earlier, chosen.astype(BF16)) + carry_ref[...]
    carry_ref[...] = carry_ref[...] + jnp.sum(chosen, axis=0, keepdims=True)
    cnt_ref[...] = carry_ref[...]

    out_lane = lax.broadcasted_iota(I32, (TM, LANES), 1)
    packed = jnp.zeros((TM, LANES), F32)
    for kk in range(TOP_K):
        rank_k = jnp.sum(jnp.where(sels[kk], rank, 0.0), axis=-1, keepdims=True)
        packed = jnp.where(out_lane == kk, idxs[kk], packed)
        packed = jnp.where(out_lane == TOP_K + kk, rank_k, packed)
        packed = jnp.where(out_lane == 2 * TOP_K + kk, exps[kk] * inv_den, packed)
    route_ref[...] = packed


def _out0_kernel(attn_ref, bg_ref, p_ref, pprev_ref, pnext_ref, cw_ref, wo_ref,
                 ctx_ref, x_ref, mod_ref, g_ref, wr_ref, br_ref,
                 x1_ref, h2_ref, route_ref, cnt_ref, carry_ref, *, ntb):
    i = pl.program_id(0)
    seg = i % ntb

    @pl.when(i == 0)
    def _():
        carry_ref[...] = jnp.zeros_like(carry_ref)

    p = p_ref[...].astype(F32)
    row = lax.broadcasted_iota(I32, p.shape, 0)
    has_prev = seg > 1
    has_next = jnp.logical_and(seg > 0, seg < ntb - 1)
    prev_row = jnp.where(has_prev, pprev_ref[SUBLANES - 1:SUBLANES, :].astype(F32), 0.0)
    next_row = jnp.where(has_next, pnext_ref[0:1, :].astype(F32), 0.0)
    before = jnp.where(row == 0, prev_row, pltpu.roll(p, 1, 0))
    after = jnp.where(row == TM - 1, next_row, pltpu.roll(p, TM - 1, 0))
    cw = cw_ref[...]
    conv = bg_ref[...].astype(F32) * (cw[0:1] * before + cw[1:2] * p + cw[2:3] * after)
    aw = attn_ref.shape[1]
    y = _dot(attn_ref[...], wo_ref[:aw, :]) + _dot(conv.astype(BF16), wo_ref[aw:, :])
    _route_tail(y, _stream_tile(ctx_ref, x_ref, ntb), mod_ref, g_ref, wr_ref, br_ref,
                x1_ref, h2_ref, route_ref, cnt_ref, carry_ref)


def _out1_kernel(hs_ref, gg_ref, wo_ref, x_ref, mod_ref, g_ref, wr_ref, br_ref,
                 x1_ref, h2_ref, route_ref, cnt_ref, carry_ref, *, ntb):
    i = pl.program_id(0)

    @pl.when(i == 0)
    def _():
        carry_ref[...] = jnp.zeros_like(carry_ref)

    @pl.when(i % ntb > 0)
    def _():
        rec = hs_ref[0].astype(F32) + hs_ref[1].astype(F32)
        y = _dot((rec * gg_ref[...].astype(F32)).astype(BF16), wo_ref[...])
        _route_tail(y, x_ref[...], mod_ref, g_ref, wr_ref, br_ref, x1_ref, h2_ref, route_ref, cnt_ref, carry_ref)


def _tail_specs(d, n_exp, ntb, latent_only):
    row = lambda i: (i, 0)
    const = lambda i: (0, 0)
    out_row = (lambda i: ((i // ntb) * (ntb - 1) + jnp.maximum(i % ntb - 1, 0), 0)) if latent_only else row
    in_specs = [
        pl.BlockSpec((TM, d), row),
        pl.BlockSpec((1, 6, d), lambda i: ((i // ntb) * 2 + jnp.minimum(i % ntb, 1), 0, 0)),
        pl.BlockSpec((1, d), const),
        pl.BlockSpec((d, n_exp), const),
        pl.BlockSpec((1, n_exp), const),
    ]
    out_specs = [
        pl.BlockSpec((TM, d), out_row),
        pl.BlockSpec((TM * (d // LANES), LANES), out_row),
        pl.BlockSpec((TM, LANES), out_row),
        pl.BlockSpec((1, n_exp), const),
    ]
    return in_specs, out_specs


def _tail_shapes(t, d, n_exp):
    return [jax.ShapeDtypeStruct((t, d), F32), jax.ShapeDtypeStruct((t * (d // LANES), LANES), F32),
            jax.ShapeDtypeStruct((t, LANES), F32), jax.ShapeDtypeStruct((1, n_exp), F32)]


def _out_proj0(attn, bg, p, conv_w, w_out, ctx, x, modv, g, w_r, b_r, ntb):
    d = x.shape[1]
    t = ctx.shape[0] + x.shape[0]
    n_exp = w_r.shape[1]
    bw = bg.shape[1]
    row = lambda i: (i, 0)
    const = lambda i: (0, 0)
    nblk = TM // SUBLANES
    tail_in, tail_out = _tail_specs(d, n_exp, ntb, False)
    return pl.pallas_call(
        functools.partial(_out0_kernel, ntb=ntb),
        grid=(t // TM,),
        in_specs=[
            pl.BlockSpec((TM, attn.shape[1]), row),
            pl.BlockSpec((TM, bw), row),
            pl.BlockSpec((TM, bw), row),
            pl.BlockSpec((SUBLANES, bw), lambda i: (jnp.maximum(i * nblk - 1, 0), 0)),
            pl.BlockSpec((SUBLANES, bw), lambda i: (jnp.minimum((i + 1) * nblk, t // SUBLANES - 1), 0)),
            pl.BlockSpec(conv_w.shape, const),
            pl.BlockSpec(w_out.shape, const),
        ] + _stream_specs(d, ntb) + tail_in[1:],
        out_specs=tail_out,
        out_shape=_tail_shapes(t, d, n_exp),
        scratch_shapes=[pltpu.VMEM((1, n_exp), F32)],
        compiler_params=_cparams(("arbitrary",)),
        name="l0_out_proj_router",
    )(attn, bg, p, p, p, conv_w, w_out, ctx, x, modv, g, w_r, b_r)


def _out_proj1(hs, gg, w_out, x, modv, g, w_r, b_r, ntb):
    t, d = x.shape
    n_exp = w_r.shape[1]
    row = lambda i: (i, 0)
    const = lambda i: (0, 0)
    tail_in, tail_out = _tail_specs(d, n_exp, ntb, True)
    t_lat = t // ntb * (ntb - 1)
    return pl.pallas_call(
        functools.partial(_out1_kernel, ntb=ntb),
        grid=(t // TM,),
        in_specs=[
            pl.BlockSpec((2, TM, hs.shape[2]), lambda i: (0, i, 0)),
            pl.BlockSpec((TM, gg.shape[1]), row),
            pl.BlockSpec(w_out.shape, const),
        ] + tail_in,
        out_specs=tail_out,
        out_shape=_tail_shapes(t_lat, d, n_exp),
        scratch_shapes=[pltpu.VMEM((1, n_exp), F32)],
        compiler_params=_cparams(("arbitrary",)),
        name="l1_out_proj_router",
    )(hs, gg, w_out, x, modv, g, w_r, b_r)


PAIRS_PER_TILE = TM * TOP_K


def _stage_pos(pos_ref, pbuf, psem):
    i = pl.program_id(0)
    slot = i % 2

    def chunk(ti, s):
        return pltpu.make_async_copy(pos_ref.at[pl.ds(ti * PAIRS_PER_TILE, PAIRS_PER_TILE)],
                                     pbuf.at[pl.ds(s * PAIRS_PER_TILE, PAIRS_PER_TILE)], psem.at[s])

    @pl.when(i == 0)
    def _():
        chunk(0, 0).start()

    chunk(i, slot).wait()

    @pl.when(i + 1 < pl.num_programs(0))
    def _():
        chunk(i + 1, 1 - slot).start()

    return slot * PAIRS_PER_TILE


def _dispatch_kernel(pad_lo_ref, pad_hi_ref, pos_ref, h2_ref, xs_ref, pbuf, zbuf, psem, dsem, zsem):
    i = pl.program_id(0)
    nch = h2_ref.shape[0] // TM
    n_tails = pad_lo_ref.shape[0] - 1

    def zero_slots(first, n):
        return pltpu.make_async_copy(zbuf.at[pl.ds(0, n * nch)],
                                     xs_ref.at[pl.ds(pl.multiple_of(first * nch, nch), n * nch)], zsem)

    def over_padding(act):
        def tail(e, carry):
            lo = pad_lo_ref[e]
            n = pad_hi_ref[e] - lo
            for bit in range(TME.bit_length() - 1):
                size = 1 << bit

                @pl.when((n & size) != 0)
                def _():
                    act(zero_slots(lo + (n & -(2 * size)), size))
            return carry

        lax.fori_loop(0, n_tails, tail, 0)

        def tile(j, c):
            act(zero_slots(j * TME, TME))
            return c

        lax.fori_loop(pad_lo_ref[n_tails] // TME, pad_hi_ref[n_tails] // TME, tile, 0)

    @pl.when(i == 0)
    def _():
        zbuf[...] = jnp.zeros(zbuf.shape, zbuf.dtype)
        over_padding(lambda cp: cp.start())
        over_padding(lambda cp: cp.wait())

    off = _stage_pos(pos_ref, pbuf, psem)
    for r in range(TM):
        for kk in range(TOP_K):
            dst = pl.multiple_of(pbuf[off + r * TOP_K + kk] * nch, nch)
            pltpu.make_async_copy(h2_ref.at[pl.ds(r * nch, nch)], xs_ref.at[pl.ds(dst, nch)], dsem).start(
                priority=kk % 2)
    for _ in range(TOP_K):
        pltpu.make_async_copy(h2_ref, xs_ref.at[pl.ds(0, TM * nch)], dsem).wait()


def _dispatch(pos, pad_lo, pad_hi, h2, n_slots, d):
    nch = d // LANES
    n_tok = h2.shape[0] // nch
    grid_spec = pltpu.PrefetchScalarGridSpec(
        num_scalar_prefetch=2,
        grid=(n_tok // TM,),
        in_specs=[pl.BlockSpec(memory_space=pl.ANY),
                  pl.BlockSpec((TM * nch, LANES), lambda i, lo, hi: (i, 0))],
        out_specs=pl.BlockSpec(memory_space=pl.ANY),
        scratch_shapes=[pltpu.SMEM((2 * PAIRS_PER_TILE,), I32), pltpu.VMEM((TME * nch, LANES), F32),
                        pltpu.SemaphoreType.DMA((2,)), pltpu.SemaphoreType.DMA, pltpu.SemaphoreType.DMA],
    )
    return pl.pallas_call(
        _dispatch_kernel,
        grid_spec=grid_spec,
        out_shape=jax.ShapeDtypeStruct((n_slots * nch, LANES), F32),
        compiler_params=_cparams(("arbitrary",)),
        name="expert_dispatch",
    )(pad_lo, pad_hi, pos, h2)


def _moe_kernel(te_ref, nx_ref, par_ref, meta_ref, x_ref, w1_hbm, b1_ref, w2_hbm, b2_ref, y_ref,
                w1f, w2f, w1b, w2b, wsem, *, layer):
    i = pl.program_id(0)
    n_tiles = meta_ref[0]
    d = w1b.shape[0]
    ff = w2b.shape[0]
    nch = d // LANES

    def weight_copies(e, s):
        return (pltpu.make_async_copy(w1_hbm.at[layer, e], w1f.at[s], wsem.at[0, s]),
                pltpu.make_async_copy(w2_hbm.at[layer, e], w2f.at[s], wsem.at[1, s]))

    @pl.when(i == 0)
    def _():
        for cp in weight_copies(te_ref[0], par_ref[0]):
            cp.start()

    @pl.when(i < n_tiles)
    def _():
        new_expert = jnp.logical_or(i == 0, te_ref[i] != te_ref[jnp.maximum(i - 1, 0)])

        @pl.when(new_expert)
        def _():
            s = par_ref[i]
            for cp in weight_copies(te_ref[i], s):
                cp.wait()
            w1b[...] = w1f[s].astype(BF16)
            w2b[...] = w2f[s].astype(BF16)

            @pl.when(nx_ref[i] >= 0)
            def _():
                for cp in weight_copies(nx_ref[i], 1 - s):
                    cp.start()

        x = jnp.concatenate([x_ref[pl.ds(c, TME, stride=nch), :] for c in range(nch)], axis=1)
        h = _dot(x.astype(BF16), w1b[...]) + b1_ref[0]
        glu = jnp.minimum(h[:, :ff], SWIGLU_LIMIT)
        lin = jnp.clip(h[:, ff:], -SWIGLU_LIMIT, SWIGLU_LIMIT)
        act = glu * _sigmoid(SWIGLU_ALPHA * glu) * (lin + 1.0)
        y = _dot(act.astype(BF16), w2b[...]) + b2_ref[0]
        for c in range(nch):
            y_ref[pl.ds(c, TME, stride=nch), :] = y[:, c * LANES:(c + 1) * LANES]

    @pl.when(i >= n_tiles)
    def _():
        y_ref[...] = jnp.zeros(y_ref.shape, y_ref.dtype)


def _routed_experts(tile_expert, next_expert, parity, meta, xs, w1, b1, w2, b2, layer):
    _, n_exp, d, ff2 = w1.shape
    ff = w2.shape[2]
    nch = d // LANES
    n_grid = tile_expert.shape[0]
    used = lambda i, mt: jnp.minimum(i, mt[0] - 1)
    bmap = lambda i, te, nx, pr, mt: (layer * n_exp + te[used(i, mt)], 0, 0)
    grid_spec = pltpu.PrefetchScalarGridSpec(
        num_scalar_prefetch=4,
        grid=(n_grid,),
        in_specs=[
            pl.BlockSpec((TME * nch, LANES), lambda i, te, nx, pr, mt: (used(i, mt), 0)),
            pl.BlockSpec(memory_space=pl.ANY),
            pl.BlockSpec((1, 1, ff2), bmap),
            pl.BlockSpec(memory_space=pl.ANY),
            pl.BlockSpec((1, 1, d), bmap),
        ],
        out_specs=pl.BlockSpec((TME * nch, LANES), lambda i, te, nx, pr, mt: (i, 0)),
        scratch_shapes=[pltpu.VMEM((2, d, ff2), F32), pltpu.VMEM((2, ff, d), F32),
                        pltpu.VMEM((d, ff2), BF16), pltpu.VMEM((ff, d), BF16), pltpu.SemaphoreType.DMA((2, 2))],
    )
    return pl.pallas_call(
        functools.partial(_moe_kernel, layer=layer),
        grid_spec=grid_spec,
        out_shape=jax.ShapeDtypeStruct(xs.shape, F32),
        compiler_params=_cparams(("arbitrary",)),
        name="routed_experts",
    )(tile_expert, next_expert, parity, meta, xs, w1, b1.reshape(-1, 1, ff2), w2, b2.reshape(-1, 1, d))


def _gather_choices(pos_ref, ys_ref, ybuf, pbuf, psem, csem):
    i = pl.program_id(0)
    n = pl.num_programs(0)
    slot = i % 2
    nch = ybuf.shape[2] // TM

    def chunk(ti, s):
        return pltpu.make_async_copy(pos_ref.at[pl.ds(ti * PAIRS_PER_TILE, PAIRS_PER_TILE)],
                                     pbuf.at[pl.ds(s * PAIRS_PER_TILE, PAIRS_PER_TILE)], psem.at[s])

    def row_copy(s, r, kk):
        src = pl.multiple_of(pbuf[s * PAIRS_PER_TILE + r * TOP_K + kk] * nch, nch)
        return pltpu.make_async_copy(ys_ref.at[pl.ds(src, nch)], ybuf.at[s, kk, pl.ds(r * nch, nch)], csem.at[s])

    @pl.when(i == 0)
    def _():
        chunk(0, 0).start()
        chunk(0, 0).wait()

        def first_rows(r, c):
            for kk in range(TOP_K):
                row_copy(0, r, kk).start(priority=kk % 2)
            return c

        lax.fori_loop(0, TM, first_rows, 0)

        @pl.when(n > 1)
        def _():
            chunk(1, 1).start()

    @pl.when(i + 1 < n)
    def _():
        chunk(i + 1, 1 - slot).wait()
        for r in range(TM):
            for kk in range(TOP_K):
                row_copy(1 - slot, r, kk).start(priority=kk % 2)

        @pl.when(i + 2 < n)
        def _():
            chunk(i + 2, slot).start()

    for kk in range(TOP_K):
        pltpu.make_async_copy(ys_ref.at[pl.ds(0, TM * nch)], ybuf.at[slot, kk], csem.at[slot]).wait()
    return slot


def _gather_scratch(d):
    nch = d // LANES
    return [pltpu.VMEM((2, TOP_K, TM * nch, LANES), F32), pltpu.SMEM((2 * PAIRS_PER_TILE,), I32),
            pltpu.SemaphoreType.DMA((2,)), pltpu.SemaphoreType.DMA((2,))]


def _plan_routes(route, counts, n_grid):
    n_exp = counts.shape[-1]
    cnt = counts.reshape(n_exp).astype(I32)
    tiles_per = (cnt + TME - 1) // TME
    tile_end = jnp.cumsum(tiles_per)
    offset = (tile_end - tiles_per) * TME
    n_tiles = tile_end[-1]
    tile_ids = jnp.minimum(jnp.arange(n_grid, dtype=I32), n_tiles - 1)
    tile_expert = jnp.sum((tile_ids[:, None] >= tile_end[None, :]).astype(I32), axis=1)
    expert_ids = jnp.arange(n_exp, dtype=I32)
    of_tile = tile_expert[:, None] == expert_ids
    later = jnp.logical_and(expert_ids[None, :] > expert_ids[:, None], (tiles_per > 0)[None, :])
    next_of = jnp.min(jnp.where(later, expert_ids[None, :], n_exp), axis=1)
    next_of = jnp.where(next_of == n_exp, -1, next_of)
    next_expert = jnp.sum(jnp.where(of_tile, next_of, 0), axis=1)
    parity = jnp.sum(jnp.where(of_tile, jnp.cumsum((tiles_per > 0).astype(I32)), 0), axis=1) % 2
    eidx = route[:, :TOP_K].astype(I32)
    rank = route[:, TOP_K:2 * TOP_K].astype(I32)
    pos = jnp.sum(jnp.where(eidx[..., None] == expert_ids, offset, 0), axis=-1) + rank
    pad_lo = jnp.concatenate([offset + cnt, (n_tiles * TME).reshape(1)])
    pad_hi = jnp.concatenate([tile_end * TME, jnp.full((1,), n_grid * TME, I32)])
    return (tile_expert, next_expert, parity, n_tiles.reshape(1)), pos.reshape(-1), pad_lo, pad_hi


def _moe_layer(route, counts, h2, w1, b1, w2, b2, layer):
    n_exp, d = w1.shape[1], w1.shape[2]
    n_tok = route.shape[0]
    n_grid = n_tok * TOP_K // TME + n_exp
    tiles, pos, pad_lo, pad_hi = _plan_routes(route, counts, n_grid)
    xs = _dispatch(pos, pad_lo, pad_hi, h2, n_grid * TME, d)
    return pos, _routed_experts(*tiles, xs, w1, b1, w2, b2, layer)


def _combine(ybuf, slot, route_ref, d):
    nch = d // LANES
    gates = route_ref[...]
    gk = [jnp.broadcast_to(gates[:, 2 * TOP_K + kk:2 * TOP_K + kk + 1], (TM, LANES)) for kk in range(TOP_K)]
    chunks = []
    for c in range(nch):
        acc = gk[0] * ybuf[slot, 0, pl.ds(c, TM, stride=nch), :]
        for kk in range(1, TOP_K):
            acc = acc + gk[kk] * ybuf[slot, kk, pl.ds(c, TM, stride=nch), :]
        chunks.append(acc)
    return jnp.concatenate(chunks, axis=1)


def _in1_kernel(x_ref, pos_ref, ys_ref, route_ref, mod0_ref, mod1_ref, g_ref, w_ref, x2_ref, gg_ref, u_ref,
                ybuf, pbuf, psem, csem):
    slot = _gather_choices(pos_ref, ys_ref, ybuf, pbuf, psem, csem)
    x2 = x_ref[...] + mod0_ref[0][5:6] * _combine(ybuf, slot, route_ref, x_ref.shape[1])
    x2_ref[...] = x2
    mod1 = mod1_ref[0]
    h = _norm_mod(x2, g_ref[...], mod1[0:1], mod1[1:2])
    y = _dot(h.astype(BF16), w_ref[...])
    half = y.shape[1] // 2
    gg_ref[...] = jax.nn.gelu(y[:, :half]).astype(BF16)
    u_ref[...] = y[:, half:]


def _in_proj1(x, pos, ys, route, modv0, modv1, g, w, ntb):
    t, d = x.shape
    n = w.shape[1]
    row = lambda i: (i, 0)
    mod_map = lambda i: ((i // ntb) * 2 + jnp.minimum(i % ntb, 1), 0, 0)
    return pl.pallas_call(
        _in1_kernel,
        grid=(t // TM,),
        in_specs=[
            pl.BlockSpec((TM, d), row),
            pl.BlockSpec(memory_space=pl.ANY),
            pl.BlockSpec(memory_space=pl.ANY),
            pl.BlockSpec((TM, LANES), row),
            pl.BlockSpec((1, 6, d), mod_map),
            pl.BlockSpec((1, 6, d), mod_map),
            pl.BlockSpec((1, d), lambda i: (0, 0)),
            pl.BlockSpec((d, n), lambda i: (0, 0)),
        ],
        out_specs=[pl.BlockSpec((TM, d), row), pl.BlockSpec((TM, n // 2), row), pl.BlockSpec((TM, n // 2), row)],
        out_shape=[jax.ShapeDtypeStruct((t, d), F32), jax.ShapeDtypeStruct((t, n // 2), BF16),
                   jax.ShapeDtypeStruct((t, n // 2), F32)],
        scratch_shapes=_gather_scratch(d),
        compiler_params=_cparams(("arbitrary",)),
        name="l1_combine_in_proj",
    )(x, pos, ys, route, modv0, modv1, g, w)


def _rglru_kernel(u_ref, halo_ref, cw_ref, cb_ref, gaw_ref, gab_ref, gxw_ref, gxb_ref, lam_ref, o_ref,
                  ext, a_s, b_s, h_s, state, *, ntb):
    d = pl.program_id(0)
    s = pl.program_id(2)
    nb, tc, cw_ = u_ref.shape
    u = u_ref[...]
    cw = cw_ref[0]

    def finish(xc, reverse):
        xc2 = xc.reshape(nb * tc, cw_) + cb_ref[0]
        xb = xc2.astype(BF16)
        t_r = jnp.tanh(_dot(xb, gaw_ref[0, 0]) + gab_ref[0])
        t_i = jnp.tanh(_dot(xb, gxw_ref[0, 0]) + gxb_ref[0])
        nl = -lam_ref[0]
        softplus = jnp.maximum(nl, 0.0) + jnp.log1p(jnp.exp(-jnp.abs(nl)))
        half_rate = (-0.5 * LRU_C) * softplus
        log_a = half_rate * t_r + half_rate
        a = jnp.exp(log_a)
        half_x = 0.5 * xc2
        bb = jnp.sqrt(1.0 - a * a) * (half_x * t_i + half_x)
        n_lane = cw_ // LANES
        pitch = tc + SCAN_PAD
        for c in range(n_lane):
            for bi in range(nb):
                a_s[c, bi * pitch:bi * pitch + tc, :] = a[bi * tc:(bi + 1) * tc, c * LANES:(c + 1) * LANES]
                b_s[c, bi * pitch:bi * pitch + tc, :] = bb[bi * tc:(bi + 1) * tc, c * LANES:(c + 1) * LANES]

        @pl.when(s == 0)
        def _():
            state[...] = jnp.zeros_like(state)

        def steps(tb, hs):
            hs = list(hs)
            for uu in range(SCAN_UNROLL):
                tt = tb * SCAN_UNROLL + uu
                t = (tc - 1 - tt) if reverse else tt
                rows = pl.ds(t, nb, stride=pitch)
                for c in range(n_lane):
                    hs[c] = a_s[c, rows, :] * hs[c] + b_s[c, rows, :]
                    h_s[c, rows, :] = hs[c]
            return tuple(hs)

        hs = lax.fori_loop(0, tc // SCAN_UNROLL, steps, tuple(state[c] for c in range(n_lane)))
        for c in range(n_lane):
            state[c] = hs[c]
            for bi in range(nb):
                o_ref[0, bi, :, c * LANES:(c + 1) * LANES] = h_s[c, bi * pitch:bi * pitch + tc, :].astype(o_ref.dtype)

    @pl.when(d == 0)
    def _():
        chunk = s
        keep = chunk > 1
        ext[:, 0:SUBLANES, :] = jnp.where(keep, halo_ref[...], 0.0)
        ext[:, SUBLANES:, :] = u
        xc = cw[3:4] * u
        for j in range(1, C_CONV):
            xc = xc + cw[3 - j:4 - j] * ext[:, SUBLANES - j:SUBLANES - j + tc, :]
        finish(xc, False)

    @pl.when(d == 1)
    def _():
        chunk = jnp.where(s == 0, 0, ntb - s)
        keep = jnp.logical_and(chunk > 0, chunk < ntb - 1)
        ext[:, 0:tc, :] = u
        ext[:, tc:, :] = jnp.where(keep, halo_ref[...], 0.0)
        xc = cw[0:1] * u
        for j in range(1, C_CONV):
            xc = xc + cw[j:j + 1] * ext[:, j:j + tc, :]
        finish(xc, True)


def _rglru(u, conv_w, conv_b, ga_w, ga_b, gx_w, gx_b, lam, b, l):
    width = u.shape[1]
    cb = width // C_BLOCKS
    ntb = l // TM
    u3 = u.reshape(b, l, width)
    nblk = TM // SUBLANES

    def chunk_of(d, s):
        return jnp.where(d == 0, s, jnp.where(s == 0, 0, ntb - s))

    def halo_of(d, s):
        c = chunk_of(d, s)
        return jnp.where(d == 0, jnp.maximum(c * nblk - 1, 0), jnp.minimum((c + 1) * nblk, l // SUBLANES - 1))

    vec = pl.BlockSpec((1, 1, cb), lambda d, g, s: (d, 0, g))
    mat = pl.BlockSpec((1, 1, cb, cb), lambda d, g, s: (d, g, 0, 0))
    return pl.pallas_call(
        functools.partial(_rglru_kernel, ntb=ntb),
        grid=(2, C_BLOCKS, ntb),
        in_specs=[
            pl.BlockSpec((b, TM, cb), lambda d, g, s: (0, chunk_of(d, s), g)),
            pl.BlockSpec((b, SUBLANES, cb), lambda d, g, s: (0, halo_of(d, s), g)),
            pl.BlockSpec((1, C_CONV, cb), lambda d, g, s: (d, 0, g)),
            vec, mat, vec, mat, vec, vec,
        ],
        out_specs=pl.BlockSpec((1, b, TM, cb), lambda d, g, s: (d, 0, chunk_of(d, s), g)),
        out_shape=jax.ShapeDtypeStruct((2, b, l, width), BF16),
        scratch_shapes=[
            pltpu.VMEM((b, TM + SUBLANES, cb), F32),
            pltpu.VMEM((cb // LANES, b * (TM + SCAN_PAD), LANES), F32),
            pltpu.VMEM((cb // LANES, b * (TM + SCAN_PAD), LANES), F32),
            pltpu.VMEM((cb // LANES, b * (TM + SCAN_PAD), LANES), F32),
            pltpu.VMEM((cb // LANES, b, LANES), F32),
        ],
        compiler_params=_cparams(("arbitrary", "arbitrary", "arbitrary")),
        name="l1_rglru",
    )(u3, u3, conv_w, conv_b.reshape(2, 1, width), (0.5 * ga_w).astype(BF16), (0.5 * ga_b).reshape(2, 1, width),
      (0.5 * gx_w).astype(BF16), (0.5 * gx_b).reshape(2, 1, width), lam.reshape(2, 1, width))


def _final_kernel(x_ref, pos_ref, ys_ref, route_ref, mod_ref, g_ref, o_ref, ybuf, pbuf, psem, csem):
    slot = _gather_choices(pos_ref, ys_ref, ybuf, pbuf, psem, csem)
    x = x_ref[...] + mod_ref[0][5:6] * _combine(ybuf, slot, route_ref, x_ref.shape[1])
    ms = jnp.mean(x * x, axis=-1, keepdims=True)
    o_ref[...] = x * lax.rsqrt(ms + NORM_EPS) * g_ref[...]


def _final(x, pos, ys, route, modv, g, b, s_len):
    t, d = x.shape
    per_b = s_len // TM
    row = lambda i: (i, 0)
    return pl.pallas_call(
        _final_kernel,
        grid=(t // TM,),
        in_specs=[
            pl.BlockSpec((TM, d), row),
            pl.BlockSpec(memory_space=pl.ANY),
            pl.BlockSpec(memory_space=pl.ANY),
            pl.BlockSpec((TM, LANES), row),
            pl.BlockSpec((1, 6, d), lambda i: ((i // per_b) * 2 + 1, 0, 0)),
            pl.BlockSpec((1, d), lambda i: (0, 0)),
        ],
        out_specs=pl.BlockSpec((TM, d), row),
        out_shape=jax.ShapeDtypeStruct((t, d), F32),
        scratch_shapes=_gather_scratch(d),
        compiler_params=_cparams(("arbitrary",)),
        name="final_combine_norm",
    )(x, pos, ys, route, modv, g)


def _rope_tables(s_len, n_ctx):
    n_rows = s_len // GRID_W
    rows, cols = jnp.meshgrid(jnp.arange(n_rows), jnp.arange(GRID_W), indexing="ij")
    pos = jnp.stack([rows.reshape(-1), cols.reshape(-1)], axis=-1).astype(F32)
    n_freq = HEAD_DIM // 4
    inv = ROPE_THETA ** (-jnp.arange(n_freq, dtype=F32) / n_freq)
    ang = pos[:, :, None] * inv
    cos, sin = jnp.cos(ang), jnp.sin(ang)
    cos64 = jnp.stack([cos, cos], axis=2).reshape(s_len, HEAD_DIM)
    sin64 = jnp.stack([-sin, sin], axis=2).reshape(s_len, HEAD_DIM)
    cos_l = jnp.tile(cos64, (1, LANES // HEAD_DIM))
    sin_l = jnp.tile(sin64, (1, LANES // HEAD_DIM))
    cos_t = jnp.concatenate([jnp.ones((n_ctx, LANES), F32), cos_l], axis=0)
    sin_t = jnp.concatenate([jnp.zeros((n_ctx, LANES), F32), sin_l], axis=0)
    return cos_t, sin_t


def kernel(x, c, ctx, c_ctx, w_mod, b_mod, norm_mix, norm_ffn, ev_w_in, ev_w_out, ev_lambda_q1, ev_lambda_k1, ev_lambda_q2, ev_lambda_k2, ev_subln, ev_conv_w, od_w_in, od_w_out, od_conv_w, od_conv_b, od_gate_a_w, od_gate_a_b, od_gate_x_w, od_gate_x_b, od_lru_lambda, moe_w_router, moe_b_router, moe_w1, moe_b1, moe_w2, moe_b2, final_norm):
    b, s_len, d = x.shape
    n_ctx = ctx.shape[1]
    l = n_ctx + s_len
    t = b * l
    ntb = l // TM
    assert n_ctx == TM and s_len % TM == 0 and w_mod.shape[0] == 2

    ctx2 = ctx.reshape(b * n_ctx, d)
    x2d = x.reshape(b * s_len, d)

    n_rows = -(-(b + 1) // SUBLANES) * SUBLANES
    cs = jnp.concatenate([c, c_ctx[None, :], jnp.zeros((n_rows - b - 1, d), F32)], axis=0)
    mod = _modulation(cs, w_mod, b_mod)

    def mod_table(i):
        lat = mod[i, :b]
        cx = jnp.broadcast_to(mod[i, b][None, :], lat.shape)
        return jnp.stack([cx, lat], axis=1).reshape(b * 2, 6, d)

    modv0, modv1 = mod_table(0), mod_table(1)

    cos_t, sin_t = _rope_tables(s_len, n_ctx)
    q, k, v, bg, p = _in_proj0(ctx2, x2d, modv0, norm_mix[0:1], ev_w_in[0].astype(BF16), cos_t, sin_t, ntb)
    lam_init = 0.8 - 0.6 * math.exp(-0.3 * 0)
    attn = _diff_attention(q, k, v, ev_lambda_q1[0:1], ev_lambda_k1[0:1], ev_lambda_q2[0:1], ev_lambda_k2[0:1],
                           ev_subln[0:1], lam_init, b, l, n_ctx)
    x1, h2, route0, counts0 = _out_proj0(attn, bg, p, ev_conv_w[0], ev_w_out[0].astype(BF16), ctx2, x2d, modv0,
                                         norm_ffn[0:1], moe_w_router[0], moe_b_router[0:1], ntb)
    pos0, ys0 = _moe_layer(route0, counts0, h2, moe_w1, moe_b1, moe_w2, moe_b2, 0)

    x2, gg, u = _in_proj1(x1, pos0, ys0, route0, modv0, modv1, norm_mix[1:2], od_w_in[0].astype(BF16), ntb)
    hs = _rglru(u, od_conv_w[0], od_conv_b[0], od_gate_a_w[0], od_gate_a_b[0], od_gate_x_w[0], od_gate_x_b[0],
                od_lru_lambda[0], b, l)
    x3, h3, route1, counts1 = _out_proj1(hs.reshape(2, t, hs.shape[-1]), gg, od_w_out[0].astype(BF16), x2, modv1,
                                         norm_ffn[1:2], moe_w_router[1], moe_b_router[1:2], ntb)
    pos1, ys1 = _moe_layer(route1, counts1, h3, moe_w1, moe_b1, moe_w2, moe_b2, 1)

    out = _final(x3, pos1, ys1, route1, modv1, final_norm[None, :], b, s_len)
    return out.reshape(b, s_len, d)
```

```python
import functools
import math

import jax
import jax.numpy as jnp
from jax import lax
from jax.experimental import pallas as pl
from jax.experimental.pallas import tpu as pltpu

F32 = jnp.float32
BF16 = jnp.bfloat16
I32 = jnp.int32

NORM_EPS = 1e-6
ROPE_THETA = 10000.0
GRID_W = 64
N_HEADS = 4
HEAD_DIM = 64
A_WIDTH = 2 * N_HEADS * HEAD_DIM
B_CONV = 3
C_CONV = 4
C_BLOCKS = 4
LRU_C = 8.0
TOP_K = 4
SWIGLU_ALPHA = 1.702
SWIGLU_LIMIT = 7.0

LANES = 128
SUBLANES = 8
TM = 256
TME = 256
VMEM_LIMIT = 56 * 1024 * 1024
SCAN_PAD = 8
SCAN_UNROLL = 8
LOG2_E = 1.4426950408889634


def _cparams(sem, vmem=VMEM_LIMIT):
    return pltpu.CompilerParams(dimension_semantics=sem, vmem_limit_bytes=vmem)


def _norm_mod(x, g, shift, scale):
    ms = jnp.mean(x * x, axis=-1, keepdims=True)
    return (x * lax.rsqrt(ms + NORM_EPS) * g) * (1.0 + scale) + shift


def _dot(a, b):
    return jnp.dot(a, b, preferred_element_type=F32)


def _sigmoid(x):
    return 0.5 * jnp.tanh(0.5 * x) + 0.5


def _mod_kernel(cs_ref, w_ref, b_ref, o_ref):
    s = cs_ref[...]
    s = s * jax.nn.sigmoid(s)
    o_ref[0] = _dot(s.astype(BF16), w_ref[0].astype(BF16)) + b_ref[0]


def _modulation(cs, w_mod, b_mod):
    depth, d, n = w_mod.shape
    rows = cs.shape[0]
    tn = 1536
    return pl.pallas_call(
        _mod_kernel,
        grid=(depth, n // tn),
        in_specs=[
            pl.BlockSpec((rows, d), lambda i, j: (0, 0)),
            pl.BlockSpec((1, d, tn), lambda i, j: (i, 0, j)),
            pl.BlockSpec((1, 1, tn), lambda i, j: (i, 0, j)),
        ],
        out_specs=pl.BlockSpec((1, rows, tn), lambda i, j: (i, 0, j)),
        out_shape=jax.ShapeDtypeStruct((depth, rows, n), F32),
        compiler_params=_cparams(("parallel", "parallel")),
        name="adaln_modulation",
    )(cs, w_mod, b_mod.reshape(depth, 1, n))


def _stream_tile(ctx_ref, x_ref, ntb):
    return jnp.where(pl.program_id(0) % ntb == 0, ctx_ref[...], x_ref[...])


def _stream_specs(d, ntb):
    return [pl.BlockSpec((TM, d), lambda i: (i // ntb, 0)),
            pl.BlockSpec((TM, d), lambda i: ((i // ntb) * (ntb - 1) + jnp.maximum(i % ntb - 1, 0), 0))]


def _in0_kernel(ctx_ref, x_ref, mod_ref, g_ref, w_ref, cos_ref, sin_ref,
                q_ref, k_ref, v_ref, bg_ref, p_ref, *, ntb):
    mod = mod_ref[0]
    h = _norm_mod(_stream_tile(ctx_ref, x_ref, ntb), g_ref[...], mod[0:1], mod[1:2])
    y = _dot(h.astype(BF16), w_ref[...])
    cosv = cos_ref[...]
    sinv = sin_ref[...]
    lane = lax.broadcasted_iota(I32, (TM, LANES), 1)
    first_half = (lane & 16) == 0

    def rope(z):
        outs = []
        for g in range(A_WIDTH // LANES):
            zg = z[:, g * LANES:(g + 1) * LANES]
            partner = jnp.where(first_half, pltpu.roll(zg, LANES - 16, 1), pltpu.roll(zg, 16, 1))
            outs.append(zg * cosv + partner * sinv)
        return jnp.concatenate(outs, axis=1)

    aw = A_WIDTH
    q_ref[...] = (rope(y[:, :aw]) * (HEAD_DIM ** -0.5 * LOG2_E)).astype(BF16)
    k_ref[...] = rope(y[:, aw:2 * aw]).astype(BF16)
    v_ref[...] = y[:, 2 * aw:3 * aw].astype(BF16)
    bw = (y.shape[1] - 3 * aw) // 3
    bg_ref[...] = y[:, 3 * aw:3 * aw + bw].astype(BF16)
    p_ref[...] = (y[:, 3 * aw + bw:3 * aw + 2 * bw] * y[:, 3 * aw + 2 * bw:]).astype(BF16)


def _in_proj0(ctx, x, modv, g, w, cos_t, sin_t, ntb):
    d = x.shape[1]
    t = ctx.shape[0] + x.shape[0]
    n = w.shape[1]
    bw = (n - 3 * A_WIDTH) // 3
    row = lambda i: (i, 0)
    return pl.pallas_call(
        functools.partial(_in0_kernel, ntb=ntb),
        grid=(t // TM,),
        in_specs=_stream_specs(d, ntb) + [
            pl.BlockSpec((1, 6, d), lambda i: ((i // ntb) * 2 + jnp.minimum(i % ntb, 1), 0, 0)),
            pl.BlockSpec((1, d), lambda i: (0, 0)),
            pl.BlockSpec((d, n), lambda i: (0, 0)),
            pl.BlockSpec((TM, LANES), lambda i: (i % ntb, 0)),
            pl.BlockSpec((TM, LANES), lambda i: (i % ntb, 0)),
        ],
        out_specs=[pl.BlockSpec((TM, A_WIDTH), row)] * 3 + [pl.BlockSpec((TM, bw), row)] * 2,
        out_shape=[jax.ShapeDtypeStruct((t, A_WIDTH), BF16)] * 3 + [jax.ShapeDtypeStruct((t, bw), BF16)] * 2,
        compiler_params=_cparams(("parallel",)),
        name="l0_in_proj_rope",
    )(ctx, x, modv, g, w, cos_t, sin_t)


def _attn_kernel(lq1_ref, lk1_ref, lq2_ref, lk2_ref, g_ref, q_ref, k_ref, v_ref, o_ref, *, lam_init, n_ctx):
    qi = pl.program_id(2)
    lam = (jnp.exp(jnp.sum(lq1_ref[...] * lk1_ref[...], axis=-1, keepdims=True))
           - jnp.exp(jnp.sum(lq2_ref[...] * lk2_ref[...], axis=-1, keepdims=True)) + lam_init)
    q = q_ref[...]
    lane = lax.broadcasted_iota(I32, q.shape, 1)
    zero = jnp.zeros_like(q)
    q1 = jnp.where(lane < HEAD_DIM, q, zero)
    q2 = jnp.where(lane < HEAD_DIM, zero, q)
    contract_last = (((1,), (1,)), ((), ()))

    def attend(nk):
        k = k_ref[0, :nk, :]
        v = v_ref[0, :nk, :]
        s1 = lax.dot_general(q1, k, contract_last, preferred_element_type=F32)
        s2 = lax.dot_general(q2, k, contract_last, preferred_element_type=F32)
        p1 = jnp.exp2(s1 - jnp.max(s1, axis=-1, keepdims=True))
        p2 = jnp.exp2(s2 - jnp.max(s2, axis=-1, keepdims=True))
        r1 = 1.0 / jnp.sum(p1, axis=-1, keepdims=True)
        r2 = lam / jnp.sum(p2, axis=-1, keepdims=True)
        o = _dot(p1.astype(BF16), v) * r1 - _dot(p2.astype(BF16), v) * r2
        ms = jnp.mean(o * o, axis=-1, keepdims=True)
        o = o * lax.rsqrt(ms + NORM_EPS) * g_ref[...] * (1.0 - lam_init)
        o_ref[...] = o.astype(BF16)

    @pl.when(qi == 0)
    def _():
        attend(n_ctx)

    @pl.when(qi > 0)
    def _():
        attend(k_ref.shape[1])


def _diff_attention(q, k, v, lq1, lk1, lq2, lk2, subln, lam_init, b, l, n_ctx):
    t = q.shape[0]
    ntb = l // TM
    hw = 2 * HEAD_DIM
    k3 = k.reshape(b, l, A_WIDTH)
    v3 = v.reshape(b, l, A_WIDTH)
    vec = lambda n: pl.BlockSpec((1, n), lambda bi, h, qi: (0, 0))
    qspec = pl.BlockSpec((TM, hw), lambda bi, h, qi: (bi * ntb + qi, h))
    kspec = pl.BlockSpec((1, l, hw), lambda bi, h, qi: (bi, 0, h))
    return pl.pallas_call(
        functools.partial(_attn_kernel, lam_init=lam_init, n_ctx=n_ctx),
        grid=(b, N_HEADS, ntb),
        in_specs=[vec(HEAD_DIM)] * 4 + [vec(hw), qspec, kspec, kspec],
        out_specs=qspec,
        out_shape=jax.ShapeDtypeStruct((t, A_WIDTH), BF16),
        compiler_params=_cparams(("parallel", "parallel", "parallel")),
        name="l0_diff_attention",
    )(lq1, lk1, lq2, lk2, subln, q, k3, v3)


def _route_tail(y, x_in, mod_ref, g_ref, wr_ref, br_ref,
                x1_ref, h2_ref, route_ref, cnt_ref, carry_ref):
    mod = mod_ref[0]
    x1 = x_in + mod[2:3] * y
    x1_ref[...] = x1
    h2 = _norm_mod(x1, g_ref[...], mod[3:4], mod[4:5])
    nch = h2.shape[1] // LANES
    for s in range(nch):
        h2_ref[pl.ds(s, TM, stride=nch), :] = h2[:, s * LANES:(s + 1) * LANES]

    wr = wr_ref[...]
    h_hi = h2.astype(BF16)
    h_lo = (h2 - h_hi.astype(F32)).astype(BF16)
    w_hi = wr.astype(BF16)
    w_lo = (wr - w_hi.astype(F32)).astype(BF16)
    logits = _dot(h_hi, w_hi) + _dot(h_hi, w_lo) + _dot(h_lo, w_hi) + br_ref[...]

    n_exp = logits.shape[1]
    lane = lax.broadcasted_iota(I32, logits.shape, 1).astype(F32)
    work = logits
    sels, vals, idxs = [], [], []
    for _ in range(TOP_K):
        m = jnp.max(work, axis=-1, keepdims=True)
        idx = jnp.min(jnp.where(work == m, lane, float(n_exp)), axis=-1, keepdims=True)
        sel = lane == idx
        sels.append(sel)
        vals.append(m)
        idxs.append(idx)
        work = jnp.where(sel, -jnp.inf, work)
    exps = [jnp.exp(vv - vals[0]) for vv in vals]
    inv_den = 1.0 / (exps[0] + exps[1] + exps[2] + exps[3])

    chosen = jnp.zeros(logits.shape, F32)
    for sel in sels:
        chosen = chosen + jnp.where(sel, 1.0, 0.0)
    r_i = lax.broadcasted_iota(I32, (TM, TM), 0)
    c_i = lax.broadcasted_iota(I32, (TM, TM), 1)
    earlier = jnp.where(c_i < r_i, 1.0, 0.0).astype(BF16)
    rank = _dot(earlier, chosen.astype(BF16)) + carry_ref[...]
    carry_ref[...] = carry_ref[...] + jnp.sum(chosen, axis=0, keepdims=True)
    cnt_ref[...] = carry_ref[...]

    out_lane = lax.broadcasted_iota(I32, (TM, LANES), 1)
    packed = jnp.zeros((TM, LANES), F32)
    for kk in range(TOP_K):
        rank_k = jnp.sum(jnp.where(sels[kk], rank, 0.0), axis=-1, keepdims=True)
        packed = jnp.where(out_lane == kk, idxs[kk], packed)
        packed = jnp.where(out_lane == TOP_K + kk, rank_k, packed)
        packed = jnp.where(out_lane == 2 * TOP_K + kk, exps[kk] * inv_den, packed)
    route_ref[...] = packed


def _out0_kernel(attn_ref, bg_ref, p_ref, pprev_ref, pnext_ref, cw_ref, wo_ref,
                 ctx_ref, x_ref, mod_ref, g_ref, wr_ref, br_ref,
                 x1_ref, h2_ref, route_ref, cnt_ref, carry_ref, *, ntb):
    i = pl.program_id(0)
    seg = i % ntb

    @pl.when(i == 0)
    def _():
        carry_ref[...] = jnp.zeros_like(carry_ref)

    p = p_ref[...].astype(F32)
    row = lax.broadcasted_iota(I32, p.shape, 0)
    has_prev = seg > 1
    has_next = jnp.logical_and(seg > 0, seg < ntb - 1)
    prev_row = jnp.where(has_prev, pprev_ref[SUBLANES - 1:SUBLANES, :].astype(F32), 0.0)
    next_row = jnp.where(has_next, pnext_ref[0:1, :].astype(F32), 0.0)
    before = jnp.where(row == 0, prev_row, pltpu.roll(p, 1, 0))
    after = jnp.where(row == TM - 1, next_row, pltpu.roll(p, TM - 1, 0))
    cw = cw_ref[...]
    conv = bg_ref[...].astype(F32) * (cw[0:1] * before + cw[1:2] * p + cw[2:3] * after)
    aw = attn_ref.shape[1]
    y = _dot(attn_ref[...], wo_ref[:aw, :]) + _dot(conv.astype(BF16), wo_ref[aw:, :])
    _route_tail(y, _stream_tile(ctx_ref, x_ref, ntb), mod_ref, g_ref, wr_ref, br_ref,
                x1_ref, h2_ref, route_ref, cnt_ref, carry_ref)


def _out1_kernel(hs_ref, gg_ref, wo_ref, x_ref, mod_ref, g_ref, wr_ref, br_ref,
                 x1_ref, h2_ref, route_ref, cnt_ref, carry_ref, *, ntb):
    i = pl.program_id(0)

    @pl.when(i == 0)
    def _():
        carry_ref[...] = jnp.zeros_like(carry_ref)

    @pl.when(i % ntb > 0)
    def _():
        rec = hs_ref[0].astype(F32) + hs_ref[1].astype(F32)
        y = _dot((rec * gg_ref[...].astype(F32)).astype(BF16), wo_ref[...])
        _route_tail(y, x_ref[...], mod_ref, g_ref, wr_ref, br_ref, x1_ref, h2_ref, route_ref, cnt_ref, carry_ref)


def _tail_specs(d, n_exp, ntb, latent_only):
    row = lambda i: (i, 0)
    const = lambda i: (0, 0)
    out_row = (lambda i: ((i // ntb) * (ntb - 1) + jnp.maximum(i % ntb - 1, 0), 0)) if latent_only else row
    in_specs = [
        pl.BlockSpec((TM, d), row),
        pl.BlockSpec((1, 6, d), lambda i: ((i // ntb) * 2 + jnp.minimum(i % ntb, 1), 0, 0)),
        pl.BlockSpec((1, d), const),
        pl.BlockSpec((d, n_exp), const),
        pl.BlockSpec((1, n_exp), const),
    ]
    out_specs = [
        pl.BlockSpec((TM, d), out_row),
        pl.BlockSpec((TM * (d // LANES), LANES), out_row),
        pl.BlockSpec((TM, LANES), out_row),
        pl.BlockSpec((1, n_exp), const),
    ]
    return in_specs, out_specs


def _tail_shapes(t, d, n_exp):
    return [jax.ShapeDtypeStruct((t, d), F32), jax.ShapeDtypeStruct((t * (d // LANES), LANES), F32),
            jax.ShapeDtypeStruct((t, LANES), F32), jax.ShapeDtypeStruct((1, n_exp), F32)]


def _out_proj0(attn, bg, p, conv_w, w_out, ctx, x, modv, g, w_r, b_r, ntb):
    d = x.shape[1]
    t = ctx.shape[0] + x.shape[0]
    n_exp = w_r.shape[1]
    bw = bg.shape[1]
    row = lambda i: (i, 0)
    const = lambda i: (0, 0)
    nblk = TM // SUBLANES
    tail_in, tail_out = _tail_specs(d, n_exp, ntb, False)
    return pl.pallas_call(
        functools.partial(_out0_kernel, ntb=ntb),
        grid=(t // TM,),
        in_specs=[
            pl.BlockSpec((TM, attn.shape[1]), row),
            pl.BlockSpec((TM, bw), row),
            pl.BlockSpec((TM, bw), row),
            pl.BlockSpec((SUBLANES, bw), lambda i: (jnp.maximum(i * nblk - 1, 0), 0)),
            pl.BlockSpec((SUBLANES, bw), lambda i: (jnp.minimum((i + 1) * nblk, t // SUBLANES - 1), 0)),
            pl.BlockSpec(conv_w.shape, const),
            pl.BlockSpec(w_out.shape, const),
        ] + _stream_specs(d, ntb) + tail_in[1:],
        out_specs=tail_out,
        out_shape=_tail_shapes(t, d, n_exp),
        scratch_shapes=[pltpu.VMEM((1, n_exp), F32)],
        compiler_params=_cparams(("arbitrary",)),
        name="l0_out_proj_router",
    )(attn, bg, p, p, p, conv_w, w_out, ctx, x, modv, g, w_r, b_r)


def _out_proj1(hs, gg, w_out, x, modv, g, w_r, b_r, ntb):
    t, d = x.shape
    n_exp = w_r.shape[1]
    row = lambda i: (i, 0)
    const = lambda i: (0, 0)
    tail_in, tail_out = _tail_specs(d, n_exp, ntb, True)
    t_lat = t // ntb * (ntb - 1)
    return pl.pallas_call(
        functools.partial(_out1_kernel, ntb=ntb),
        grid=(t // TM,),
        in_specs=[
            pl.BlockSpec((2, TM, hs.shape[2]), lambda i: (0, i, 0)),
            pl.BlockSpec((TM, gg.shape[1]), row),
            pl.BlockSpec(w_out.shape, const),
        ] + tail_in,
        out_specs=tail_out,
        out_shape=_tail_shapes(t_lat, d, n_exp),
        scratch_shapes=[pltpu.VMEM((1, n_exp), F32)],
        compiler_params=_cparams(("arbitrary",)),
        name="l1_out_proj_router",
    )(hs, gg, w_out, x, modv, g, w_r, b_r)


PAIRS_PER_TILE = TM * TOP_K


def _stage_pos(pos_ref, pbuf, psem):
    i = pl.program_id(0)
    slot = i % 2

    def chunk(ti, s):
        return pltpu.make_async_copy(pos_ref.at[pl.ds(ti * PAIRS_PER_TILE, PAIRS_PER_TILE)],
                                     pbuf.at[pl.ds(s * PAIRS_PER_TILE, PAIRS_PER_TILE)], psem.at[s])

    @pl.when(i == 0)
    def _():
        chunk(0, 0).start()

    chunk(i, slot).wait()

    @pl.when(i + 1 < pl.num_programs(0))
    def _():
        chunk(i + 1, 1 - slot).start()

    return slot * PAIRS_PER_TILE


def _dispatch_kernel(pad_lo_ref, pad_hi_ref, pos_ref, h2_ref, xs_ref, pbuf, zbuf, psem, dsem, zsem):
    i = pl.program_id(0)
    nch = h2_ref.shape[0] // TM
    n_tails = pad_lo_ref.shape[0] - 1

    def zero_slots(first, n):
        return pltpu.make_async_copy(zbuf.at[pl.ds(0, n * nch)],
                                     xs_ref.at[pl.ds(pl.multiple_of(first * nch, nch), n * nch)], zsem)

    def over_padding(act):
        def tail(e, carry):
            lo = pad_lo_ref[e]
            n = pad_hi_ref[e] - lo
            for bit in range(TME.bit_length() - 1):
                size = 1 << bit

                @pl.when((n & size) != 0)
                def _():
                    act(zero_slots(lo + (n & -(2 * size)), size))
            return carry

        lax.fori_loop(0, n_tails, tail, 0)

        def tile(j, c):
            act(zero_slots(j * TME, TME))
            return c

        lax.fori_loop(pad_lo_ref[n_tails] // TME, pad_hi_ref[n_tails] // TME, tile, 0)

    @pl.when(i == 0)
    def _():
        zbuf[...] = jnp.zeros(zbuf.shape, zbuf.dtype)
        over_padding(lambda cp: cp.start())
        over_padding(lambda cp: cp.wait())

    off = _stage_pos(pos_ref, pbuf, psem)
    for r in range(TM):
        for kk in range(TOP_K):
            dst = pl.multiple_of(pbuf[off + r * TOP_K + kk] * nch, nch)
            pltpu.make_async_copy(h2_ref.at[pl.ds(r * nch, nch)], xs_ref.at[pl.ds(dst, nch)], dsem).start(
                priority=kk % 2)
    for _ in range(TOP_K):
        pltpu.make_async_copy(h2_ref, xs_ref.at[pl.ds(0, TM * nch)], dsem).wait()


def _dispatch(pos, pad_lo, pad_hi, h2, n_slots, d):
    nch = d // LANES
    n_tok = h2.shape[0] // nch
    grid_spec = pltpu.PrefetchScalarGridSpec(
        num_scalar_prefetch=2,
        grid=(n_tok // TM,),
        in_specs=[pl.BlockSpec(memory_space=pl.ANY),
                  pl.BlockSpec((TM * nch, LANES), lambda i, lo, hi: (i, 0))],
        out_specs=pl.BlockSpec(memory_space=pl.ANY),
        scratch_shapes=[pltpu.SMEM((2 * PAIRS_PER_TILE,), I32), pltpu.VMEM((TME * nch, LANES), F32),
                        pltpu.SemaphoreType.DMA((2,)), pltpu.SemaphoreType.DMA, pltpu.SemaphoreType.DMA],
    )
    return pl.pallas_call(
        _dispatch_kernel,
        grid_spec=grid_spec,
        out_shape=jax.ShapeDtypeStruct((n_slots * nch, LANES), F32),
        compiler_params=_cparams(("arbitrary",)),
        name="expert_dispatch",
    )(pad_lo, pad_hi, pos, h2)


def _moe_kernel(te_ref, nx_ref, par_ref, meta_ref, x_ref, w1_hbm, b1_ref, w2_hbm, b2_ref, y_ref,
                w1f, w2f, w1b, w2b, wsem, *, layer):
    i = pl.program_id(0)
    n_tiles = meta_ref[0]
    d = w1b.shape[0]
    ff = w2b.shape[0]
    nch = d // LANES

    def weight_copies(e, s):
        return (pltpu.make_async_copy(w1_hbm.at[layer, e], w1f.at[s], wsem.at[0, s]),
                pltpu.make_async_copy(w2_hbm.at[layer, e], w2f.at[s], wsem.at[1, s]))

    @pl.when(i == 0)
    def _():
        for cp in weight_copies(te_ref[0], par_ref[0]):
            cp.start()

    @pl.when(i < n_tiles)
    def _():
        new_expert = jnp.logical_or(i == 0, te_ref[i] != te_ref[jnp.maximum(i - 1, 0)])

        @pl.when(new_expert)
        def _():
            s = par_ref[i]
            for cp in weight_copies(te_ref[i], s):
                cp.wait()
            w1b[...] = w1f[s].astype(BF16)
            w2b[...] = w2f[s].astype(BF16)

            @pl.when(nx_ref[i] >= 0)
            def _():
                for cp in weight_copies(nx_ref[i], 1 - s):
                    cp.start()

        x = jnp.concatenate([x_ref[pl.ds(c, TME, stride=nch), :] for c in range(nch)], axis=1)
        h = _dot(x.astype(BF16), w1b[...]) + b1_ref[0]
        glu = jnp.minimum(h[:, :ff], SWIGLU_LIMIT)
        lin = jnp.clip(h[:, ff:], -SWIGLU_LIMIT, SWIGLU_LIMIT)
        act = glu * _sigmoid(SWIGLU_ALPHA * glu) * (lin + 1.0)
        y = _dot(act.astype(BF16), w2b[...]) + b2_ref[0]
        for c in range(nch):
            y_ref[pl.ds(c, TME, stride=nch), :] = y[:, c * LANES:(c + 1) * LANES]

    @pl.when(i >= n_tiles)
    def _():
        y_ref[...] = jnp.zeros(y_ref.shape, y_ref.dtype)


def _routed_experts(tile_expert, next_expert, parity, meta, xs, w1, b1, w2, b2, layer):
    _, n_exp, d, ff2 = w1.shape
    ff = w2.shape[2]
    nch = d // LANES
    n_grid = tile_expert.shape[0]
    used = lambda i, mt: jnp.minimum(i, mt[0] - 1)
    bmap = lambda i, te, nx, pr, mt: (layer * n_exp + te[used(i, mt)], 0, 0)
    grid_spec = pltpu.PrefetchScalarGridSpec(
        num_scalar_prefetch=4,
        grid=(n_grid,),
        in_specs=[
            pl.BlockSpec((TME * nch, LANES), lambda i, te, nx, pr, mt: (used(i, mt), 0)),
            pl.BlockSpec(memory_space=pl.ANY),
            pl.BlockSpec((1, 1, ff2), bmap),
            pl.BlockSpec(memory_space=pl.ANY),
            pl.BlockSpec((1, 1, d), bmap),
        ],
        out_specs=pl.BlockSpec((TME * nch, LANES), lambda i, te, nx, pr, mt: (i, 0)),
        scratch_shapes=[pltpu.VMEM((2, d, ff2), F32), pltpu.VMEM((2, ff, d), F32),
                        pltpu.VMEM((d, ff2), BF16), pltpu.VMEM((ff, d), BF16), pltpu.SemaphoreType.DMA((2, 2))],
    )
    return pl.pallas_call(
        functools.partial(_moe_kernel, layer=layer),
        grid_spec=grid_spec,
        out_shape=jax.ShapeDtypeStruct(xs.shape, F32),
        compiler_params=_cparams(("arbitrary",)),
        name="routed_experts",
    )(tile_expert, next_expert, parity, meta, xs, w1, b1.reshape(-1, 1, ff2), w2, b2.reshape(-1, 1, d))


def _gather_pipeline(pos_ref, ys_ref, ybufs, pbufs, psem, csem, compute):
    i = pl.program_id(0)
    n = pl.num_programs(0)
    nch = ybufs[0].shape[1] // TM

    def chunk(ti, s):
        return pltpu.make_async_copy(pos_ref.at[pl.ds(ti * PAIRS_PER_TILE, PAIRS_PER_TILE)], pbufs[s], psem.at[s])

    def row_copy(s, r, kk):
        src = pl.multiple_of(pbufs[s][r * TOP_K + kk] * nch, nch)
        return pltpu.make_async_copy(ys_ref.at[pl.ds(src, nch)], ybufs[s].at[kk, pl.ds(r * nch, nch)], csem.at[s])

    def wait_rows(s):
        for kk in range(TOP_K):
            pltpu.make_async_copy(ys_ref.at[pl.ds(0, TM * nch)], ybufs[s].at[kk], csem.at[s]).wait()

    @pl.when(i == 0)
    def _():
        chunk(0, 0).start()
        chunk(0, 0).wait()

        def first_rows(r, c):
            for kk in range(TOP_K):
                row_copy(0, r, kk).start(priority=kk % 2)
            return c

        lax.fori_loop(0, TM, first_rows, 0)

        @pl.when(n > 1)
        def _():
            chunk(1, 1).start()

    for s in range(2):
        mine = i % 2 == s

        @pl.when(jnp.logical_and(mine, i + 2 < n))
        def _(s=s):
            chunk(i + 2, s).start()

        @pl.when(jnp.logical_and(mine, i + 1 < n))
        def _(s=s):
            wait_rows(s)
            chunk(i + 1, 1 - s).wait()
            for r in range(TM):
                for kk in range(TOP_K):
                    row_copy(1 - s, r, kk).start(priority=kk % 2)
            compute(ybufs[s])

        @pl.when(jnp.logical_and(mine, i + 1 >= n))
        def _(s=s):
            wait_rows(s)
            compute(ybufs[s])


def _gather_scratch(d):
    nch = d // LANES
    return [pltpu.VMEM((TOP_K, TM * nch, LANES), F32), pltpu.VMEM((TOP_K, TM * nch, LANES), F32),
            pltpu.SMEM((PAIRS_PER_TILE,), I32), pltpu.SMEM((PAIRS_PER_TILE,), I32),
            pltpu.SemaphoreType.DMA((2,)), pltpu.SemaphoreType.DMA((2,))]


def _plan_routes(route, counts, n_grid):
    n_exp = counts.shape[-1]
    cnt = counts.reshape(n_exp).astype(I32)
    tiles_per = (cnt + TME - 1) // TME
    tile_end = jnp.cumsum(tiles_per)
    offset = (tile_end - tiles_per) * TME
    n_tiles = tile_end[-1]
    tile_ids = jnp.minimum(jnp.arange(n_grid, dtype=I32), n_tiles - 1)
    tile_expert = jnp.sum((tile_ids[:, None] >= tile_end[None, :]).astype(I32), axis=1)
    expert_ids = jnp.arange(n_exp, dtype=I32)
    of_tile = tile_expert[:, None] == expert_ids
    later = jnp.logical_and(expert_ids[None, :] > expert_ids[:, None], (tiles_per > 0)[None, :])
    next_of = jnp.min(jnp.where(later, expert_ids[None, :], n_exp), axis=1)
    next_of = jnp.where(next_of == n_exp, -1, next_of)
    next_expert = jnp.sum(jnp.where(of_tile, next_of, 0), axis=1)
    parity = jnp.sum(jnp.where(of_tile, jnp.cumsum((tiles_per > 0).astype(I32)), 0), axis=1) % 2
    eidx = route[:, :TOP_K].astype(I32)
    rank = route[:, TOP_K:2 * TOP_K].astype(I32)
    pos = jnp.sum(jnp.where(eidx[..., None] == expert_ids, offset, 0), axis=-1) + rank
    pad_lo = jnp.concatenate([offset + cnt, (n_tiles * TME).reshape(1)])
    pad_hi = jnp.concatenate([tile_end * TME, jnp.full((1,), n_grid * TME, I32)])
    return (tile_expert, next_expert, parity, n_tiles.reshape(1)), pos.reshape(-1), pad_lo, pad_hi


def _moe_layer(route, counts, h2, w1, b1, w2, b2, layer):
    n_exp, d = w1.shape[1], w1.shape[2]
    n_tok = route.shape[0]
    n_grid = n_tok * TOP_K // TME + n_exp
    tiles, pos, pad_lo, pad_hi = _plan_routes(route, counts, n_grid)
    xs = _dispatch(pos, pad_lo, pad_hi, h2, n_grid * TME, d)
    return pos, _routed_experts(*tiles, xs, w1, b1, w2, b2, layer)


def _combine(ybuf, route_ref, d):
    nch = d // LANES
    gates = route_ref[...]
    gk = [jnp.broadcast_to(gates[:, 2 * TOP_K + kk:2 * TOP_K + kk + 1], (TM, LANES)) for kk in range(TOP_K)]
    chunks = []
    for c in range(nch):
        acc = gk[0] * ybuf[0, pl.ds(c, TM, stride=nch), :]
        for kk in range(1, TOP_K):
            acc = acc + gk[kk] * ybuf[kk, pl.ds(c, TM, stride=nch), :]
        chunks.append(acc)
    return jnp.concatenate(chunks, axis=1)


def _in1_kernel(x_ref, pos_ref, ys_ref, route_ref, mod0_ref, mod1_ref, g_ref, w_ref, x2_ref, gg_ref, u_ref,
                ybuf0, ybuf1, pbuf0, pbuf1, psem, csem):
    def compute(ybuf):
        x2 = x_ref[...] + mod0_ref[0][5:6] * _combine(ybuf, route_ref, x_ref.shape[1])
        x2_ref[...] = x2
        mod1 = mod1_ref[0]
        h = _norm_mod(x2, g_ref[...], mod1[0:1], mod1[1:2])
        y = _dot(h.astype(BF16), w_ref[...])
        half = y.shape[1] // 2
        gg_ref[...] = jax.nn.gelu(y[:, :half]).astype(BF16)
        u_ref[...] = y[:, half:]

    _gather_pipeline(pos_ref, ys_ref, (ybuf0, ybuf1), (pbuf0, pbuf1), psem, csem, compute)


def _in_proj1(x, pos, ys, route, modv0, modv1, g, w, ntb):
    t, d = x.shape
    n = w.shape[1]
    row = lambda i: (i, 0)
    mod_map = lambda i: ((i // ntb) * 2 + jnp.minimum(i % ntb, 1), 0, 0)
    return pl.pallas_call(
        _in1_kernel,
        grid=(t // TM,),
        in_specs=[
            pl.BlockSpec((TM, d), row),
            pl.BlockSpec(memory_space=pl.ANY),
            pl.BlockSpec(memory_space=pl.ANY),
            pl.BlockSpec((TM, LANES), row),
            pl.BlockSpec((1, 6, d), mod_map),
            pl.BlockSpec((1, 6, d), mod_map),
            pl.BlockSpec((1, d), lambda i: (0, 0)),
            pl.BlockSpec((d, n), lambda i: (0, 0)),
        ],
        out_specs=[pl.BlockSpec((TM, d), row), pl.BlockSpec((TM, n // 2), row), pl.BlockSpec((TM, n // 2), row)],
        out_shape=[jax.ShapeDtypeStruct((t, d), F32), jax.ShapeDtypeStruct((t, n // 2), BF16),
                   jax.ShapeDtypeStruct((t, n // 2), F32)],
        scratch_shapes=_gather_scratch(d),
        compiler_params=_cparams(("arbitrary",)),
        name="l1_combine_in_proj",
    )(x, pos, ys, route, modv0, modv1, g, w)


def _rglru_kernel(u_ref, halo_ref, cw_ref, cb_ref, gaw_ref, gab_ref, gxw_ref, gxb_ref, lam_ref, o_ref,
                  ext, a_s, b_s, h_s, state, *, ntb):
    d = pl.program_id(0)
    s = pl.program_id(2)
    nb, tc, cw_ = u_ref.shape
    u = u_ref[...]
    cw = cw_ref[0]

    def finish(xc, reverse):
        xc2 = xc.reshape(nb * tc, cw_) + cb_ref[0]
        xb = xc2.astype(BF16)
        t_r = jnp.tanh(_dot(xb, gaw_ref[0, 0]) + gab_ref[0])
        t_i = jnp.tanh(_dot(xb, gxw_ref[0, 0]) + gxb_ref[0])
        nl = -lam_ref[0]
        softplus = jnp.maximum(nl, 0.0) + jnp.log1p(jnp.exp(-jnp.abs(nl)))
        half_rate = (-0.5 * LRU_C) * softplus
        log_a = half_rate * t_r + half_rate
        a = jnp.exp(log_a)
        half_x = 0.5 * xc2
        bb = jnp.sqrt(1.0 - a * a) * (half_x * t_i + half_x)
        n_lane = cw_ // LANES
        pitch = tc + SCAN_PAD
        for c in range(n_lane):
            for bi in range(nb):
                a_s[c, bi * pitch:bi * pitch + tc, :] = a[bi * tc:(bi + 1) * tc, c * LANES:(c + 1) * LANES]
                b_s[c, bi * pitch:bi * pitch + tc, :] = bb[bi * tc:(bi + 1) * tc, c * LANES:(c + 1) * LANES]

        @pl.when(s == 0)
        def _():
            state[...] = jnp.zeros_like(state)

        def steps(tb, hs):
            hs = list(hs)
            for uu in range(SCAN_UNROLL):
                tt = tb * SCAN_UNROLL + uu
                t = (tc - 1 - tt) if reverse else tt
                rows = pl.ds(t, nb, stride=pitch)
                for c in range(n_lane):
                    hs[c] = a_s[c, rows, :] * hs[c] + b_s[c, rows, :]
                    h_s[c, rows, :] = hs[c]
            return tuple(hs)

        hs = lax.fori_loop(0, tc // SCAN_UNROLL, steps, tuple(state[c] for c in range(n_lane)))
        for c in range(n_lane):
            state[c] = hs[c]
            for bi in range(nb):
                o_ref[0, bi, :, c * LANES:(c + 1) * LANES] = h_s[c, bi * pitch:bi * pitch + tc, :].astype(o_ref.dtype)

    @pl.when(d == 0)
    def _():
        chunk = s
        keep = chunk > 1
        ext[:, 0:SUBLANES, :] = jnp.where(keep, halo_ref[...], 0.0)
        ext[:, SUBLANES:, :] = u
        xc = cw[3:4] * u
        for j in range(1, C_CONV):
            xc = xc + cw[3 - j:4 - j] * ext[:, SUBLANES - j:SUBLANES - j + tc, :]
        finish(xc, False)

    @pl.when(d == 1)
    def _():
        chunk = jnp.where(s == 0, 0, ntb - s)
        keep = jnp.logical_and(chunk > 0, chunk < ntb - 1)
        ext[:, 0:tc, :] = u
        ext[:, tc:, :] = jnp.where(keep, halo_ref[...], 0.0)
        xc = cw[0:1] * u
        for j in range(1, C_CONV):
            xc = xc + cw[j:j + 1] * ext[:, j:j + tc, :]
        finish(xc, True)


def _rglru(u, conv_w, conv_b, ga_w, ga_b, gx_w, gx_b, lam, b, l):
    width = u.shape[1]
    cb = width // C_BLOCKS
    ntb = l // TM
    u3 = u.reshape(b, l, width)
    nblk = TM // SUBLANES

    def chunk_of(d, s):
        return jnp.where(d == 0, s, jnp.where(s == 0, 0, ntb - s))

    def halo_of(d, s):
        c = chunk_of(d, s)
        return jnp.where(d == 0, jnp.maximum(c * nblk - 1, 0), jnp.minimum((c + 1) * nblk, l // SUBLANES - 1))

    vec = pl.BlockSpec((1, 1, cb), lambda d, g, s: (d, 0, g))
    mat = pl.BlockSpec((1, 1, cb, cb), lambda d, g, s: (d, g, 0, 0))
    return pl.pallas_call(
        functools.partial(_rglru_kernel, ntb=ntb),
        grid=(2, C_BLOCKS, ntb),
        in_specs=[
            pl.BlockSpec((b, TM, cb), lambda d, g, s: (0, chunk_of(d, s), g)),
            pl.BlockSpec((b, SUBLANES, cb), lambda d, g, s: (0, halo_of(d, s), g)),
            pl.BlockSpec((1, C_CONV, cb), lambda d, g, s: (d, 0, g)),
            vec, mat, vec, mat, vec, vec,
        ],
        out_specs=pl.BlockSpec((1, b, TM, cb), lambda d, g, s: (d, 0, chunk_of(d, s), g)),
        out_shape=jax.ShapeDtypeStruct((2, b, l, width), BF16),
        scratch_shapes=[
            pltpu.VMEM((b, TM + SUBLANES, cb), F32),
            pltpu.VMEM((cb // LANES, b * (TM + SCAN_PAD), LANES), F32),
            pltpu.VMEM((cb // LANES, b * (TM + SCAN_PAD), LANES), F32),
            pltpu.VMEM((cb // LANES, b * (TM + SCAN_PAD), LANES), F32),
            pltpu.VMEM((cb // LANES, b, LANES), F32),
        ],
        compiler_params=_cparams(("arbitrary", "arbitrary", "arbitrary")),
        name="l1_rglru",
    )(u3, u3, conv_w, conv_b.reshape(2, 1, width), (0.5 * ga_w).astype(BF16), (0.5 * ga_b).reshape(2, 1, width),
      (0.5 * gx_w).astype(BF16), (0.5 * gx_b).reshape(2, 1, width), lam.reshape(2, 1, width))


def _final_kernel(x_ref, pos_ref, ys_ref, route_ref, mod_ref, g_ref, o_ref,
                  ybuf0, ybuf1, pbuf0, pbuf1, psem, csem):
    def compute(ybuf):
        x = x_ref[...] + mod_ref[0][5:6] * _combine(ybuf, route_ref, x_ref.shape[1])
        ms = jnp.mean(x * x, axis=-1, keepdims=True)
        o_ref[...] = x * lax.rsqrt(ms + NORM_EPS) * g_ref[...]

    _gather_pipeline(pos_ref, ys_ref, (ybuf0, ybuf1), (pbuf0, pbuf1), psem, csem, compute)


def _final(x, pos, ys, route, modv, g, b, s_len):
    t, d = x.shape
    per_b = s_len // TM
    row = lambda i: (i, 0)
    return pl.pallas_call(
        _final_kernel,
        grid=(t // TM,),
        in_specs=[
            pl.BlockSpec((TM, d), row),
            pl.BlockSpec(memory_space=pl.ANY),
            pl.BlockSpec(memory_space=pl.ANY),
            pl.BlockSpec((TM, LANES), row),
            pl.BlockSpec((1, 6, d), lambda i: ((i // per_b) * 2 + 1, 0, 0)),
            pl.BlockSpec((1, d), lambda i: (0, 0)),
        ],
        out_specs=pl.BlockSpec((TM, d), row),
        out_shape=jax.ShapeDtypeStruct((t, d), F32),
        scratch_shapes=_gather_scratch(d),
        compiler_params=_cparams(("arbitrary",)),
        name="final_combine_norm",
    )(x, pos, ys, route, modv, g)


def _rope_tables(s_len, n_ctx):
    n_rows = s_len // GRID_W
    rows, cols = jnp.meshgrid(jnp.arange(n_rows), jnp.arange(GRID_W), indexing="ij")
    pos = jnp.stack([rows.reshape(-1), cols.reshape(-1)], axis=-1).astype(F32)
    n_freq = HEAD_DIM // 4
    inv = ROPE_THETA ** (-jnp.arange(n_freq, dtype=F32) / n_freq)
    ang = pos[:, :, None] * inv
    cos, sin = jnp.cos(ang), jnp.sin(ang)
    cos64 = jnp.stack([cos, cos], axis=2).reshape(s_len, HEAD_DIM)
    sin64 = jnp.stack([-sin, sin], axis=2).reshape(s_len, HEAD_DIM)
    cos_l = jnp.tile(cos64, (1, LANES // HEAD_DIM))
    sin_l = jnp.tile(sin64, (1, LANES // HEAD_DIM))
    cos_t = jnp.concatenate([jnp.ones((n_ctx, LANES), F32), cos_l], axis=0)
    sin_t = jnp.concatenate([jnp.zeros((n_ctx, LANES), F32), sin_l], axis=0)
    return cos_t, sin_t


def kernel(x, c, ctx, c_ctx, w_mod, b_mod, norm_mix, norm_ffn, ev_w_in, ev_w_out, ev_lambda_q1, ev_lambda_k1, ev_lambda_q2, ev_lambda_k2, ev_subln, ev_conv_w, od_w_in, od_w_out, od_conv_w, od_conv_b, od_gate_a_w, od_gate_a_b, od_gate_x_w, od_gate_x_b, od_lru_lambda, moe_w_router, moe_b_router, moe_w1, moe_b1, moe_w2, moe_b2, final_norm):
    b, s_len, d = x.shape
    n_ctx = ctx.shape[1]
    l = n_ctx + s_len
    t = b * l
    ntb = l // TM
    assert n_ctx == TM and s_len % TM == 0 and w_mod.shape[0] == 2

    ctx2 = ctx.reshape(b * n_ctx, d)
    x2d = x.reshape(b * s_len, d)

    n_rows = -(-(b + 1) // SUBLANES) * SUBLANES
    cs = jnp.concatenate([c, c_ctx[None, :], jnp.zeros((n_rows - b - 1, d), F32)], axis=0)
    mod = _modulation(cs, w_mod, b_mod)

    def mod_table(i):
        lat = mod[i, :b]
        cx = jnp.broadcast_to(mod[i, b][None, :], lat.shape)
        return jnp.stack([cx, lat], axis=1).reshape(b * 2, 6, d)

    modv0, modv1 = mod_table(0), mod_table(1)

    cos_t, sin_t = _rope_tables(s_len, n_ctx)
    q, k, v, bg, p = _in_proj0(ctx2, x2d, modv0, norm_mix[0:1], ev_w_in[0].astype(BF16), cos_t, sin_t, ntb)
    lam_init = 0.8 - 0.6 * math.exp(-0.3 * 0)
    attn = _diff_attention(q, k, v, ev_lambda_q1[0:1], ev_lambda_k1[0:1], ev_lambda_q2[0:1], ev_lambda_k2[0:1],
                           ev_subln[0:1], lam_init, b, l, n_ctx)
    x1, h2, route0, counts0 = _out_proj0(attn, bg, p, ev_conv_w[0], ev_w_out[0].astype(BF16), ctx2, x2d, modv0,
                                         norm_ffn[0:1], moe_w_router[0], moe_b_router[0:1], ntb)
    pos0, ys0 = _moe_layer(route0, counts0, h2, moe_w1, moe_b1, moe_w2, moe_b2, 0)

    x2, gg, u = _in_proj1(x1, pos0, ys0, route0, modv0, modv1, norm_mix[1:2], od_w_in[0].astype(BF16), ntb)
    hs = _rglru(u, od_conv_w[0], od_conv_b[0], od_gate_a_w[0], od_gate_a_b[0], od_gate_x_w[0], od_gate_x_b[0],
                od_lru_lambda[0], b, l)
    x3, h3, route1, counts1 = _out_proj1(hs.reshape(2, t, hs.shape[-1]), gg, od_w_out[0].astype(BF16), x2, modv1,
                                         norm_ffn[1:2], moe_w_router[1], moe_b_router[1:2], ntb)
    pos1, ys1 = _moe_layer(route1, counts1, h3, moe_w1, moe_b1, moe_w2, moe_b2, 1)

    out = _final(x3, pos1, ys1, route1, modv1, final_norm[None, :], b, s_len)
    return out.reshape(b, s_len, d)
```

```python
import functools
import math

import jax
import jax.numpy as jnp
from jax import lax
from jax.experimental import pallas as pl
from jax.experimental.pallas import tpu as pltpu

F32 = jnp.float32
BF16 = jnp.bfloat16
I32 = jnp.int32

NORM_EPS = 1e-6
ROPE_THETA = 10000.0
GRID_W = 64
N_HEADS = 4
HEAD_DIM = 64
A_WIDTH = 2 * N_HEADS * HEAD_DIM
B_CONV = 3
C_CONV = 4
C_BLOCKS = 4
LRU_C = 8.0
TOP_K = 4
SWIGLU_ALPHA = 1.702
SWIGLU_LIMIT = 7.0

LANES = 128
SUBLANES = 8
TM = 256
TME = 256
VMEM_LIMIT = 56 * 1024 * 1024
SCAN_PAD = 8
SCAN_UNROLL = 8
LOG2_E = 1.4426950408889634


def _cparams(sem, vmem=VMEM_LIMIT):
    return pltpu.CompilerParams(dimension_semantics=sem, vmem_limit_bytes=vmem)


def _norm_mod(x, g, shift, scale):
    ms = jnp.mean(x * x, axis=-1, keepdims=True)
    return (x * lax.rsqrt(ms + NORM_EPS) * g) * (1.0 + scale) + shift


def _dot(a, b):
    return jnp.dot(a, b, preferred_element_type=F32)


def _sigmoid(x):
    return 0.5 * jnp.tanh(0.5 * x) + 0.5


def _mod_kernel(cs_ref, w_ref, b_ref, o_ref):
    s = cs_ref[...]
    s = s * jax.nn.sigmoid(s)
    o_ref[0] = _dot(s.astype(BF16), w_ref[0].astype(BF16)) + b_ref[0]


def _modulation(cs, w_mod, b_mod):
    depth, d, n = w_mod.shape
    rows = cs.shape[0]
    tn = 1536
    return pl.pallas_call(
        _mod_kernel,
        grid=(depth, n // tn),
        in_specs=[
            pl.BlockSpec((rows, d), lambda i, j: (0, 0)),
            pl.BlockSpec((1, d, tn), lambda i, j: (i, 0, j)),
            pl.BlockSpec((1, 1, tn), lambda i, j: (i, 0, j)),
        ],
        out_specs=pl.BlockSpec((1, rows, tn), lambda i, j: (i, 0, j)),
        out_shape=jax.ShapeDtypeStruct((depth, rows, n), F32),
        compiler_params=_cparams(("parallel", "parallel")),
        name="adaln_modulation",
    )(cs, w_mod, b_mod.reshape(depth, 1, n))


def _stream_tile(ctx_ref, x_ref, ntb):
    return jnp.where(pl.program_id(0) % ntb == 0, ctx_ref[...], x_ref[...])


def _stream_specs(d, ntb):
    return [pl.BlockSpec((TM, d), lambda i: (i // ntb, 0)),
            pl.BlockSpec((TM, d), lambda i: ((i // ntb) * (ntb - 1) + jnp.maximum(i % ntb - 1, 0), 0))]


def _in0_kernel(ctx_ref, x_ref, mod_ref, g_ref, w_ref, cos_ref, sin_ref,
                q_ref, k_ref, v_ref, bg_ref, p_ref, *, ntb):
    mod = mod_ref[0]
    h = _norm_mod(_stream_tile(ctx_ref, x_ref, ntb), g_ref[...], mod[0:1], mod[1:2])
    y = _dot(h.astype(BF16), w_ref[...])
    cosv = cos_ref[...]
    sinv = sin_ref[...]
    lane = lax.broadcasted_iota(I32, (TM, LANES), 1)
    first_half = (lane & 16) == 0

    def rope(z):
        outs = []
        for g in range(A_WIDTH // LANES):
            zg = z[:, g * LANES:(g + 1) * LANES]
            partner = jnp.where(first_half, pltpu.roll(zg, LANES - 16, 1), pltpu.roll(zg, 16, 1))
            outs.append(zg * cosv + partner * sinv)
        return jnp.concatenate(outs, axis=1)

    aw = A_WIDTH
    q_ref[...] = (rope(y[:, :aw]) * (HEAD_DIM ** -0.5 * LOG2_E)).astype(BF16)
    k_ref[...] = rope(y[:, aw:2 * aw]).astype(BF16)
    v_ref[...] = y[:, 2 * aw:3 * aw].astype(BF16)
    bw = (y.shape[1] - 3 * aw) // 3
    bg_ref[...] = y[:, 3 * aw:3 * aw + bw].astype(BF16)
    p_ref[...] = (y[:, 3 * aw + bw:3 * aw + 2 * bw] * y[:, 3 * aw + 2 * bw:]).astype(BF16)


def _in_proj0(ctx, x, modv, g, w, cos_t, sin_t, ntb):
    d = x.shape[1]
    t = ctx.shape[0] + x.shape[0]
    n = w.shape[1]
    bw = (n - 3 * A_WIDTH) // 3
    row = lambda i: (i, 0)
    return pl.pallas_call(
        functools.partial(_in0_kernel, ntb=ntb),
        grid=(t // TM,),
        in_specs=_stream_specs(d, ntb) + [
            pl.BlockSpec((1, 6, d), lambda i: ((i // ntb) * 2 + jnp.minimum(i % ntb, 1), 0, 0)),
            pl.BlockSpec((1, d), lambda i: (0, 0)),
            pl.BlockSpec((d, n), lambda i: (0, 0)),
            pl.BlockSpec((TM, LANES), lambda i: (i % ntb, 0)),
            pl.BlockSpec((TM, LANES), lambda i: (i % ntb, 0)),
        ],
        out_specs=[pl.BlockSpec((TM, A_WIDTH), row)] * 3 + [pl.BlockSpec((TM, bw), row)] * 2,
        out_shape=[jax.ShapeDtypeStruct((t, A_WIDTH), BF16)] * 3 + [jax.ShapeDtypeStruct((t, bw), BF16)] * 2,
        compiler_params=_cparams(("parallel",)),
        name="l0_in_proj_rope",
    )(ctx, x, modv, g, w, cos_t, sin_t)


def _attn_kernel(lq1_ref, lk1_ref, lq2_ref, lk2_ref, g_ref, q_ref, k_ref, v_ref, o_ref, *, lam_init, n_ctx):
    qi = pl.program_id(2)
    lam = (jnp.exp(jnp.sum(lq1_ref[...] * lk1_ref[...], axis=-1, keepdims=True))
           - jnp.exp(jnp.sum(lq2_ref[...] * lk2_ref[...], axis=-1, keepdims=True)) + lam_init)
    q = q_ref[...]
    lane = lax.broadcasted_iota(I32, q.shape, 1)
    zero = jnp.zeros_like(q)
    q1 = jnp.where(lane < HEAD_DIM, q, zero)
    q2 = jnp.where(lane < HEAD_DIM, zero, q)
    contract_last = (((1,), (1,)), ((), ()))

    def attend(nk):
        k = k_ref[0, :nk, :]
        v = v_ref[0, :nk, :]
        s1 = lax.dot_general(q1, k, contract_last, preferred_element_type=F32)
        s2 = lax.dot_general(q2, k, contract_last, preferred_element_type=F32)
        p1 = jnp.exp2(s1 - jnp.max(s1, axis=-1, keepdims=True))
        p2 = jnp.exp2(s2 - jnp.max(s2, axis=-1, keepdims=True))
        r1 = 1.0 / jnp.sum(p1, axis=-1, keepdims=True)
        r2 = lam / jnp.sum(p2, axis=-1, keepdims=True)
        o = _dot(p1.astype(BF16), v) * r1 - _dot(p2.astype(BF16), v) * r2
        ms = jnp.mean(o * o, axis=-1, keepdims=True)
        o = o * lax.rsqrt(ms + NORM_EPS) * g_ref[...] * (1.0 - lam_init)
        o_ref[...] = o.astype(BF16)

    @pl.when(qi == 0)
    def _():
        attend(n_ctx)

    @pl.when(qi > 0)
    def _():
        attend(k_ref.shape[1])


def _diff_attention(q, k, v, lq1, lk1, lq2, lk2, subln, lam_init, b, l, n_ctx):
    t = q.shape[0]
    ntb = l // TM
    hw = 2 * HEAD_DIM
    k3 = k.reshape(b, l, A_WIDTH)
    v3 = v.reshape(b, l, A_WIDTH)
    vec = lambda n: pl.BlockSpec((1, n), lambda bi, h, qi: (0, 0))
    qspec = pl.BlockSpec((TM, hw), lambda bi, h, qi: (bi * ntb + qi, h))
    kspec = pl.BlockSpec((1, l, hw), lambda bi, h, qi: (bi, 0, h))
    return pl.pallas_call(
        functools.partial(_attn_kernel, lam_init=lam_init, n_ctx=n_ctx),
        grid=(b, N_HEADS, ntb),
        in_specs=[vec(HEAD_DIM)] * 4 + [vec(hw), qspec, kspec, kspec],
        out_specs=qspec,
        out_shape=jax.ShapeDtypeStruct((t, A_WIDTH), BF16),
        compiler_params=_cparams(("parallel", "parallel", "parallel")),
        name="l0_diff_attention",
    )(lq1, lk1, lq2, lk2, subln, q, k3, v3)


def _route_tail(y, x_in, mod_ref, g_ref, wr_ref, br_ref,
                x1_ref, h2_ref, route_ref, cnt_ref, carry_ref):
    mod = mod_ref[0]
    x1 = x_in + mod[2:3] * y
    x1_ref[...] = x1
    h2 = _norm_mod(x1, g_ref[...], mod[3:4], mod[4:5])
    nch = h2.shape[1] // LANES
    for s in range(nch):
        h2_ref[pl.ds(s, TM, stride=nch), :] = h2[:, s * LANES:(s + 1) * LANES]

    logits = _dot(h2.astype(BF16), wr_ref[...].astype(BF16)) + br_ref[...]

    n_exp = logits.shape[1]
    lane = lax.broadcasted_iota(I32, logits.shape, 1).astype(F32)
    work = logits
    sels, vals, idxs = [], [], []
    for _ in range(TOP_K):
        m = jnp.max(work, axis=-1, keepdims=True)
        idx = jnp.min(jnp.where(work == m, lane, float(n_exp)), axis=-1, keepdims=True)
        sel = lane == idx
        sels.append(sel)
        vals.append(m)
        idxs.append(idx)
        work = jnp.where(sel, -jnp.inf, work)
    exps = [jnp.exp(vv - vals[0]) for vv in vals]
    inv_den = 1.0 / (exps[0] + exps[1] + exps[2] + exps[3])

    chosen = jnp.zeros(logits.shape, F32)
    for sel in sels:
        chosen = chosen + jnp.where(sel, 1.0, 0.0)
    r_i = lax.broadcasted_iota(I32, (TM, TM), 0)
    c_i = lax.broadcasted_iota(I32, (TM, TM), 1)
    earlier = jnp.where(c_i < r_i, 1.0, 0.0).astype(BF16)
    rank = _dot(earlier, chosen.astype(BF16)) + carry_ref[...]
    carry_ref[...] = carry_ref[...] + jnp.sum(chosen, axis=0, keepdims=True)
    cnt_ref[...] = carry_ref[...]

    out_lane = lax.broadcasted_iota(I32, (TM, LANES), 1)
    packed = jnp.zeros((TM, LANES), F32)
    for kk in range(TOP_K):
        rank_k = jnp.sum(jnp.where(sels[kk], rank, 0.0), axis=-1, keepdims=True)
        packed = jnp.where(out_lane == kk, idxs[kk], packed)
        packed = jnp.where(out_lane == TOP_K + kk, rank_k, packed)
        packed = jnp.where(out_lane == 2 * TOP_K + kk, exps[kk] * inv_den, packed)
    route_ref[...] = packed


def _out0_kernel(attn_ref, bg_ref, p_ref, pprev_ref, pnext_ref, cw_ref, wo_ref,
                 ctx_ref, x_ref, mod_ref, g_ref, wr_ref, br_ref,
                 x1_ref, h2_ref, route_ref, cnt_ref, carry_ref, *, ntb):
    i = pl.program_id(0)
    seg = i % ntb

    @pl.when(i == 0)
    def _():
        carry_ref[...] = jnp.zeros_like(carry_ref)

    p = p_ref[...].astype(F32)
    row = lax.broadcasted_iota(I32, p.shape, 0)
    has_prev = seg > 1
    has_next = jnp.logical_and(seg > 0, seg < ntb - 1)
    prev_row = jnp.where(has_prev, pprev_ref[SUBLANES - 1:SUBLANES, :].astype(F32), 0.0)
    next_row = jnp.where(has_next, pnext_ref[0:1, :].astype(F32), 0.0)
    before = jnp.where(row == 0, prev_row, pltpu.roll(p, 1, 0))
    after = jnp.where(row == TM - 1, next_row, pltpu.roll(p, TM - 1, 0))
    cw = cw_ref[...]
    conv = bg_ref[...].astype(F32) * (cw[0:1] * before + cw[1:2] * p + cw[2:3] * after)
    aw = attn_ref.shape[1]
    y = _dot(attn_ref[...], wo_ref[:aw, :]) + _dot(conv.astype(BF16), wo_ref[aw:, :])
    _route_tail(y, _stream_tile(ctx_ref, x_ref, ntb), mod_ref, g_ref, wr_ref, br_ref,
                x1_ref, h2_ref, route_ref, cnt_ref, carry_ref)


def _out1_kernel(hs_ref, gg_ref, wo_ref, x_ref, mod_ref, g_ref, wr_ref, br_ref,
                 x1_ref, h2_ref, route_ref, cnt_ref, carry_ref, *, ntb):
    i = pl.program_id(0)

    @pl.when(i == 0)
    def _():
        carry_ref[...] = jnp.zeros_like(carry_ref)

    @pl.when(i % ntb > 0)
    def _():
        rec = hs_ref[0].astype(F32) + hs_ref[1].astype(F32)
        y = _dot((rec * gg_ref[...].astype(F32)).astype(BF16), wo_ref[...])
        _route_tail(y, x_ref[...], mod_ref, g_ref, wr_ref, br_ref, x1_ref, h2_ref, route_ref, cnt_ref, carry_ref)


def _tail_specs(d, n_exp, ntb, latent_only):
    row = lambda i: (i, 0)
    const = lambda i: (0, 0)
    out_row = (lambda i: ((i // ntb) * (ntb - 1) + jnp.maximum(i % ntb - 1, 0), 0)) if latent_only else row
    in_specs = [
        pl.BlockSpec((TM, d), row),
        pl.BlockSpec((1, 6, d), lambda i: ((i // ntb) * 2 + jnp.minimum(i % ntb, 1), 0, 0)),
        pl.BlockSpec((1, d), const),
        pl.BlockSpec((d, n_exp), const),
        pl.BlockSpec((1, n_exp), const),
    ]
    out_specs = [
        pl.BlockSpec((TM, d), out_row),
        pl.BlockSpec((TM * (d // LANES), LANES), out_row),
        pl.BlockSpec((TM, LANES), out_row),
        pl.BlockSpec((1, n_exp), const),
    ]
    return in_specs, out_specs


def _tail_shapes(t, d, n_exp):
    return [jax.ShapeDtypeStruct((t, d), F32), jax.ShapeDtypeStruct((t * (d // LANES), LANES), F32),
            jax.ShapeDtypeStruct((t, LANES), F32), jax.ShapeDtypeStruct((1, n_exp), F32)]


def _out_proj0(attn, bg, p, conv_w, w_out, ctx, x, modv, g, w_r, b_r, ntb):
    d = x.shape[1]
    t = ctx.shape[0] + x.shape[0]
    n_exp = w_r.shape[1]
    bw = bg.shape[1]
    row = lambda i: (i, 0)
    const = lambda i: (0, 0)
    nblk = TM // SUBLANES
    tail_in, tail_out = _tail_specs(d, n_exp, ntb, False)
    return pl.pallas_call(
        functools.partial(_out0_kernel, ntb=ntb),
        grid=(t // TM,),
        in_specs=[
            pl.BlockSpec((TM, attn.shape[1]), row),
            pl.BlockSpec((TM, bw), row),
            pl.BlockSpec((TM, bw), row),
            pl.BlockSpec((SUBLANES, bw), lambda i: (jnp.maximum(i * nblk - 1, 0), 0)),
            pl.BlockSpec((SUBLANES, bw), lambda i: (jnp.minimum((i + 1) * nblk, t // SUBLANES - 1), 0)),
            pl.BlockSpec(conv_w.shape, const),
            pl.BlockSpec(w_out.shape, const),
        ] + _stream_specs(d, ntb) + tail_in[1:],
        out_specs=tail_out,
        out_shape=_tail_shapes(t, d, n_exp),
        scratch_shapes=[pltpu.VMEM((1, n_exp), F32)],
        compiler_params=_cparams(("arbitrary",)),
        name="l0_out_proj_router",
    )(attn, bg, p, p, p, conv_w, w_out, ctx, x, modv, g, w_r, b_r)


def _out_proj1(hs, gg, w_out, x, modv, g, w_r, b_r, ntb):
    t, d = x.shape
    n_exp = w_r.shape[1]
    row = lambda i: (i, 0)
    const = lambda i: (0, 0)
    tail_in, tail_out = _tail_specs(d, n_exp, ntb, True)
    t_lat = t // ntb * (ntb - 1)
    return pl.pallas_call(
        functools.partial(_out1_kernel, ntb=ntb),
        grid=(t // TM,),
        in_specs=[
            pl.BlockSpec((2, TM, hs.shape[2]), lambda i: (0, i, 0)),
            pl.BlockSpec((TM, gg.shape[1]), row),
            pl.BlockSpec(w_out.shape, const),
        ] + tail_in,
        out_specs=tail_out,
        out_shape=_tail_shapes(t_lat, d, n_exp),
        scratch_shapes=[pltpu.VMEM((1, n_exp), F32)],
        compiler_params=_cparams(("arbitrary",)),
        name="l1_out_proj_router",
    )(hs, gg, w_out, x, modv, g, w_r, b_r)


PAIRS_PER_TILE = TM * TOP_K


def _stage_pos(pos_ref, pbuf, psem):
    i = pl.program_id(0)
    slot = i % 2

    def chunk(ti, s):
        return pltpu.make_async_copy(pos_ref.at[pl.ds(ti * PAIRS_PER_TILE, PAIRS_PER_TILE)],
                                     pbuf.at[pl.ds(s * PAIRS_PER_TILE, PAIRS_PER_TILE)], psem.at[s])

    @pl.when(i == 0)
    def _():
        chunk(0, 0).start()

    chunk(i, slot).wait()

    @pl.when(i + 1 < pl.num_programs(0))
    def _():
        chunk(i + 1, 1 - slot).start()

    return slot * PAIRS_PER_TILE


def _dispatch_kernel(pad_lo_ref, pad_hi_ref, pos_ref, h2_ref, xs_ref, pbuf, zbuf, psem, dsem, zsem):
    i = pl.program_id(0)
    nch = h2_ref.shape[0] // TM
    n_tails = pad_lo_ref.shape[0] - 1

    def zero_slots(first, n):
        return pltpu.make_async_copy(zbuf.at[pl.ds(0, n * nch)],
                                     xs_ref.at[pl.ds(pl.multiple_of(first * nch, nch), n * nch)], zsem)

    def over_padding(act):
        def tail(e, carry):
            lo = pad_lo_ref[e]
            n = pad_hi_ref[e] - lo
            for bit in range(TME.bit_length() - 1):
                size = 1 << bit

                @pl.when((n & size) != 0)
                def _():
                    act(zero_slots(lo + (n & -(2 * size)), size))
            return carry

        lax.fori_loop(0, n_tails, tail, 0)

        def tile(j, c):
            act(zero_slots(j * TME, TME))
            return c

        lax.fori_loop(pad_lo_ref[n_tails] // TME, pad_hi_ref[n_tails] // TME, tile, 0)

    @pl.when(i == 0)
    def _():
        zbuf[...] = jnp.zeros(zbuf.shape, zbuf.dtype)
        over_padding(lambda cp: cp.start())
        over_padding(lambda cp: cp.wait())

    off = _stage_pos(pos_ref, pbuf, psem)
    for r in range(TM):
        for kk in range(TOP_K):
            dst = pl.multiple_of(pbuf[off + r * TOP_K + kk] * nch, nch)
            pltpu.make_async_copy(h2_ref.at[pl.ds(r * nch, nch)], xs_ref.at[pl.ds(dst, nch)], dsem).start(
                priority=kk % 2)
    for _ in range(TOP_K):
        pltpu.make_async_copy(h2_ref, xs_ref.at[pl.ds(0, TM * nch)], dsem).wait()


def _dispatch(pos, pad_lo, pad_hi, h2, n_slots, d):
    nch = d // LANES
    n_tok = h2.shape[0] // nch
    grid_spec = pltpu.PrefetchScalarGridSpec(
        num_scalar_prefetch=2,
        grid=(n_tok // TM,),
        in_specs=[pl.BlockSpec(memory_space=pl.ANY),
                  pl.BlockSpec((TM * nch, LANES), lambda i, lo, hi: (i, 0))],
        out_specs=pl.BlockSpec(memory_space=pl.ANY),
        scratch_shapes=[pltpu.SMEM((2 * PAIRS_PER_TILE,), I32), pltpu.VMEM((TME * nch, LANES), F32),
                        pltpu.SemaphoreType.DMA((2,)), pltpu.SemaphoreType.DMA, pltpu.SemaphoreType.DMA],
    )
    return pl.pallas_call(
        _dispatch_kernel,
        grid_spec=grid_spec,
        out_shape=jax.ShapeDtypeStruct((n_slots * nch, LANES), F32),
        compiler_params=_cparams(("arbitrary",)),
        name="expert_dispatch",
    )(pad_lo, pad_hi, pos, h2)


def _moe_kernel(te_ref, nx_ref, par_ref, meta_ref, x_ref, w1_hbm, b1_ref, w2_hbm, b2_ref, y_ref,
                w1f, w2f, w1b, w2b, wsem, *, layer):
    i = pl.program_id(0)
    n_tiles = meta_ref[0]
    d = w1b.shape[0]
    ff = w2b.shape[0]
    nch = d // LANES

    def weight_copies(e, s):
        return (pltpu.make_async_copy(w1_hbm.at[layer, e], w1f.at[s], wsem.at[0, s]),
                pltpu.make_async_copy(w2_hbm.at[layer, e], w2f.at[s], wsem.at[1, s]))

    @pl.when(i == 0)
    def _():
        for cp in weight_copies(te_ref[0], par_ref[0]):
            cp.start()

    @pl.when(i < n_tiles)
    def _():
        new_expert = jnp.logical_or(i == 0, te_ref[i] != te_ref[jnp.maximum(i - 1, 0)])

        @pl.when(new_expert)
        def _():
            s = par_ref[i]
            for cp in weight_copies(te_ref[i], s):
                cp.wait()
            w1b[...] = w1f[s].astype(BF16)
            w2b[...] = w2f[s].astype(BF16)

            @pl.when(nx_ref[i] >= 0)
            def _():
                for cp in weight_copies(nx_ref[i], 1 - s):
                    cp.start()

        x = jnp.concatenate([x_ref[pl.ds(c, TME, stride=nch), :] for c in range(nch)], axis=1)
        h = _dot(x.astype(BF16), w1b[...]) + b1_ref[0]
        glu = jnp.minimum(h[:, :ff], SWIGLU_LIMIT)
        lin = jnp.clip(h[:, ff:], -SWIGLU_LIMIT, SWIGLU_LIMIT)
        act = glu * _sigmoid(SWIGLU_ALPHA * glu) * (lin + 1.0)
        y = _dot(act.astype(BF16), w2b[...]) + b2_ref[0]
        for c in range(nch):
            y_ref[pl.ds(c, TME, stride=nch), :] = y[:, c * LANES:(c + 1) * LANES]

    @pl.when(i >= n_tiles)
    def _():
        y_ref[...] = jnp.zeros(y_ref.shape, y_ref.dtype)


def _routed_experts(tile_expert, next_expert, parity, meta, xs, w1, b1, w2, b2, layer):
    _, n_exp, d, ff2 = w1.shape
    ff = w2.shape[2]
    nch = d // LANES
    n_grid = tile_expert.shape[0]
    used = lambda i, mt: jnp.minimum(i, mt[0] - 1)
    bmap = lambda i, te, nx, pr, mt: (layer * n_exp + te[used(i, mt)], 0, 0)
    grid_spec = pltpu.PrefetchScalarGridSpec(
        num_scalar_prefetch=4,
        grid=(n_grid,),
        in_specs=[
            pl.BlockSpec((TME * nch, LANES), lambda i, te, nx, pr, mt: (used(i, mt), 0)),
            pl.BlockSpec(memory_space=pl.ANY),
            pl.BlockSpec((1, 1, ff2), bmap),
            pl.BlockSpec(memory_space=pl.ANY),
            pl.BlockSpec((1, 1, d), bmap),
        ],
        out_specs=pl.BlockSpec((TME * nch, LANES), lambda i, te, nx, pr, mt: (i, 0)),
        scratch_shapes=[pltpu.VMEM((2, d, ff2), F32), pltpu.VMEM((2, ff, d), F32),
                        pltpu.VMEM((d, ff2), BF16), pltpu.VMEM((ff, d), BF16), pltpu.SemaphoreType.DMA((2, 2))],
    )
    return pl.pallas_call(
        functools.partial(_moe_kernel, layer=layer),
        grid_spec=grid_spec,
        out_shape=jax.ShapeDtypeStruct(xs.shape, F32),
        compiler_params=_cparams(("arbitrary",)),
        name="routed_experts",
    )(tile_expert, next_expert, parity, meta, xs, w1, b1.reshape(-1, 1, ff2), w2, b2.reshape(-1, 1, d))


def _gather_choices(pos_ref, ys_ref, ybuf, pbuf, psem, csem):
    i = pl.program_id(0)
    n = pl.num_programs(0)
    slot = i % 2
    nch = ybuf.shape[2] // TM

    def chunk(ti, s):
        return pltpu.make_async_copy(pos_ref.at[pl.ds(ti * PAIRS_PER_TILE, PAIRS_PER_TILE)],
                                     pbuf.at[pl.ds(s * PAIRS_PER_TILE, PAIRS_PER_TILE)], psem.at[s])

    def row_copy(s, r, kk):
        src = pl.multiple_of(pbuf[s * PAIRS_PER_TILE + r * TOP_K + kk] * nch, nch)
        return pltpu.make_async_copy(ys_ref.at[pl.ds(src, nch)], ybuf.at[s, kk, pl.ds(r * nch, nch)], csem.at[s])

    @pl.when(i == 0)
    def _():
        chunk(0, 0).start()
        chunk(0, 0).wait()

        def first_rows(r, c):
            for kk in range(TOP_K):
                row_copy(0, r, kk).start(priority=kk % 2)
            return c

        lax.fori_loop(0, TM, first_rows, 0)

        @pl.when(n > 1)
        def _():
            chunk(1, 1).start()

    @pl.when(i + 1 < n)
    def _():
        chunk(i + 1, 1 - slot).wait()
        for r in range(TM):
            for kk in range(TOP_K):
                row_copy(1 - slot, r, kk).start(priority=kk % 2)

        @pl.when(i + 2 < n)
        def _():
            chunk(i + 2, slot).start()

    for kk in range(TOP_K):
        pltpu.make_async_copy(ys_ref.at[pl.ds(0, TM * nch)], ybuf.at[slot, kk], csem.at[slot]).wait()
    return slot


def _gather_scratch(d):
    nch = d // LANES
    return [pltpu.VMEM((2, TOP_K, TM * nch, LANES), F32), pltpu.SMEM((2 * PAIRS_PER_TILE,), I32),
            pltpu.SemaphoreType.DMA((2,)), pltpu.SemaphoreType.DMA((2,))]


def _plan_routes(route, counts, n_grid):
    n_exp = counts.shape[-1]
    cnt = counts.reshape(n_exp).astype(I32)
    tiles_per = (cnt + TME - 1) // TME
    tile_end = jnp.cumsum(tiles_per)
    offset = (tile_end - tiles_per) * TME
    n_tiles = tile_end[-1]
    tile_ids = jnp.minimum(jnp.arange(n_grid, dtype=I32), n_tiles - 1)
    tile_expert = jnp.sum((tile_ids[:, None] >= tile_end[None, :]).astype(I32), axis=1)
    expert_ids = jnp.arange(n_exp, dtype=I32)
    of_tile = tile_expert[:, None] == expert_ids
    later = jnp.logical_and(expert_ids[None, :] > expert_ids[:, None], (tiles_per > 0)[None, :])
    next_of = jnp.min(jnp.where(later, expert_ids[None, :], n_exp), axis=1)
    next_of = jnp.where(next_of == n_exp, -1, next_of)
    next_expert = jnp.sum(jnp.where(of_tile, next_of, 0), axis=1)
    parity = jnp.sum(jnp.where(of_tile, jnp.cumsum((tiles_per > 0).astype(I32)), 0), axis=1) % 2
    eidx = route[:, :TOP_K].astype(I32)
    rank = route[:, TOP_K:2 * TOP_K].astype(I32)
    pos = jnp.sum(jnp.where(eidx[..., None] == expert_ids, offset, 0), axis=-1) + rank
    pad_lo = jnp.concatenate([offset + cnt, (n_tiles * TME).reshape(1)])
    pad_hi = jnp.concatenate([tile_end * TME, jnp.full((1,), n_grid * TME, I32)])
    return (tile_expert, next_expert, parity, n_tiles.reshape(1)), pos.reshape(-1), pad_lo, pad_hi


def _moe_layer(route, counts, h2, w1, b1, w2, b2, layer):
    n_exp, d = w1.shape[1], w1.shape[2]
    n_tok = route.shape[0]
    n_grid = n_tok * TOP_K // TME + n_exp
    tiles, pos, pad_lo, pad_hi = _plan_routes(route, counts, n_grid)
    xs = _dispatch(pos, pad_lo, pad_hi, h2, n_grid * TME, d)
    return pos, _routed_experts(*tiles, xs, w1, b1, w2, b2, layer)


def _combine(ybuf, slot, route_ref, d):
    nch = d // LANES
    gates = route_ref[...]
    gk = [jnp.broadcast_to(gates[:, 2 * TOP_K + kk:2 * TOP_K + kk + 1], (TM, LANES)) for kk in range(TOP_K)]
    chunks = []
    for c in range(nch):
        acc = gk[0] * ybuf[slot, 0, pl.ds(c, TM, stride=nch), :]
        for kk in range(1, TOP_K):
            acc = acc + gk[kk] * ybuf[slot, kk, pl.ds(c, TM, stride=nch), :]
        chunks.append(acc)
    return jnp.concatenate(chunks, axis=1)


def _in1_kernel(x_ref, pos_ref, ys_ref, route_ref, mod0_ref, mod1_ref, g_ref, w_ref, x2_ref, gg_ref, u_ref,
                ybuf, pbuf, psem, csem):
    slot = _gather_choices(pos_ref, ys_ref, ybuf, pbuf, psem, csem)
    x2 = x_ref[...] + mod0_ref[0][5:6] * _combine(ybuf, slot, route_ref, x_ref.shape[1])
    x2_ref[...] = x2
    mod1 = mod1_ref[0]
    h = _norm_mod(x2, g_ref[...], mod1[0:1], mod1[1:2])
    y = _dot(h.astype(BF16), w_ref[...])
    half = y.shape[1] // 2
    gg_ref[...] = jax.nn.gelu(y[:, :half]).astype(BF16)
    u_ref[...] = y[:, half:]


def _in_proj1(x, pos, ys, route, modv0, modv1, g, w, ntb):
    t, d = x.shape
    n = w.shape[1]
    row = lambda i: (i, 0)
    mod_map = lambda i: ((i // ntb) * 2 + jnp.minimum(i % ntb, 1), 0, 0)
    return pl.pallas_call(
        _in1_kernel,
        grid=(t // TM,),
        in_specs=[
            pl.BlockSpec((TM, d), row),
            pl.BlockSpec(memory_space=pl.ANY),
            pl.BlockSpec(memory_space=pl.ANY),
            pl.BlockSpec((TM, LANES), row),
            pl.BlockSpec((1, 6, d), mod_map),
            pl.BlockSpec((1, 6, d), mod_map),
            pl.BlockSpec((1, d), lambda i: (0, 0)),
            pl.BlockSpec((d, n), lambda i: (0, 0)),
        ],
        out_specs=[pl.BlockSpec((TM, d), row), pl.BlockSpec((TM, n // 2), row), pl.BlockSpec((TM, n // 2), row)],
        out_shape=[jax.ShapeDtypeStruct((t, d), F32), jax.ShapeDtypeStruct((t, n // 2), BF16),
                   jax.ShapeDtypeStruct((t, n // 2), F32)],
        scratch_shapes=_gather_scratch(d),
        compiler_params=_cparams(("arbitrary",)),
        name="l1_combine_in_proj",
    )(x, pos, ys, route, modv0, modv1, g, w)


def _rglru_kernel(u_ref, halo_ref, cw_ref, cb_ref, gaw_ref, gab_ref, gxw_ref, gxb_ref, lam_ref, o_ref,
                  ext, a_s, b_s, h_s, state, *, ntb):
    d = pl.program_id(0)
    s = pl.program_id(2)
    nb, tc, cw_ = u_ref.shape
    u = u_ref[...]
    cw = cw_ref[0]

    def finish(xc, reverse):
        xc2 = xc.reshape(nb * tc, cw_) + cb_ref[0]
        xb = xc2.astype(BF16)
        t_r = jnp.tanh(_dot(xb, gaw_ref[0, 0]) + gab_ref[0])
        t_i = jnp.tanh(_dot(xb, gxw_ref[0, 0]) + gxb_ref[0])
        nl = -lam_ref[0]
        softplus = jnp.maximum(nl, 0.0) + jnp.log1p(jnp.exp(-jnp.abs(nl)))
        half_rate = (-0.5 * LRU_C) * softplus
        log_a = half_rate * t_r + half_rate
        a = jnp.exp(log_a)
        half_x = 0.5 * xc2
        bb = jnp.sqrt(1.0 - a * a) * (half_x * t_i + half_x)
        n_lane = cw_ // LANES
        pitch = tc + SCAN_PAD
        for c in range(n_lane):
            for bi in range(nb):
                a_s[c, bi * pitch:bi * pitch + tc, :] = a[bi * tc:(bi + 1) * tc, c * LANES:(c + 1) * LANES]
                b_s[c, bi * pitch:bi * pitch + tc, :] = bb[bi * tc:(bi + 1) * tc, c * LANES:(c + 1) * LANES]

        @pl.when(s == 0)
        def _():
            state[...] = jnp.zeros_like(state)

        def steps(tb, hs):
            hs = list(hs)
            for uu in range(SCAN_UNROLL):
                tt = tb * SCAN_UNROLL + uu
                t = (tc - 1 - tt) if reverse else tt
                rows = pl.ds(t, nb, stride=pitch)
                for c in range(n_lane):
                    hs[c] = a_s[c, rows, :] * hs[c] + b_s[c, rows, :]
                    h_s[c, rows, :] = hs[c]
            return tuple(hs)

        hs = lax.fori_loop(0, tc // SCAN_UNROLL, steps, tuple(state[c] for c in range(n_lane)))
        for c in range(n_lane):
            state[c] = hs[c]
            for bi in range(nb):
                o_ref[0, bi, :, c * LANES:(c + 1) * LANES] = h_s[c, bi * pitch:bi * pitch + tc, :].astype(o_ref.dtype)

    @pl.when(d == 0)
    def _():
        chunk = s
        keep = chunk > 1
        ext[:, 0:SUBLANES, :] = jnp.where(keep, halo_ref[...], 0.0)
        ext[:, SUBLANES:, :] = u
        xc = cw[3:4] * u
        for j in range(1, C_CONV):
            xc = xc + cw[3 - j:4 - j] * ext[:, SUBLANES - j:SUBLANES - j + tc, :]
        finish(xc, False)

    @pl.when(d == 1)
    def _():
        chunk = jnp.where(s == 0, 0, ntb - s)
        keep = jnp.logical_and(chunk > 0, chunk < ntb - 1)
        ext[:, 0:tc, :] = u
        ext[:, tc:, :] = jnp.where(keep, halo_ref[...], 0.0)
        xc = cw[0:1] * u
        for j in range(1, C_CONV):
            xc = xc + cw[j:j + 1] * ext[:, j:j + tc, :]
        finish(xc, True)


def _rglru(u, conv_w, conv_b, ga_w, ga_b, gx_w, gx_b, lam, b, l):
    width = u.shape[1]
    cb = width // C_BLOCKS
    ntb = l // TM
    u3 = u.reshape(b, l, width)
    nblk = TM // SUBLANES

    def chunk_of(d, s):
        return jnp.where(d == 0, s, jnp.where(s == 0, 0, ntb - s))

    def halo_of(d, s):
        c = chunk_of(d, s)
        return jnp.where(d == 0, jnp.maximum(c * nblk - 1, 0), jnp.minimum((c + 1) * nblk, l // SUBLANES - 1))

    vec = pl.BlockSpec((1, 1, cb), lambda d, g, s: (d, 0, g))
    mat = pl.BlockSpec((1, 1, cb, cb), lambda d, g, s: (d, g, 0, 0))
    return pl.pallas_call(
        functools.partial(_rglru_kernel, ntb=ntb),
        grid=(2, C_BLOCKS, ntb),
        in_specs=[
            pl.BlockSpec((b, TM, cb), lambda d, g, s: (0, chunk_of(d, s), g)),
            pl.BlockSpec((b, SUBLANES, cb), lambda d, g, s: (0, halo_of(d, s), g)),
            pl.BlockSpec((1, C_CONV, cb), lambda d, g, s: (d, 0, g)),
            vec, mat, vec, mat, vec, vec,
        ],
        out_specs=pl.BlockSpec((1, b, TM, cb), lambda d, g, s: (d, 0, chunk_of(d, s), g)),
        out_shape=jax.ShapeDtypeStruct((2, b, l, width), BF16),
        scratch_shapes=[
            pltpu.VMEM((b, TM + SUBLANES, cb), F32),
            pltpu.VMEM((cb // LANES, b * (TM + SCAN_PAD), LANES), F32),
            pltpu.VMEM((cb // LANES, b * (TM + SCAN_PAD), LANES), F32),
            pltpu.VMEM((cb // LANES, b * (TM + SCAN_PAD), LANES), F32),
            pltpu.VMEM((cb // LANES, b, LANES), F32),
        ],
        compiler_params=_cparams(("arbitrary", "arbitrary", "arbitrary")),
        name="l1_rglru",
    )(u3, u3, conv_w, conv_b.reshape(2, 1, width), (0.5 * ga_w).astype(BF16), (0.5 * ga_b).reshape(2, 1, width),
      (0.5 * gx_w).astype(BF16), (0.5 * gx_b).reshape(2, 1, width), lam.reshape(2, 1, width))


def _final_kernel(x_ref, pos_ref, ys_ref, route_ref, mod_ref, g_ref, o_ref, ybuf, pbuf, psem, csem):
    slot = _gather_choices(pos_ref, ys_ref, ybuf, pbuf, psem, csem)
    x = x_ref[...] + mod_ref[0][5:6] * _combine(ybuf, slot, route_ref, x_ref.shape[1])
    ms = jnp.mean(x * x, axis=-1, keepdims=True)
    o_ref[...] = x * lax.rsqrt(ms + NORM_EPS) * g_ref[...]


def _final(x, pos, ys, route, modv, g, b, s_len):
    t, d = x.shape
    per_b = s_len // TM
    row = lambda i: (i, 0)
    return pl.pallas_call(
        _final_kernel,
        grid=(t // TM,),
        in_specs=[
            pl.BlockSpec((TM, d), row),
            pl.BlockSpec(memory_space=pl.ANY),
            pl.BlockSpec(memory_space=pl.ANY),
            pl.BlockSpec((TM, LANES), row),
            pl.BlockSpec((1, 6, d), lambda i: ((i // per_b) * 2 + 1, 0, 0)),
            pl.BlockSpec((1, d), lambda i: (0, 0)),
        ],
        out_specs=pl.BlockSpec((TM, d), row),
        out_shape=jax.ShapeDtypeStruct((t, d), F32),
        scratch_shapes=_gather_scratch(d),
        compiler_params=_cparams(("arbitrary",)),
        name="final_combine_norm",
    )(x, pos, ys, route, modv, g)


def _rope_tables(s_len, n_ctx):
    n_rows = s_len // GRID_W
    rows, cols = jnp.meshgrid(jnp.arange(n_rows), jnp.arange(GRID_W), indexing="ij")
    pos = jnp.stack([rows.reshape(-1), cols.reshape(-1)], axis=-1).astype(F32)
    n_freq = HEAD_DIM // 4
    inv = ROPE_THETA ** (-jnp.arange(n_freq, dtype=F32) / n_freq)
    ang = pos[:, :, None] * inv
    cos, sin = jnp.cos(ang), jnp.sin(ang)
    cos64 = jnp.stack([cos, cos], axis=2).reshape(s_len, HEAD_DIM)
    sin64 = jnp.stack([-sin, sin], axis=2).reshape(s_len, HEAD_DIM)
    cos_l = jnp.tile(cos64, (1, LANES // HEAD_DIM))
    sin_l = jnp.tile(sin64, (1, LANES // HEAD_DIM))
    cos_t = jnp.concatenate([jnp.ones((n_ctx, LANES), F32), cos_l], axis=0)
    sin_t = jnp.concatenate([jnp.zeros((n_ctx, LANES), F32), sin_l], axis=0)
    return cos_t, sin_t


def kernel(x, c, ctx, c_ctx, w_mod, b_mod, norm_mix, norm_ffn, ev_w_in, ev_w_out, ev_lambda_q1, ev_lambda_k1, ev_lambda_q2, ev_lambda_k2, ev_subln, ev_conv_w, od_w_in, od_w_out, od_conv_w, od_conv_b, od_gate_a_w, od_gate_a_b, od_gate_x_w, od_gate_x_b, od_lru_lambda, moe_w_router, moe_b_router, moe_w1, moe_b1, moe_w2, moe_b2, final_norm):
    b, s_len, d = x.shape
    n_ctx = ctx.shape[1]
    l = n_ctx + s_len
    t = b * l
    ntb = l // TM
    assert n_ctx == TM and s_len % TM == 0 and w_mod.shape[0] == 2

    ctx2 = ctx.reshape(b * n_ctx, d)
    x2d = x.reshape(b * s_len, d)

    n_rows = -(-(b + 1) // SUBLANES) * SUBLANES
    cs = jnp.concatenate([c, c_ctx[None, :], jnp.zeros((n_rows - b - 1, d), F32)], axis=0)
    mod = _modulation(cs, w_mod, b_mod)

    def mod_table(i):
        lat = mod[i, :b]
        cx = jnp.broadcast_to(mod[i, b][None, :], lat.shape)
        return jnp.stack([cx, lat], axis=1).reshape(b * 2, 6, d)

    modv0, modv1 = mod_table(0), mod_table(1)

    cos_t, sin_t = _rope_tables(s_len, n_ctx)
    q, k, v, bg, p = _in_proj0(ctx2, x2d, modv0, norm_mix[0:1], ev_w_in[0].astype(BF16), cos_t, sin_t, ntb)
    lam_init = 0.8 - 0.6 * math.exp(-0.3 * 0)
    attn = _diff_attention(q, k, v, ev_lambda_q1[0:1], ev_lambda_k1[0:1], ev_lambda_q2[0:1], ev_lambda_k2[0:1],
                           ev_subln[0:1], lam_init, b, l, n_ctx)
    x1, h2, route0, counts0 = _out_proj0(attn, bg, p, ev_conv_w[0], ev_w_out[0].astype(BF16), ctx2, x2d, modv0,
                                         norm_ffn[0:1], moe_w_router[0], moe_b_router[0:1], ntb)
    pos0, ys0 = _moe_layer(route0, counts0, h2, moe_w1, moe_b1, moe_w2, moe_b2, 0)

    x2, gg, u = _in_proj1(x1, pos0, ys0, route0, modv0, modv1, norm_mix[1:2], od_w_in[0].astype(BF16), ntb)
    hs = _rglru(u, od_conv_w[0], od_conv_b[0], od_gate_a_w[0], od_gate_a_b[0], od_gate_x_w[0], od_gate_x_b[0],
                od_lru_lambda[0], b, l)
    x3, h3, route1, counts1 = _out_proj1(hs.reshape(2, t, hs.shape[-1]), gg, od_w_out[0].astype(BF16), x2, modv1,
                                         norm_ffn[1:2], moe_w_router[1], moe_b_router[1:2], ntb)
    pos1, ys1 = _moe_layer(route1, counts1, h3, moe_w1, moe_b1, moe_w2, moe_b2, 1)

    out = _final(x3, pos1, ys1, route1, modv1, final_norm[None, :], b, s_len)
    return out.reshape(b, s_len, d)
```

```python
import functools
import math

import jax
import jax.numpy as jnp
from jax import lax
from jax.experimental import pallas as pl
from jax.experimental.pallas import tpu as pltpu

F32 = jnp.float32
BF16 = jnp.bfloat16
I32 = jnp.int32

NORM_EPS = 1e-6
ROPE_THETA = 10000.0
GRID_W = 64
N_HEADS = 4
HEAD_DIM = 64
A_WIDTH = 2 * N_HEADS * HEAD_DIM
B_CONV = 3
C_CONV = 4
C_BLOCKS = 4
LRU_C = 8.0
TOP_K = 4
SWIGLU_ALPHA = 1.702
SWIGLU_LIMIT = 7.0

LANES = 128
SUBLANES = 8
TM = 256
TME = 512
VMEM_LIMIT = 56 * 1024 * 1024
SCAN_PAD = 8
SCAN_UNROLL = 8
LOG2_E = 1.4426950408889634


def _cparams(sem, vmem=VMEM_LIMIT):
    return pltpu.CompilerParams(dimension_semantics=sem, vmem_limit_bytes=vmem)


def _norm_mod(x, g, shift, scale):
    ms = jnp.mean(x * x, axis=-1, keepdims=True)
    return (x * lax.rsqrt(ms + NORM_EPS) * g) * (1.0 + scale) + shift


def _dot(a, b):
    return jnp.dot(a, b, preferred_element_type=F32)


def _sigmoid(x):
    return 0.5 * jnp.tanh(0.5 * x) + 0.5


def _mod_kernel(cs_ref, w_ref, b_ref, o_ref):
    s = cs_ref[...]
    s = s * jax.nn.sigmoid(s)
    o_ref[0] = _dot(s.astype(BF16), w_ref[0].astype(BF16)) + b_ref[0]


def _modulation(cs, w_mod, b_mod):
    depth, d, n = w_mod.shape
    rows = cs.shape[0]
    tn = 1536
    return pl.pallas_call(
        _mod_kernel,
        grid=(depth, n // tn),
        in_specs=[
            pl.BlockSpec((rows, d), lambda i, j: (0, 0)),
            pl.BlockSpec((1, d, tn), lambda i, j: (i, 0, j)),
            pl.BlockSpec((1, 1, tn), lambda i, j: (i, 0, j)),
        ],
        out_specs=pl.BlockSpec((1, rows, tn), lambda i, j: (i, 0, j)),
        out_shape=jax.ShapeDtypeStruct((depth, rows, n), F32),
        compiler_params=_cparams(("parallel", "parallel")),
        name="adaln_modulation",
    )(cs, w_mod, b_mod.reshape(depth, 1, n))


def _stream_tile(ctx_ref, x_ref, ntb):
    return jnp.where(pl.program_id(0) % ntb == 0, ctx_ref[...], x_ref[...])


def _stream_specs(d, ntb):
    return [pl.BlockSpec((TM, d), lambda i: (i // ntb, 0)),
            pl.BlockSpec((TM, d), lambda i: ((i // ntb) * (ntb - 1) + jnp.maximum(i % ntb - 1, 0), 0))]


def _in0_kernel(ctx_ref, x_ref, mod_ref, g_ref, w_ref, cos_ref, sin_ref,
                q_ref, k_ref, v_ref, bg_ref, p_ref, *, ntb):
    mod = mod_ref[0]
    h = _norm_mod(_stream_tile(ctx_ref, x_ref, ntb), g_ref[...], mod[0:1], mod[1:2])
    y = _dot(h.astype(BF16), w_ref[...])
    cosv = cos_ref[...]
    sinv = sin_ref[...]
    lane = lax.broadcasted_iota(I32, (TM, LANES), 1)
    first_half = (lane & 16) == 0

    def rope(z):
        outs = []
        for g in range(A_WIDTH // LANES):
            zg = z[:, g * LANES:(g + 1) * LANES]
            partner = jnp.where(first_half, pltpu.roll(zg, LANES - 16, 1), pltpu.roll(zg, 16, 1))
            outs.append(zg * cosv + partner * sinv)
        return jnp.concatenate(outs, axis=1)

    aw = A_WIDTH
    q_ref[...] = (rope(y[:, :aw]) * (HEAD_DIM ** -0.5 * LOG2_E)).astype(BF16)
    k_ref[...] = rope(y[:, aw:2 * aw]).astype(BF16)
    v_ref[...] = y[:, 2 * aw:3 * aw].astype(BF16)
    bw = (y.shape[1] - 3 * aw) // 3
    bg_ref[...] = y[:, 3 * aw:3 * aw + bw].astype(BF16)
    p_ref[...] = (y[:, 3 * aw + bw:3 * aw + 2 * bw] * y[:, 3 * aw + 2 * bw:]).astype(BF16)


def _in_proj0(ctx, x, modv, g, w, cos_t, sin_t, ntb):
    d = x.shape[1]
    t = ctx.shape[0] + x.shape[0]
    n = w.shape[1]
    bw = (n - 3 * A_WIDTH) // 3
    row = lambda i: (i, 0)
    return pl.pallas_call(
        functools.partial(_in0_kernel, ntb=ntb),
        grid=(t // TM,),
        in_specs=_stream_specs(d, ntb) + [
            pl.BlockSpec((1, 6, d), lambda i: ((i // ntb) * 2 + jnp.minimum(i % ntb, 1), 0, 0)),
            pl.BlockSpec((1, d), lambda i: (0, 0)),
            pl.BlockSpec((d, n), lambda i: (0, 0)),
            pl.BlockSpec((TM, LANES), lambda i: (i % ntb, 0)),
            pl.BlockSpec((TM, LANES), lambda i: (i % ntb, 0)),
        ],
        out_specs=[pl.BlockSpec((TM, A_WIDTH), row)] * 3 + [pl.BlockSpec((TM, bw), row)] * 2,
        out_shape=[jax.ShapeDtypeStruct((t, A_WIDTH), BF16)] * 3 + [jax.ShapeDtypeStruct((t, bw), BF16)] * 2,
        compiler_params=_cparams(("parallel",)),
        name="l0_in_proj_rope",
    )(ctx, x, modv, g, w, cos_t, sin_t)


def _attn_kernel(lq1_ref, lk1_ref, lq2_ref, lk2_ref, g_ref, q_ref, k_ref, v_ref, o_ref, *, lam_init, n_ctx):
    qi = pl.program_id(2)
    lam = (jnp.exp(jnp.sum(lq1_ref[...] * lk1_ref[...], axis=-1, keepdims=True))
           - jnp.exp(jnp.sum(lq2_ref[...] * lk2_ref[...], axis=-1, keepdims=True)) + lam_init)
    q = q_ref[...]
    lane = lax.broadcasted_iota(I32, q.shape, 1)
    zero = jnp.zeros_like(q)
    q1 = jnp.where(lane < HEAD_DIM, q, zero)
    q2 = jnp.where(lane < HEAD_DIM, zero, q)
    contract_last = (((1,), (1,)), ((), ()))

    def attend(nk):
        k = k_ref[0, :nk, :]
        v = v_ref[0, :nk, :]
        s1 = lax.dot_general(q1, k, contract_last, preferred_element_type=F32)
        s2 = lax.dot_general(q2, k, contract_last, preferred_element_type=F32)
        p1 = jnp.exp2(s1 - jnp.max(s1, axis=-1, keepdims=True))
        p2 = jnp.exp2(s2 - jnp.max(s2, axis=-1, keepdims=True))
        r1 = 1.0 / jnp.sum(p1, axis=-1, keepdims=True)
        r2 = lam / jnp.sum(p2, axis=-1, keepdims=True)
        o = _dot(p1.astype(BF16), v) * r1 - _dot(p2.astype(BF16), v) * r2
        ms = jnp.mean(o * o, axis=-1, keepdims=True)
        o = o * lax.rsqrt(ms + NORM_EPS) * g_ref[...] * (1.0 - lam_init)
        o_ref[...] = o.astype(BF16)

    @pl.when(qi == 0)
    def _():
        attend(n_ctx)

    @pl.when(qi > 0)
    def _():
        attend(k_ref.shape[1])


def _diff_attention(q, k, v, lq1, lk1, lq2, lk2, subln, lam_init, b, l, n_ctx):
    t = q.shape[0]
    ntb = l // TM
    hw = 2 * HEAD_DIM
    k3 = k.reshape(b, l, A_WIDTH)
    v3 = v.reshape(b, l, A_WIDTH)
    vec = lambda n: pl.BlockSpec((1, n), lambda bi, h, qi: (0, 0))
    qspec = pl.BlockSpec((TM, hw), lambda bi, h, qi: (bi * ntb + qi, h))
    kspec = pl.BlockSpec((1, l, hw), lambda bi, h, qi: (bi, 0, h))
    return pl.pallas_call(
        functools.partial(_attn_kernel, lam_init=lam_init, n_ctx=n_ctx),
        grid=(b, N_HEADS, ntb),
        in_specs=[vec(HEAD_DIM)] * 4 + [vec(hw), qspec, kspec, kspec],
        out_specs=qspec,
        out_shape=jax.ShapeDtypeStruct((t, A_WIDTH), BF16),
        compiler_params=_cparams(("parallel", "parallel", "parallel")),
        name="l0_diff_attention",
    )(lq1, lk1, lq2, lk2, subln, q, k3, v3)


def _route_tail(y, x_in, mod_ref, g_ref, wr_ref, br_ref,
                x1_ref, h2_ref, route_ref, cnt_ref, carry_ref):
    mod = mod_ref[0]
    x1 = x_in + mod[2:3] * y
    x1_ref[...] = x1
    h2 = _norm_mod(x1, g_ref[...], mod[3:4], mod[4:5])
    nch = h2.shape[1] // LANES
    for s in range(nch):
        h2_ref[pl.ds(s, TM, stride=nch), :] = h2[:, s * LANES:(s + 1) * LANES]

    logits = _dot(h2.astype(BF16), wr_ref[...].astype(BF16)) + br_ref[...]

    n_exp = logits.shape[1]
    lane = lax.broadcasted_iota(I32, logits.shape, 1).astype(F32)
    work = logits
    sels, vals, idxs = [], [], []
    for _ in range(TOP_K):
        m = jnp.max(work, axis=-1, keepdims=True)
        idx = jnp.min(jnp.where(work == m, lane, float(n_exp)), axis=-1, keepdims=True)
        sel = lane == idx
        sels.append(sel)
        vals.append(m)
        idxs.append(idx)
        work = jnp.where(sel, -jnp.inf, work)
    exps = [jnp.exp(vv - vals[0]) for vv in vals]
    inv_den = 1.0 / (exps[0] + exps[1] + exps[2] + exps[3])

    chosen = jnp.zeros(logits.shape, F32)
    for sel in sels:
        chosen = chosen + jnp.where(sel, 1.0, 0.0)
    r_i = lax.broadcasted_iota(I32, (TM, TM), 0)
    c_i = lax.broadcasted_iota(I32, (TM, TM), 1)
    earlier = jnp.where(c_i < r_i, 1.0, 0.0).astype(BF16)
    rank = _dot(earlier, chosen.astype(BF16)) + carry_ref[...]
    carry_ref[...] = carry_ref[...] + jnp.sum(chosen, axis=0, keepdims=True)
    cnt_ref[...] = carry_ref[...]

    out_lane = lax.broadcasted_iota(I32, (TM, LANES), 1)
    packed = jnp.zeros((TM, LANES), F32)
    for kk in range(TOP_K):
        rank_k = jnp.sum(jnp.where(sels[kk], rank, 0.0), axis=-1, keepdims=True)
        packed = jnp.where(out_lane == kk, idxs[kk], packed)
        packed = jnp.where(out_lane == TOP_K + kk, rank_k, packed)
        packed = jnp.where(out_lane == 2 * TOP_K + kk, exps[kk] * inv_den, packed)
    route_ref[...] = packed


def _out0_kernel(attn_ref, bg_ref, p_ref, pprev_ref, pnext_ref, cw_ref, wo_ref,
                 ctx_ref, x_ref, mod_ref, g_ref, wr_ref, br_ref,
                 x1_ref, h2_ref, route_ref, cnt_ref, carry_ref, *, ntb):
    i = pl.program_id(0)
    seg = i % ntb

    @pl.when(i == 0)
    def _():
        carry_ref[...] = jnp.zeros_like(carry_ref)

    p = p_ref[...].astype(F32)
    row = lax.broadcasted_iota(I32, p.shape, 0)
    has_prev = seg > 1
    has_next = jnp.logical_and(seg > 0, seg < ntb - 1)
    prev_row = jnp.where(has_prev, pprev_ref[SUBLANES - 1:SUBLANES, :].astype(F32), 0.0)
    next_row = jnp.where(has_next, pnext_ref[0:1, :].astype(F32), 0.0)
    before = jnp.where(row == 0, prev_row, pltpu.roll(p, 1, 0))
    after = jnp.where(row == TM - 1, next_row, pltpu.roll(p, TM - 1, 0))
    cw = cw_ref[...]
    conv = bg_ref[...].astype(F32) * (cw[0:1] * before + cw[1:2] * p + cw[2:3] * after)
    aw = attn_ref.shape[1]
    y = _dot(attn_ref[...], wo_ref[:aw, :]) + _dot(conv.astype(BF16), wo_ref[aw:, :])
    _route_tail(y, _stream_tile(ctx_ref, x_ref, ntb), mod_ref, g_ref, wr_ref, br_ref,
                x1_ref, h2_ref, route_ref, cnt_ref, carry_ref)


def _out1_kernel(hs_ref, gg_ref, wo_ref, x_ref, mod_ref, g_ref, wr_ref, br_ref,
                 x1_ref, h2_ref, route_ref, cnt_ref, carry_ref, *, ntb):
    i = pl.program_id(0)

    @pl.when(i == 0)
    def _():
        carry_ref[...] = jnp.zeros_like(carry_ref)

    @pl.when(i % ntb > 0)
    def _():
        rec = hs_ref[0].astype(F32) + hs_ref[1].astype(F32)
        y = _dot((rec * gg_ref[...].astype(F32)).astype(BF16), wo_ref[...])
        _route_tail(y, x_ref[...], mod_ref, g_ref, wr_ref, br_ref, x1_ref, h2_ref, route_ref, cnt_ref, carry_ref)


def _tail_specs(d, n_exp, ntb, latent_only):
    row = lambda i: (i, 0)
    const = lambda i: (0, 0)
    out_row = (lambda i: ((i // ntb) * (ntb - 1) + jnp.maximum(i % ntb - 1, 0), 0)) if latent_only else row
    in_specs = [
        pl.BlockSpec((TM, d), row),
        pl.BlockSpec((1, 6, d), lambda i: ((i // ntb) * 2 + jnp.minimum(i % ntb, 1), 0, 0)),
        pl.BlockSpec((1, d), const),
        pl.BlockSpec((d, n_exp), const),
        pl.BlockSpec((1, n_exp), const),
    ]
    out_specs = [
        pl.BlockSpec((TM, d), out_row),
        pl.BlockSpec((TM * (d // LANES), LANES), out_row),
        pl.BlockSpec((TM, LANES), out_row),
        pl.BlockSpec((1, n_exp), const),
    ]
    return in_specs, out_specs


def _tail_shapes(t, d, n_exp):
    return [jax.ShapeDtypeStruct((t, d), F32), jax.ShapeDtypeStruct((t * (d // LANES), LANES), F32),
            jax.ShapeDtypeStruct((t, LANES), F32), jax.ShapeDtypeStruct((1, n_exp), F32)]


def _out_proj0(attn, bg, p, conv_w, w_out, ctx, x, modv, g, w_r, b_r, ntb):
    d = x.shape[1]
    t = ctx.shape[0] + x.shape[0]
    n_exp = w_r.shape[1]
    bw = bg.shape[1]
    row = lambda i: (i, 0)
    const = lambda i: (0, 0)
    nblk = TM // SUBLANES
    tail_in, tail_out = _tail_specs(d, n_exp, ntb, False)
    return pl.pallas_call(
        functools.partial(_out0_kernel, ntb=ntb),
        grid=(t // TM,),
        in_specs=[
            pl.BlockSpec((TM, attn.shape[1]), row),
            pl.BlockSpec((TM, bw), row),
            pl.BlockSpec((TM, bw), row),
            pl.BlockSpec((SUBLANES, bw), lambda i: (jnp.maximum(i * nblk - 1, 0), 0)),
            pl.BlockSpec((SUBLANES, bw), lambda i: (jnp.minimum((i + 1) * nblk, t // SUBLANES - 1), 0)),
            pl.BlockSpec(conv_w.shape, const),
            pl.BlockSpec(w_out.shape, const),
        ] + _stream_specs(d, ntb) + tail_in[1:],
        out_specs=tail_out,
        out_shape=_tail_shapes(t, d, n_exp),
        scratch_shapes=[pltpu.VMEM((1, n_exp), F32)],
        compiler_params=_cparams(("arbitrary",)),
        name="l0_out_proj_router",
    )(attn, bg, p, p, p, conv_w, w_out, ctx, x, modv, g, w_r, b_r)


def _out_proj1(hs, gg, w_out, x, modv, g, w_r, b_r, ntb):
    t, d = x.shape
    n_exp = w_r.shape[1]
    row = lambda i: (i, 0)
    const = lambda i: (0, 0)
    tail_in, tail_out = _tail_specs(d, n_exp, ntb, True)
    t_lat = t // ntb * (ntb - 1)
    return pl.pallas_call(
        functools.partial(_out1_kernel, ntb=ntb),
        grid=(t // TM,),
        in_specs=[
            pl.BlockSpec((2, TM, hs.shape[2]), lambda i: (0, i, 0)),
            pl.BlockSpec((TM, gg.shape[1]), row),
            pl.BlockSpec(w_out.shape, const),
        ] + tail_in,
        out_specs=tail_out,
        out_shape=_tail_shapes(t_lat, d, n_exp),
        scratch_shapes=[pltpu.VMEM((1, n_exp), F32)],
        compiler_params=_cparams(("arbitrary",)),
        name="l1_out_proj_router",
    )(hs, gg, w_out, x, modv, g, w_r, b_r)


PAIRS_PER_TILE = TM * TOP_K


def _stage_pos(pos_ref, pbuf, psem):
    i = pl.program_id(0)
    slot = i % 2

    def chunk(ti, s):
        return pltpu.make_async_copy(pos_ref.at[pl.ds(ti * PAIRS_PER_TILE, PAIRS_PER_TILE)],
                                     pbuf.at[pl.ds(s * PAIRS_PER_TILE, PAIRS_PER_TILE)], psem.at[s])

    @pl.when(i == 0)
    def _():
        chunk(0, 0).start()

    chunk(i, slot).wait()

    @pl.when(i + 1 < pl.num_programs(0))
    def _():
        chunk(i + 1, 1 - slot).start()

    return slot * PAIRS_PER_TILE


def _dispatch_kernel(pad_lo_ref, pad_hi_ref, pos_ref, h2_ref, xs_ref, pbuf, zbuf, psem, dsem, zsem):
    i = pl.program_id(0)
    nch = h2_ref.shape[0] // TM
    n_tails = pad_lo_ref.shape[0] - 1

    def zero_slots(first, n):
        return pltpu.make_async_copy(zbuf.at[pl.ds(0, n * nch)],
                                     xs_ref.at[pl.ds(pl.multiple_of(first * nch, nch), n * nch)], zsem)

    def over_padding(act):
        def tail(e, carry):
            lo = pad_lo_ref[e]
            n = pad_hi_ref[e] - lo
            for bit in range(TME.bit_length() - 1):
                size = 1 << bit

                @pl.when((n & size) != 0)
                def _():
                    act(zero_slots(lo + (n & -(2 * size)), size))
            return carry

        lax.fori_loop(0, n_tails, tail, 0)

        def tile(j, c):
            act(zero_slots(j * TME, TME))
            return c

        lax.fori_loop(pad_lo_ref[n_tails] // TME, pad_hi_ref[n_tails] // TME, tile, 0)

    @pl.when(i == 0)
    def _():
        zbuf[...] = jnp.zeros(zbuf.shape, zbuf.dtype)
        over_padding(lambda cp: cp.start())
        over_padding(lambda cp: cp.wait())

    off = _stage_pos(pos_ref, pbuf, psem)
    for r in range(TM):
        for kk in range(TOP_K):
            dst = pl.multiple_of(pbuf[off + r * TOP_K + kk] * nch, nch)
            pltpu.make_async_copy(h2_ref.at[pl.ds(r * nch, nch)], xs_ref.at[pl.ds(dst, nch)], dsem).start(
                priority=kk % 2)
    for _ in range(TOP_K):
        pltpu.make_async_copy(h2_ref, xs_ref.at[pl.ds(0, TM * nch)], dsem).wait()


def _dispatch(pos, pad_lo, pad_hi, h2, n_slots, d):
    nch = d // LANES
    n_tok = h2.shape[0] // nch
    grid_spec = pltpu.PrefetchScalarGridSpec(
        num_scalar_prefetch=2,
        grid=(n_tok // TM,),
        in_specs=[pl.BlockSpec(memory_space=pl.ANY),
                  pl.BlockSpec((TM * nch, LANES), lambda i, lo, hi: (i, 0))],
        out_specs=pl.BlockSpec(memory_space=pl.ANY),
        scratch_shapes=[pltpu.SMEM((2 * PAIRS_PER_TILE,), I32), pltpu.VMEM((TME * nch, LANES), F32),
                        pltpu.SemaphoreType.DMA((2,)), pltpu.SemaphoreType.DMA, pltpu.SemaphoreType.DMA],
    )
    return pl.pallas_call(
        _dispatch_kernel,
        grid_spec=grid_spec,
        out_shape=jax.ShapeDtypeStruct((n_slots * nch, LANES), F32),
        compiler_params=_cparams(("arbitrary",)),
        name="expert_dispatch",
    )(pad_lo, pad_hi, pos, h2)


def _moe_kernel(te_ref, nx_ref, par_ref, meta_ref, x_ref, w1_hbm, b1_ref, w2_hbm, b2_ref, y_ref,
                w1f, w2f, w1b, w2b, wsem, *, layer):
    i = pl.program_id(0)
    n_tiles = meta_ref[0]
    d = w1b.shape[0]
    ff = w2b.shape[0]
    nch = d // LANES

    def weight_copies(e, s):
        return (pltpu.make_async_copy(w1_hbm.at[layer, e], w1f.at[s], wsem.at[0, s]),
                pltpu.make_async_copy(w2_hbm.at[layer, e], w2f.at[s], wsem.at[1, s]))

    @pl.when(i == 0)
    def _():
        for cp in weight_copies(te_ref[0], par_ref[0]):
            cp.start()

    @pl.when(i < n_tiles)
    def _():
        new_expert = jnp.logical_or(i == 0, te_ref[i] != te_ref[jnp.maximum(i - 1, 0)])

        @pl.when(new_expert)
        def _():
            s = par_ref[i]
            for cp in weight_copies(te_ref[i], s):
                cp.wait()
            w1b[...] = w1f[s].astype(BF16)
            w2b[...] = w2f[s].astype(BF16)

            @pl.when(nx_ref[i] >= 0)
            def _():
                for cp in weight_copies(nx_ref[i], 1 - s):
                    cp.start()

        x = jnp.concatenate([x_ref[pl.ds(c, TME, stride=nch), :] for c in range(nch)], axis=1)
        h = _dot(x.astype(BF16), w1b[...]) + b1_ref[0]
        glu = jnp.minimum(h[:, :ff], SWIGLU_LIMIT)
        lin = jnp.clip(h[:, ff:], -SWIGLU_LIMIT, SWIGLU_LIMIT)
        act = glu * _sigmoid(SWIGLU_ALPHA * glu) * (lin + 1.0)
        y = _dot(act.astype(BF16), w2b[...]) + b2_ref[0]
        for c in range(nch):
            y_ref[pl.ds(c, TME, stride=nch), :] = y[:, c * LANES:(c + 1) * LANES]

    @pl.when(i >= n_tiles)
    def _():
        y_ref[...] = jnp.zeros(y_ref.shape, y_ref.dtype)


def _routed_experts(tile_expert, next_expert, parity, meta, xs, w1, b1, w2, b2, layer):
    _, n_exp, d, ff2 = w1.shape
    ff = w2.shape[2]
    nch = d // LANES
    n_grid = tile_expert.shape[0]
    used = lambda i, mt: jnp.minimum(i, mt[0] - 1)
    bmap = lambda i, te, nx, pr, mt: (layer * n_exp + te[used(i, mt)], 0, 0)
    grid_spec = pltpu.PrefetchScalarGridSpec(
        num_scalar_prefetch=4,
        grid=(n_grid,),
        in_specs=[
            pl.BlockSpec((TME * nch, LANES), lambda i, te, nx, pr, mt: (used(i, mt), 0)),
            pl.BlockSpec(memory_space=pl.ANY),
            pl.BlockSpec((1, 1, ff2), bmap),
            pl.BlockSpec(memory_space=pl.ANY),
            pl.BlockSpec((1, 1, d), bmap),
        ],
        out_specs=pl.BlockSpec((TME * nch, LANES), lambda i, te, nx, pr, mt: (i, 0)),
        scratch_shapes=[pltpu.VMEM((2, d, ff2), F32), pltpu.VMEM((2, ff, d), F32),
                        pltpu.VMEM((d, ff2), BF16), pltpu.VMEM((ff, d), BF16), pltpu.SemaphoreType.DMA((2, 2))],
    )
    return pl.pallas_call(
        functools.partial(_moe_kernel, layer=layer),
        grid_spec=grid_spec,
        out_shape=jax.ShapeDtypeStruct(xs.shape, F32),
        compiler_params=_cparams(("arbitrary",)),
        name="routed_experts",
    )(tile_expert, next_expert, parity, meta, xs, w1, b1.reshape(-1, 1, ff2), w2, b2.reshape(-1, 1, d))


def _gather_choices(pos_ref, ys_ref, ybuf, pbuf, psem, csem):
    i = pl.program_id(0)
    n = pl.num_programs(0)
    slot = i % 2
    nch = ybuf.shape[2] // TM

    def chunk(ti, s):
        return pltpu.make_async_copy(pos_ref.at[pl.ds(ti * PAIRS_PER_TILE, PAIRS_PER_TILE)],
                                     pbuf.at[pl.ds(s * PAIRS_PER_TILE, PAIRS_PER_TILE)], psem.at[s])

    def row_copy(s, r, kk):
        src = pl.multiple_of(pbuf[s * PAIRS_PER_TILE + r * TOP_K + kk] * nch, nch)
        return pltpu.make_async_copy(ys_ref.at[pl.ds(src, nch)], ybuf.at[s, kk, pl.ds(r * nch, nch)], csem.at[s])

    @pl.when(i == 0)
    def _():
        chunk(0, 0).start()
        chunk(0, 0).wait()

        def first_rows(r, c):
            for kk in range(TOP_K):
                row_copy(0, r, kk).start(priority=kk % 2)
            return c

        lax.fori_loop(0, TM, first_rows, 0)

        @pl.when(n > 1)
        def _():
            chunk(1, 1).start()

    @pl.when(i + 1 < n)
    def _():
        chunk(i + 1, 1 - slot).wait()
        for r in range(TM):
            for kk in range(TOP_K):
                row_copy(1 - slot, r, kk).start(priority=kk % 2)

        @pl.when(i + 2 < n)
        def _():
            chunk(i + 2, slot).start()

    for kk in range(TOP_K):
        pltpu.make_async_copy(ys_ref.at[pl.ds(0, TM * nch)], ybuf.at[slot, kk], csem.at[slot]).wait()
    return slot


def _gather_scratch(d):
    nch = d // LANES
    return [pltpu.VMEM((2, TOP_K, TM * nch, LANES), F32), pltpu.SMEM((2 * PAIRS_PER_TILE,), I32),
            pltpu.SemaphoreType.DMA((2,)), pltpu.SemaphoreType.DMA((2,))]


def _plan_routes(route, counts, n_grid):
    n_exp = counts.shape[-1]
    cnt = counts.reshape(n_exp).astype(I32)
    tiles_per = (cnt + TME - 1) // TME
    tile_end = jnp.cumsum(tiles_per)
    offset = (tile_end - tiles_per) * TME
    n_tiles = tile_end[-1]
    tile_ids = jnp.minimum(jnp.arange(n_grid, dtype=I32), n_tiles - 1)
    tile_expert = jnp.sum((tile_ids[:, None] >= tile_end[None, :]).astype(I32), axis=1)
    expert_ids = jnp.arange(n_exp, dtype=I32)
    of_tile = tile_expert[:, None] == expert_ids
    later = jnp.logical_and(expert_ids[None, :] > expert_ids[:, None], (tiles_per > 0)[None, :])
    next_of = jnp.min(jnp.where(later, expert_ids[None, :], n_exp), axis=1)
    next_of = jnp.where(next_of == n_exp, -1, next_of)
    next_expert = jnp.sum(jnp.where(of_tile, next_of, 0), axis=1)
    parity = jnp.sum(jnp.where(of_tile, jnp.cumsum((tiles_per > 0).astype(I32)), 0), axis=1) % 2
    eidx = route[:, :TOP_K].astype(I32)
    rank = route[:, TOP_K:2 * TOP_K].astype(I32)
    pos = jnp.sum(jnp.where(eidx[..., None] == expert_ids, offset, 0), axis=-1) + rank
    pad_lo = jnp.concatenate([offset + cnt, (n_tiles * TME).reshape(1)])
    pad_hi = jnp.concatenate([tile_end * TME, jnp.full((1,), n_grid * TME, I32)])
    return (tile_expert, next_expert, parity, n_tiles.reshape(1)), pos.reshape(-1), pad_lo, pad_hi


def _moe_layer(route, counts, h2, w1, b1, w2, b2, layer):
    n_exp, d = w1.shape[1], w1.shape[2]
    n_tok = route.shape[0]
    n_grid = n_tok * TOP_K // TME + n_exp
    tiles, pos, pad_lo, pad_hi = _plan_routes(route, counts, n_grid)
    xs = _dispatch(pos, pad_lo, pad_hi, h2, n_grid * TME, d)
    return pos, _routed_experts(*tiles, xs, w1, b1, w2, b2, layer)


def _combine(ybuf, slot, route_ref, d):
    nch = d // LANES
    gates = route_ref[...]
    gk = [jnp.broadcast_to(gates[:, 2 * TOP_K + kk:2 * TOP_K + kk + 1], (TM, LANES)) for kk in range(TOP_K)]
    chunks = []
    for c in range(nch):
        acc = gk[0] * ybuf[slot, 0, pl.ds(c, TM, stride=nch), :]
        for kk in range(1, TOP_K):
            acc = acc + gk[kk] * ybuf[slot, kk, pl.ds(c, TM, stride=nch), :]
        chunks.append(acc)
    return jnp.concatenate(chunks, axis=1)


def _in1_kernel(x_ref, pos_ref, ys_ref, route_ref, mod0_ref, mod1_ref, g_ref, w_ref, x2_ref, gg_ref, u_ref,
                ybuf, pbuf, psem, csem):
    slot = _gather_choices(pos_ref, ys_ref, ybuf, pbuf, psem, csem)
    x2 = x_ref[...] + mod0_ref[0][5:6] * _combine(ybuf, slot, route_ref, x_ref.shape[1])
    x2_ref[...] = x2
    mod1 = mod1_ref[0]
    h = _norm_mod(x2, g_ref[...], mod1[0:1], mod1[1:2])
    y = _dot(h.astype(BF16), w_ref[...])
    half = y.shape[1] // 2
    gg_ref[...] = jax.nn.gelu(y[:, :half]).astype(BF16)
    u_ref[...] = y[:, half:]


def _in_proj1(x, pos, ys, route, modv0, modv1, g, w, ntb):
    t, d = x.shape
    n = w.shape[1]
    row = lambda i: (i, 0)
    mod_map = lambda i: ((i // ntb) * 2 + jnp.minimum(i % ntb, 1), 0, 0)
    return pl.pallas_call(
        _in1_kernel,
        grid=(t // TM,),
        in_specs=[
            pl.BlockSpec((TM, d), row),
            pl.BlockSpec(memory_space=pl.ANY),
            pl.BlockSpec(memory_space=pl.ANY),
            pl.BlockSpec((TM, LANES), row),
            pl.BlockSpec((1, 6, d), mod_map),
            pl.BlockSpec((1, 6, d), mod_map),
            pl.BlockSpec((1, d), lambda i: (0, 0)),
            pl.BlockSpec((d, n), lambda i: (0, 0)),
        ],
        out_specs=[pl.BlockSpec((TM, d), row), pl.BlockSpec((TM, n // 2), row), pl.BlockSpec((TM, n // 2), row)],
        out_shape=[jax.ShapeDtypeStruct((t, d), F32), jax.ShapeDtypeStruct((t, n // 2), BF16),
                   jax.ShapeDtypeStruct((t, n // 2), F32)],
        scratch_shapes=_gather_scratch(d),
        compiler_params=_cparams(("arbitrary",)),
        name="l1_combine_in_proj",
    )(x, pos, ys, route, modv0, modv1, g, w)


def _rglru_kernel(u_ref, halo_ref, cw_ref, cb_ref, gaw_ref, gab_ref, gxw_ref, gxb_ref, lam_ref, o_ref,
                  ext, a_s, b_s, h_s, state, *, ntb):
    d = pl.program_id(0)
    s = pl.program_id(2)
    nb, tc, cw_ = u_ref.shape
    u = u_ref[...]
    cw = cw_ref[0]

    def finish(xc, reverse):
        xc2 = xc.reshape(nb * tc, cw_) + cb_ref[0]
        xb = xc2.astype(BF16)
        t_r = jnp.tanh(_dot(xb, gaw_ref[0, 0]) + gab_ref[0])
        t_i = jnp.tanh(_dot(xb, gxw_ref[0, 0]) + gxb_ref[0])
        nl = -lam_ref[0]
        softplus = jnp.maximum(nl, 0.0) + jnp.log1p(jnp.exp(-jnp.abs(nl)))
        half_rate = (-0.5 * LRU_C) * softplus
        log_a = half_rate * t_r + half_rate
        a = jnp.exp(log_a)
        half_x = 0.5 * xc2
        bb = jnp.sqrt(1.0 - a * a) * (half_x * t_i + half_x)
        n_lane = cw_ // LANES
        pitch = tc + SCAN_PAD
        for c in range(n_lane):
            for bi in range(nb):
                a_s[c, bi * pitch:bi * pitch + tc, :] = a[bi * tc:(bi + 1) * tc, c * LANES:(c + 1) * LANES]
                b_s[c, bi * pitch:bi * pitch + tc, :] = bb[bi * tc:(bi + 1) * tc, c * LANES:(c + 1) * LANES]

        @pl.when(s == 0)
        def _():
            state[...] = jnp.zeros_like(state)

        def steps(tb, hs):
            hs = list(hs)
            for uu in range(SCAN_UNROLL):
                tt = tb * SCAN_UNROLL + uu
                t = (tc - 1 - tt) if reverse else tt
                rows = pl.ds(t, nb, stride=pitch)
                for c in range(n_lane):
                    hs[c] = a_s[c, rows, :] * hs[c] + b_s[c, rows, :]
                    h_s[c, rows, :] = hs[c]
            return tuple(hs)

        hs = lax.fori_loop(0, tc // SCAN_UNROLL, steps, tuple(state[c] for c in range(n_lane)))
        for c in range(n_lane):
            state[c] = hs[c]
            for bi in range(nb):
                o_ref[0, bi, :, c * LANES:(c + 1) * LANES] = h_s[c, bi * pitch:bi * pitch + tc, :].astype(o_ref.dtype)

    @pl.when(d == 0)
    def _():
        chunk = s
        keep = chunk > 1
        ext[:, 0:SUBLANES, :] = jnp.where(keep, halo_ref[...], 0.0)
        ext[:, SUBLANES:, :] = u
        xc = cw[3:4] * u
        for j in range(1, C_CONV):
            xc = xc + cw[3 - j:4 - j] * ext[:, SUBLANES - j:SUBLANES - j + tc, :]
        finish(xc, False)

    @pl.when(d == 1)
    def _():
        chunk = jnp.where(s == 0, 0, ntb - s)
        keep = jnp.logical_and(chunk > 0, chunk < ntb - 1)
        ext[:, 0:tc, :] = u
        ext[:, tc:, :] = jnp.where(keep, halo_ref[...], 0.0)
        xc = cw[0:1] * u
        for j in range(1, C_CONV):
            xc = xc + cw[j:j + 1] * ext[:, j:j + tc, :]
        finish(xc, True)


def _rglru(u, conv_w, conv_b, ga_w, ga_b, gx_w, gx_b, lam, b, l):
    width = u.shape[1]
    cb = width // C_BLOCKS
    ntb = l // TM
    u3 = u.reshape(b, l, width)
    nblk = TM // SUBLANES

    def chunk_of(d, s):
        return jnp.where(d == 0, s, jnp.where(s == 0, 0, ntb - s))

    def halo_of(d, s):
        c = chunk_of(d, s)
        return jnp.where(d == 0, jnp.maximum(c * nblk - 1, 0), jnp.minimum((c + 1) * nblk, l // SUBLANES - 1))

    vec = pl.BlockSpec((1, 1, cb), lambda d, g, s: (d, 0, g))
    mat = pl.BlockSpec((1, 1, cb, cb), lambda d, g, s: (d, g, 0, 0))
    return pl.pallas_call(
        functools.partial(_rglru_kernel, ntb=ntb),
        grid=(2, C_BLOCKS, ntb),
        in_specs=[
            pl.BlockSpec((b, TM, cb), lambda d, g, s: (0, chunk_of(d, s), g)),
            pl.BlockSpec((b, SUBLANES, cb), lambda d, g, s: (0, halo_of(d, s), g)),
            pl.BlockSpec((1, C_CONV, cb), lambda d, g, s: (d, 0, g)),
            vec, mat, vec, mat, vec, vec,
        ],
        out_specs=pl.BlockSpec((1, b, TM, cb), lambda d, g, s: (d, 0, chunk_of(d, s), g)),
        out_shape=jax.ShapeDtypeStruct((2, b, l, width), BF16),
        scratch_shapes=[
            pltpu.VMEM((b, TM + SUBLANES, cb), F32),
            pltpu.VMEM((cb // LANES, b * (TM + SCAN_PAD), LANES), F32),
            pltpu.VMEM((cb // LANES, b * (TM + SCAN_PAD), LANES), F32),
            pltpu.VMEM((cb // LANES, b * (TM + SCAN_PAD), LANES), F32),
            pltpu.VMEM((cb // LANES, b, LANES), F32),
        ],
        compiler_params=_cparams(("arbitrary", "arbitrary", "arbitrary")),
        name="l1_rglru",
    )(u3, u3, conv_w, conv_b.reshape(2, 1, width), (0.5 * ga_w).astype(BF16), (0.5 * ga_b).reshape(2, 1, width),
      (0.5 * gx_w).astype(BF16), (0.5 * gx_b).reshape(2, 1, width), lam.reshape(2, 1, width))


def _final_kernel(x_ref, pos_ref, ys_ref, route_ref, mod_ref, g_ref, o_ref, ybuf, pbuf, psem, csem):
    slot = _gather_choices(pos_ref, ys_ref, ybuf, pbuf, psem, csem)
    x = x_ref[...] + mod_ref[0][5:6] * _combine(ybuf, slot, route_ref, x_ref.shape[1])
    ms = jnp.mean(x * x, axis=-1, keepdims=True)
    o_ref[...] = x * lax.rsqrt(ms + NORM_EPS) * g_ref[...]


def _final(x, pos, ys, route, modv, g, b, s_len):
    t, d = x.shape
    per_b = s_len // TM
    row = lambda i: (i, 0)
    return pl.pallas_call(
        _final_kernel,
        grid=(t // TM,),
        in_specs=[
            pl.BlockSpec((TM, d), row),
            pl.BlockSpec(memory_space=pl.ANY),
            pl.BlockSpec(memory_space=pl.ANY),
            pl.BlockSpec((TM, LANES), row),
            pl.BlockSpec((1, 6, d), lambda i: ((i // per_b) * 2 + 1, 0, 0)),
            pl.BlockSpec((1, d), lambda i: (0, 0)),
        ],
        out_specs=pl.BlockSpec((TM, d), row),
        out_shape=jax.ShapeDtypeStruct((t, d), F32),
        scratch_shapes=_gather_scratch(d),
        compiler_params=_cparams(("arbitrary",)),
        name="final_combine_norm",
    )(x, pos, ys, route, modv, g)


def _rope_tables(s_len, n_ctx):
    n_rows = s_len // GRID_W
    rows, cols = jnp.meshgrid(jnp.arange(n_rows), jnp.arange(GRID_W), indexing="ij")
    pos = jnp.stack([rows.reshape(-1), cols.reshape(-1)], axis=-1).astype(F32)
    n_freq = HEAD_DIM // 4
    inv = ROPE_THETA ** (-jnp.arange(n_freq, dtype=F32) / n_freq)
    ang = pos[:, :, None] * inv
    cos, sin = jnp.cos(ang), jnp.sin(ang)
    cos64 = jnp.stack([cos, cos], axis=2).reshape(s_len, HEAD_DIM)
    sin64 = jnp.stack([-sin, sin], axis=2).reshape(s_len, HEAD_DIM)
    cos_l = jnp.tile(cos64, (1, LANES // HEAD_DIM))
    sin_l = jnp.tile(sin64, (1, LANES // HEAD_DIM))
    cos_t = jnp.concatenate([jnp.ones((n_ctx, LANES), F32), cos_l], axis=0)
    sin_t = jnp.concatenate([jnp.zeros((n_ctx, LANES), F32), sin_l], axis=0)
    return cos_t, sin_t


def kernel(x, c, ctx, c_ctx, w_mod, b_mod, norm_mix, norm_ffn, ev_w_in, ev_w_out, ev_lambda_q1, ev_lambda_k1, ev_lambda_q2, ev_lambda_k2, ev_subln, ev_conv_w, od_w_in, od_w_out, od_conv_w, od_conv_b, od_gate_a_w, od_gate_a_b, od_gate_x_w, od_gate_x_b, od_lru_lambda, moe_w_router, moe_b_router, moe_w1, moe_b1, moe_w2, moe_b2, final_norm):
    b, s_len, d = x.shape
    n_ctx = ctx.shape[1]
    l = n_ctx + s_len
    t = b * l
    ntb = l // TM
    assert n_ctx == TM and s_len % TM == 0 and w_mod.shape[0] == 2

    ctx2 = ctx.reshape(b * n_ctx, d)
    x2d = x.reshape(b * s_len, d)

    n_rows = -(-(b + 1) // SUBLANES) * SUBLANES
    cs = jnp.concatenate([c, c_ctx[None, :], jnp.zeros((n_rows - b - 1, d), F32)], axis=0)
    mod = _modulation(cs, w_mod, b_mod)

    def mod_table(i):
        lat = mod[i, :b]
        cx = jnp.broadcast_to(mod[i, b][None, :], lat.shape)
        return jnp.stack([cx, lat], axis=1).reshape(b * 2, 6, d)

    modv0, modv1 = mod_table(0), mod_table(1)

    cos_t, sin_t = _rope_tables(s_len, n_ctx)
    q, k, v, bg, p = _in_proj0(ctx2, x2d, modv0, norm_mix[0:1], ev_w_in[0].astype(BF16), cos_t, sin_t, ntb)
    lam_init = 0.8 - 0.6 * math.exp(-0.3 * 0)
    attn = _diff_attention(q, k, v, ev_lambda_q1[0:1], ev_lambda_k1[0:1], ev_lambda_q2[0:1], ev_lambda_k2[0:1],
                           ev_subln[0:1], lam_init, b, l, n_ctx)
    x1, h2, route0, counts0 = _out_proj0(attn, bg, p, ev_conv_w[0], ev_w_out[0].astype(BF16), ctx2, x2d, modv0,
                                         norm_ffn[0:1], moe_w_router[0], moe_b_router[0:1], ntb)
    pos0, ys0 = _moe_layer(route0, counts0, h2, moe_w1, moe_b1, moe_w2, moe_b2, 0)

    x2, gg, u = _in_proj1(x1, pos0, ys0, route0, modv0, modv1, norm_mix[1:2], od_w_in[0].astype(BF16), ntb)
    hs = _rglru(u, od_conv_w[0], od_conv_b[0], od_gate_a_w[0], od_gate_a_b[0], od_gate_x_w[0], od_gate_x_b[0],
                od_lru_lambda[0], b, l)
    x3, h3, route1, counts1 = _out_proj1(hs.reshape(2, t, hs.shape[-1]), gg, od_w_out[0].astype(BF16), x2, modv1,
                                         norm_ffn[1:2], moe_w_router[1], moe_b_router[1:2], ntb)
    pos1, ys1 = _moe_layer(route1, counts1, h3, moe_w1, moe_b1, moe_w2, moe_b2, 1)

    out = _final(x3, pos1, ys1, route1, modv1, final_norm[None, :], b, s_len)
    return out.reshape(b, s_len, d)
```

```python
import functools
import math

import jax
import jax.numpy as jnp
from jax import lax
from jax.experimental import pallas as pl
from jax.experimental.pallas import tpu as pltpu

F32 = jnp.float32
BF16 = jnp.bfloat16
I32 = jnp.int32

NORM_EPS = 1e-6
ROPE_THETA = 10000.0
GRID_W = 64
N_HEADS = 4
HEAD_DIM = 64
A_WIDTH = 2 * N_HEADS * HEAD_DIM
B_CONV = 3
C_CONV = 4
C_BLOCKS = 4
LRU_C = 8.0
TOP_K = 4
SWIGLU_ALPHA = 1.702
SWIGLU_LIMIT = 7.0

LANES = 128
SUBLANES = 8
TM = 256
TME = 512
VMEM_LIMIT = 56 * 1024 * 1024
SCAN_PAD = 8
SCAN_UNROLL = 8
LOG2_E = 1.4426950408889634


def _cparams(sem, vmem=VMEM_LIMIT):
    return pltpu.CompilerParams(dimension_semantics=sem, vmem_limit_bytes=vmem)


def _norm_mod(x, g, shift, scale):
    ms = jnp.mean(x * x, axis=-1, keepdims=True)
    return (x * lax.rsqrt(ms + NORM_EPS) * g) * (1.0 + scale) + shift


def _dot(a, b):
    return jnp.dot(a, b, preferred_element_type=F32)


def _sigmoid(x):
    return 0.5 * jnp.tanh(0.5 * x) + 0.5


def _mod_kernel(cs_ref, w_ref, b_ref, o_ref):
    s = cs_ref[...]
    s = s * jax.nn.sigmoid(s)
    o_ref[0] = _dot(s.astype(BF16), w_ref[0].astype(BF16)) + b_ref[0]


def _modulation(cs, w_mod, b_mod):
    depth, d, n = w_mod.shape
    rows = cs.shape[0]
    tn = 1536
    return pl.pallas_call(
        _mod_kernel,
        grid=(depth, n // tn),
        in_specs=[
            pl.BlockSpec((rows, d), lambda i, j: (0, 0)),
            pl.BlockSpec((1, d, tn), lambda i, j: (i, 0, j)),
            pl.BlockSpec((1, 1, tn), lambda i, j: (i, 0, j)),
        ],
        out_specs=pl.BlockSpec((1, rows, tn), lambda i, j: (i, 0, j)),
        out_shape=jax.ShapeDtypeStruct((depth, rows, n), F32),
        compiler_params=_cparams(("parallel", "parallel")),
        name="adaln_modulation",
    )(cs, w_mod, b_mod.reshape(depth, 1, n))


def _stream_tile(ctx_ref, x_ref, ntb):
    return jnp.where(pl.program_id(0) % ntb == 0, ctx_ref[...], x_ref[...])


def _stream_specs(d, ntb):
    return [pl.BlockSpec((TM, d), lambda i: (i // ntb, 0)),
            pl.BlockSpec((TM, d), lambda i: ((i // ntb) * (ntb - 1) + jnp.maximum(i % ntb - 1, 0), 0))]


def _in0_kernel(ctx_ref, x_ref, mod_ref, g_ref, w_ref, cos_ref, sin_ref,
                q_ref, k_ref, v_ref, bg_ref, p_ref, *, ntb):
    mod = mod_ref[0]
    h = _norm_mod(_stream_tile(ctx_ref, x_ref, ntb), g_ref[...], mod[0:1], mod[1:2])
    y = _dot(h.astype(BF16), w_ref[...])
    cosv = cos_ref[...]
    sinv = sin_ref[...]
    lane = lax.broadcasted_iota(I32, (TM, LANES), 1)
    first_half = (lane & 16) == 0

    def rope(z):
        outs = []
        for g in range(A_WIDTH // LANES):
            zg = z[:, g * LANES:(g + 1) * LANES]
            partner = jnp.where(first_half, pltpu.roll(zg, LANES - 16, 1), pltpu.roll(zg, 16, 1))
            outs.append(zg * cosv + partner * sinv)
        return jnp.concatenate(outs, axis=1)

    aw = A_WIDTH
    q_ref[...] = (rope(y[:, :aw]) * (HEAD_DIM ** -0.5 * LOG2_E)).astype(BF16)
    k_ref[...] = rope(y[:, aw:2 * aw]).astype(BF16)
    v_ref[...] = y[:, 2 * aw:3 * aw].astype(BF16)
    bw = (y.shape[1] - 3 * aw) // 3
    bg_ref[...] = y[:, 3 * aw:3 * aw + bw].astype(BF16)
    p_ref[...] = (y[:, 3 * aw + bw:3 * aw + 2 * bw] * y[:, 3 * aw + 2 * bw:]).astype(BF16)


def _in_proj0(ctx, x, modv, g, w, cos_t, sin_t, ntb):
    d = x.shape[1]
    t = ctx.shape[0] + x.shape[0]
    n = w.shape[1]
    bw = (n - 3 * A_WIDTH) // 3
    row = lambda i: (i, 0)
    return pl.pallas_call(
        functools.partial(_in0_kernel, ntb=ntb),
        grid=(t // TM,),
        in_specs=_stream_specs(d, ntb) + [
            pl.BlockSpec((1, 6, d), lambda i: ((i // ntb) * 2 + jnp.minimum(i % ntb, 1), 0, 0)),
            pl.BlockSpec((1, d), lambda i: (0, 0)),
            pl.BlockSpec((d, n), lambda i: (0, 0)),
            pl.BlockSpec((TM, LANES), lambda i: (i % ntb, 0)),
            pl.BlockSpec((TM, LANES), lambda i: (i % ntb, 0)),
        ],
        out_specs=[pl.BlockSpec((TM, A_WIDTH), row)] * 3 + [pl.BlockSpec((TM, bw), row)] * 2,
        out_shape=[jax.ShapeDtypeStruct((t, A_WIDTH), BF16)] * 3 + [jax.ShapeDtypeStruct((t, bw), BF16)] * 2,
        compiler_params=_cparams(("parallel",)),
        name="l0_in_proj_rope",
    )(ctx, x, modv, g, w, cos_t, sin_t)


def _attn_kernel(lq1_ref, lk1_ref, lq2_ref, lk2_ref, g_ref, q_ref, k_ref, v_ref, o_ref, *, lam_init, n_ctx):
    qi = pl.program_id(2)
    lam = (jnp.exp(jnp.sum(lq1_ref[...] * lk1_ref[...], axis=-1, keepdims=True))
           - jnp.exp(jnp.sum(lq2_ref[...] * lk2_ref[...], axis=-1, keepdims=True)) + lam_init)
    q = q_ref[...]
    lane = lax.broadcasted_iota(I32, q.shape, 1)
    zero = jnp.zeros_like(q)
    q1 = jnp.where(lane < HEAD_DIM, q, zero)
    q2 = jnp.where(lane < HEAD_DIM, zero, q)
    contract_last = (((1,), (1,)), ((), ()))

    def attend(nk):
        k = k_ref[0, :nk, :]
        v = v_ref[0, :nk, :]
        s1 = lax.dot_general(q1, k, contract_last, preferred_element_type=F32)
        s2 = lax.dot_general(q2, k, contract_last, preferred_element_type=F32)
        p1 = jnp.exp2(s1 - jnp.max(s1, axis=-1, keepdims=True))
        p2 = jnp.exp2(s2 - jnp.max(s2, axis=-1, keepdims=True))
        r1 = 1.0 / jnp.sum(p1, axis=-1, keepdims=True)
        r2 = lam / jnp.sum(p2, axis=-1, keepdims=True)
        o = _dot(p1.astype(BF16), v) * r1 - _dot(p2.astype(BF16), v) * r2
        ms = jnp.mean(o * o, axis=-1, keepdims=True)
        o = o * lax.rsqrt(ms + NORM_EPS) * g_ref[...] * (1.0 - lam_init)
        o_ref[...] = o.astype(BF16)

    @pl.when(qi == 0)
    def _():
        attend(n_ctx)

    @pl.when(qi > 0)
    def _():
        attend(k_ref.shape[1])


def _diff_attention(q, k, v, lq1, lk1, lq2, lk2, subln, lam_init, b, l, n_ctx):
    t = q.shape[0]
    ntb = l // TM
    hw = 2 * HEAD_DIM
    k3 = k.reshape(b, l, A_WIDTH)
    v3 = v.reshape(b, l, A_WIDTH)
    vec = lambda n: pl.BlockSpec((1, n), lambda bi, h, qi: (0, 0))
    qspec = pl.BlockSpec((TM, hw), lambda bi, h, qi: (bi * ntb + qi, h))
    kspec = pl.BlockSpec((1, l, hw), lambda bi, h, qi: (bi, 0, h))
    return pl.pallas_call(
        functools.partial(_attn_kernel, lam_init=lam_init, n_ctx=n_ctx),
        grid=(b, N_HEADS, ntb),
        in_specs=[vec(HEAD_DIM)] * 4 + [vec(hw), qspec, kspec, kspec],
        out_specs=qspec,
        out_shape=jax.ShapeDtypeStruct((t, A_WIDTH), BF16),
        compiler_params=_cparams(("parallel", "parallel", "parallel")),
        name="l0_diff_attention",
    )(lq1, lk1, lq2, lk2, subln, q, k3, v3)


def _route_tail(y, x_in, mod_ref, g_ref, wr_ref, br_ref,
                x1_ref, h2_ref, route_ref, cnt_ref, carry_ref):
    mod = mod_ref[0]
    x1 = x_in + mod[2:3] * y
    x1_ref[...] = x1
    h2 = _norm_mod(x1, g_ref[...], mod[3:4], mod[4:5])
    nch = h2.shape[1] // LANES
    for s in range(nch):
        h2_ref[pl.ds(s, TM, stride=nch), :] = h2[:, s * LANES:(s + 1) * LANES]

    logits = _dot(h2.astype(BF16), wr_ref[...].astype(BF16)) + br_ref[...]

    n_exp = logits.shape[1]
    lane = lax.broadcasted_iota(I32, logits.shape, 1).astype(F32)
    work = logits
    sels, vals, idxs = [], [], []
    for _ in range(TOP_K):
        m = jnp.max(work, axis=-1, keepdims=True)
        idx = jnp.min(jnp.where(work == m, lane, float(n_exp)), axis=-1, keepdims=True)
        sel = lane == idx
        sels.append(sel)
        vals.append(m)
        idxs.append(idx)
        work = jnp.where(sel, -jnp.inf, work)
    exps = [jnp.exp(vv - vals[0]) for vv in vals]
    inv_den = 1.0 / (exps[0] + exps[1] + exps[2] + exps[3])

    chosen = jnp.zeros(logits.shape, F32)
    for sel in sels:
        chosen = chosen + jnp.where(sel, 1.0, 0.0)
    r_i = lax.broadcasted_iota(I32, (TM, TM), 0)
    c_i = lax.broadcasted_iota(I32, (TM, TM), 1)
    earlier = jnp.where(c_i < r_i, 1.0, 0.0).astype(BF16)
    rank = _dot(earlier, chosen.astype(BF16)) + carry_ref[...]
    carry_ref[...] = carry_ref[...] + jnp.sum(chosen, axis=0, keepdims=True)
    cnt_ref[...] = carry_ref[...]

    out_lane = lax.broadcasted_iota(I32, (TM, LANES), 1)
    packed = jnp.zeros((TM, LANES), F32)
    for kk in range(TOP_K):
        rank_k = jnp.sum(jnp.where(sels[kk], rank, 0.0), axis=-1, keepdims=True)
        packed = jnp.where(out_lane == kk, idxs[kk], packed)
        packed = jnp.where(out_lane == TOP_K + kk, rank_k, packed)
        packed = jnp.where(out_lane == 2 * TOP_K + kk, exps[kk] * inv_den, packed)
    route_ref[...] = packed


def _out0_kernel(attn_ref, bg_ref, p_ref, pprev_ref, pnext_ref, cw_ref, wo_ref,
                 ctx_ref, x_ref, mod_ref, g_ref, wr_ref, br_ref,
                 x1_ref, h2_ref, route_ref, cnt_ref, carry_ref, *, ntb):
    i = pl.program_id(0)
    seg = i % ntb

    @pl.when(i == 0)
    def _():
        carry_ref[...] = jnp.zeros_like(carry_ref)

    p = p_ref[...].astype(F32)
    row = lax.broadcasted_iota(I32, p.shape, 0)
    has_prev = seg > 1
    has_next = jnp.logical_and(seg > 0, seg < ntb - 1)
    prev_row = jnp.where(has_prev, pprev_ref[SUBLANES - 1:SUBLANES, :].astype(F32), 0.0)
    next_row = jnp.where(has_next, pnext_ref[0:1, :].astype(F32), 0.0)
    before = jnp.where(row == 0, prev_row, pltpu.roll(p, 1, 0))
    after = jnp.where(row == TM - 1, next_row, pltpu.roll(p, TM - 1, 0))
    cw = cw_ref[...]
    conv = bg_ref[...].astype(F32) * (cw[0:1] * before + cw[1:2] * p + cw[2:3] * after)
    aw = attn_ref.shape[1]
    y = _dot(attn_ref[...], wo_ref[:aw, :]) + _dot(conv.astype(BF16), wo_ref[aw:, :])
    _route_tail(y, _stream_tile(ctx_ref, x_ref, ntb), mod_ref, g_ref, wr_ref, br_ref,
                x1_ref, h2_ref, route_ref, cnt_ref, carry_ref)


def _out1_kernel(hs_ref, gg_ref, wo_ref, x_ref, mod_ref, g_ref, wr_ref, br_ref,
                 x1_ref, h2_ref, route_ref, cnt_ref, carry_ref, *, ntb):
    i = pl.program_id(0)

    @pl.when(i == 0)
    def _():
        carry_ref[...] = jnp.zeros_like(carry_ref)

    @pl.when(i % ntb > 0)
    def _():
        rec = hs_ref[0].astype(F32) + hs_ref[1].astype(F32)
        y = _dot((rec * gg_ref[...].astype(F32)).astype(BF16), wo_ref[...])
        _route_tail(y, x_ref[...], mod_ref, g_ref, wr_ref, br_ref, x1_ref, h2_ref, route_ref, cnt_ref, carry_ref)


def _tail_specs(d, n_exp, ntb, latent_only):
    row = lambda i: (i, 0)
    const = lambda i: (0, 0)
    out_row = (lambda i: ((i // ntb) * (ntb - 1) + jnp.maximum(i % ntb - 1, 0), 0)) if latent_only else row
    in_specs = [
        pl.BlockSpec((TM, d), row),
        pl.BlockSpec((1, 6, d), lambda i: ((i // ntb) * 2 + jnp.minimum(i % ntb, 1), 0, 0)),
        pl.BlockSpec((1, d), const),
        pl.BlockSpec((d, n_exp), const),
        pl.BlockSpec((1, n_exp), const),
    ]
    out_specs = [
        pl.BlockSpec((TM, d), out_row),
        pl.BlockSpec((TM * (d // LANES), LANES), out_row),
        pl.BlockSpec((TM, LANES), out_row),
        pl.BlockSpec((1, n_exp), const),
    ]
    return in_specs, out_specs


def _tail_shapes(t, d, n_exp):
    return [jax.ShapeDtypeStruct((t, d), F32), jax.ShapeDtypeStruct((t * (d // LANES), LANES), F32),
            jax.ShapeDtypeStruct((t, LANES), F32), jax.ShapeDtypeStruct((1, n_exp), F32)]


def _out_proj0(attn, bg, p, conv_w, w_out, ctx, x, modv, g, w_r, b_r, ntb):
    d = x.shape[1]
    t = ctx.shape[0] + x.shape[0]
    n_exp = w_r.shape[1]
    bw = bg.shape[1]
    row = lambda i: (i, 0)
    const = lambda i: (0, 0)
    nblk = TM // SUBLANES
    tail_in, tail_out = _tail_specs(d, n_exp, ntb, False)
    return pl.pallas_call(
        functools.partial(_out0_kernel, ntb=ntb),
        grid=(t // TM,),
        in_specs=[
            pl.BlockSpec((TM, attn.shape[1]), row),
            pl.BlockSpec((TM, bw), row),
            pl.BlockSpec((TM, bw), row),
            pl.BlockSpec((SUBLANES, bw), lambda i: (jnp.maximum(i * nblk - 1, 0), 0)),
            pl.BlockSpec((SUBLANES, bw), lambda i: (jnp.minimum((i + 1) * nblk, t // SUBLANES - 1), 0)),
            pl.BlockSpec(conv_w.shape, const),
            pl.BlockSpec(w_out.shape, const),
        ] + _stream_specs(d, ntb) + tail_in[1:],
        out_specs=tail_out,
        out_shape=_tail_shapes(t, d, n_exp),
        scratch_shapes=[pltpu.VMEM((1, n_exp), F32)],
        compiler_params=_cparams(("arbitrary",)),
        name="l0_out_proj_router",
    )(attn, bg, p, p, p, conv_w, w_out, ctx, x, modv, g, w_r, b_r)


def _out_proj1(hs, gg, w_out, x, modv, g, w_r, b_r, ntb):
    t, d = x.shape
    n_exp = w_r.shape[1]
    row = lambda i: (i, 0)
    const = lambda i: (0, 0)
    tail_in, tail_out = _tail_specs(d, n_exp, ntb, True)
    t_lat = t // ntb * (ntb - 1)
    return pl.pallas_call(
        functools.partial(_out1_kernel, ntb=ntb),
        grid=(t // TM,),
        in_specs=[
            pl.BlockSpec((2, TM, hs.shape[2]), lambda i: (0, i, 0)),
            pl.BlockSpec((TM, gg.shape[1]), row),
            pl.BlockSpec(w_out.shape, const),
        ] + tail_in,
        out_specs=tail_out,
        out_shape=_tail_shapes(t_lat, d, n_exp),
        scratch_shapes=[pltpu.VMEM((1, n_exp), F32)],
        compiler_params=_cparams(("arbitrary",)),
        name="l1_out_proj_router",
    )(hs, gg, w_out, x, modv, g, w_r, b_r)


PAIRS_PER_TILE = TM * TOP_K


def _stage_pos(pos_ref, pbuf, psem):
    i = pl.program_id(0)
    slot = i % 2

    def chunk(ti, s):
        return pltpu.make_async_copy(pos_ref.at[pl.ds(ti * PAIRS_PER_TILE, PAIRS_PER_TILE)],
                                     pbuf.at[pl.ds(s * PAIRS_PER_TILE, PAIRS_PER_TILE)], psem.at[s])

    @pl.when(i == 0)
    def _():
        chunk(0, 0).start()

    chunk(i, slot).wait()

    @pl.when(i + 1 < pl.num_programs(0))
    def _():
        chunk(i + 1, 1 - slot).start()

    return slot * PAIRS_PER_TILE


def _dispatch_kernel(pad_lo_ref, pad_hi_ref, pos_ref, h2_ref, xs_ref, pbuf, zbuf, psem, dsem, zsem):
    i = pl.program_id(0)
    nch = h2_ref.shape[0] // TM
    n_tails = pad_lo_ref.shape[0] - 1

    def zero_slots(first, n):
        return pltpu.make_async_copy(zbuf.at[pl.ds(0, n * nch)],
                                     xs_ref.at[pl.ds(pl.multiple_of(first * nch, nch), n * nch)], zsem)

    def over_padding(act):
        def tail(e, carry):
            lo = pad_lo_ref[e]
            n = pad_hi_ref[e] - lo
            for bit in range(TME.bit_length() - 1):
                size = 1 << bit

                @pl.when((n & size) != 0)
                def _():
                    act(zero_slots(lo + (n & -(2 * size)), size))
            return carry

        lax.fori_loop(0, n_tails, tail, 0)

        def tile(j, c):
            act(zero_slots(j * TME, TME))
            return c

        lax.fori_loop(pad_lo_ref[n_tails] // TME, pad_hi_ref[n_tails] // TME, tile, 0)

    @pl.when(i == 0)
    def _():
        zbuf[...] = jnp.zeros(zbuf.shape, zbuf.dtype)
        over_padding(lambda cp: cp.start())
        over_padding(lambda cp: cp.wait())

    off = _stage_pos(pos_ref, pbuf, psem)
    for r in range(TM):
        for kk in range(TOP_K):
            dst = pl.multiple_of(pbuf[off + r * TOP_K + kk] * nch, nch)
            pltpu.make_async_copy(h2_ref.at[pl.ds(r * nch, nch)], xs_ref.at[pl.ds(dst, nch)], dsem).start(
                priority=kk % 2)
    for _ in range(TOP_K):
        pltpu.make_async_copy(h2_ref, xs_ref.at[pl.ds(0, TM * nch)], dsem).wait()


def _dispatch(pos, pad_lo, pad_hi, h2, n_slots, d):
    nch = d // LANES
    n_tok = h2.shape[0] // nch
    grid_spec = pltpu.PrefetchScalarGridSpec(
        num_scalar_prefetch=2,
        grid=(n_tok // TM,),
        in_specs=[pl.BlockSpec(memory_space=pl.ANY),
                  pl.BlockSpec((TM * nch, LANES), lambda i, lo, hi: (i, 0))],
        out_specs=pl.BlockSpec(memory_space=pl.ANY),
        scratch_shapes=[pltpu.SMEM((2 * PAIRS_PER_TILE,), I32), pltpu.VMEM((TME * nch, LANES), F32),
                        pltpu.SemaphoreType.DMA((2,)), pltpu.SemaphoreType.DMA, pltpu.SemaphoreType.DMA],
    )
    return pl.pallas_call(
        _dispatch_kernel,
        grid_spec=grid_spec,
        out_shape=jax.ShapeDtypeStruct((n_slots * nch, LANES), F32),
        compiler_params=_cparams(("arbitrary",)),
        name="expert_dispatch",
    )(pad_lo, pad_hi, pos, h2)


def _moe_kernel(te_ref, nx_ref, par_ref, tv_ref, meta_ref, x_ref, w1_hbm, b1_ref, w2_hbm, b2_ref, y_ref,
                w1f, w2f, w1b, w2b, wsem, *, layer):
    i = pl.program_id(0)
    n_tiles = meta_ref[0]
    d = w1b.shape[0]
    ff = w2b.shape[0]
    nch = d // LANES

    def weight_copies(e, s):
        return (pltpu.make_async_copy(w1_hbm.at[layer, e], w1f.at[s], wsem.at[0, s]),
                pltpu.make_async_copy(w2_hbm.at[layer, e], w2f.at[s], wsem.at[1, s]))

    @pl.when(i == 0)
    def _():
        for cp in weight_copies(te_ref[0], par_ref[0]):
            cp.start()

    @pl.when(i < n_tiles)
    def _():
        new_expert = jnp.logical_or(i == 0, te_ref[i] != te_ref[jnp.maximum(i - 1, 0)])

        @pl.when(new_expert)
        def _():
            s = par_ref[i]
            for cp in weight_copies(te_ref[i], s):
                cp.wait()
            w1b[...] = w1f[s].astype(BF16)
            w2b[...] = w2f[s].astype(BF16)

            @pl.when(nx_ref[i] >= 0)
            def _():
                for cp in weight_copies(nx_ref[i], 1 - s):
                    cp.start()

        half = TME // 2

        def ffn_rows(r0):
            x = jnp.concatenate([x_ref[pl.ds(r0 * nch + c, half, stride=nch), :] for c in range(nch)], axis=1)
            h = _dot(x.astype(BF16), w1b[...]) + b1_ref[0]
            glu = jnp.minimum(h[:, :ff], SWIGLU_LIMIT)
            lin = jnp.clip(h[:, ff:], -SWIGLU_LIMIT, SWIGLU_LIMIT)
            act = glu * _sigmoid(SWIGLU_ALPHA * glu) * (lin + 1.0)
            y = _dot(act.astype(BF16), w2b[...]) + b2_ref[0]
            for c in range(nch):
                y_ref[pl.ds(r0 * nch + c, half, stride=nch), :] = y[:, c * LANES:(c + 1) * LANES]

        ffn_rows(0)

        @pl.when(tv_ref[i] > half)
        def _():
            ffn_rows(half)

        @pl.when(tv_ref[i] <= half)
        def _():
            y_ref[pl.ds(half * nch, half * nch), :] = jnp.zeros((half * nch, LANES), y_ref.dtype)

    @pl.when(i >= n_tiles)
    def _():
        y_ref[...] = jnp.zeros(y_ref.shape, y_ref.dtype)


def _routed_experts(tile_expert, next_expert, parity, tile_valid, meta, xs, w1, b1, w2, b2, layer):
    _, n_exp, d, ff2 = w1.shape
    ff = w2.shape[2]
    nch = d // LANES
    n_grid = tile_expert.shape[0]
    used = lambda i, mt: jnp.minimum(i, mt[0] - 1)
    bmap = lambda i, te, nx, pr, tv, mt: (layer * n_exp + te[used(i, mt)], 0, 0)
    grid_spec = pltpu.PrefetchScalarGridSpec(
        num_scalar_prefetch=5,
        grid=(n_grid,),
        in_specs=[
            pl.BlockSpec((TME * nch, LANES), lambda i, te, nx, pr, tv, mt: (used(i, mt), 0)),
            pl.BlockSpec(memory_space=pl.ANY),
            pl.BlockSpec((1, 1, ff2), bmap),
            pl.BlockSpec(memory_space=pl.ANY),
            pl.BlockSpec((1, 1, d), bmap),
        ],
        out_specs=pl.BlockSpec((TME * nch, LANES), lambda i, te, nx, pr, tv, mt: (i, 0)),
        scratch_shapes=[pltpu.VMEM((2, d, ff2), F32), pltpu.VMEM((2, ff, d), F32),
                        pltpu.VMEM((d, ff2), BF16), pltpu.VMEM((ff, d), BF16), pltpu.SemaphoreType.DMA((2, 2))],
    )
    return pl.pallas_call(
        functools.partial(_moe_kernel, layer=layer),
        grid_spec=grid_spec,
        out_shape=jax.ShapeDtypeStruct(xs.shape, F32),
        compiler_params=_cparams(("arbitrary",)),
        name="routed_experts",
    )(tile_expert, next_expert, parity, tile_valid, meta, xs, w1, b1.reshape(-1, 1, ff2), w2, b2.reshape(-1, 1, d))


def _gather_choices(pos_ref, ys_ref, ybuf, pbuf, psem, csem):
    i = pl.program_id(0)
    n = pl.num_programs(0)
    slot = i % 2
    nch = ybuf.shape[2] // TM

    def chunk(ti, s):
        return pltpu.make_async_copy(pos_ref.at[pl.ds(ti * PAIRS_PER_TILE, PAIRS_PER_TILE)],
                                     pbuf.at[pl.ds(s * PAIRS_PER_TILE, PAIRS_PER_TILE)], psem.at[s])

    def row_copy(s, r, kk):
        src = pl.multiple_of(pbuf[s * PAIRS_PER_TILE + r * TOP_K + kk] * nch, nch)
        return pltpu.make_async_copy(ys_ref.at[pl.ds(src, nch)], ybuf.at[s, kk, pl.ds(r * nch, nch)], csem.at[s])

    @pl.when(i == 0)
    def _():
        chunk(0, 0).start()
        chunk(0, 0).wait()

        def first_rows(r, c):
            for kk in range(TOP_K):
                row_copy(0, r, kk).start(priority=kk % 2)
            return c

        lax.fori_loop(0, TM, first_rows, 0)

        @pl.when(n > 1)
        def _():
            chunk(1, 1).start()

    @pl.when(i + 1 < n)
    def _():
        chunk(i + 1, 1 - slot).wait()
        for r in range(TM):
            for kk in range(TOP_K):
                row_copy(1 - slot, r, kk).start(priority=kk % 2)

        @pl.when(i + 2 < n)
        def _():
            chunk(i + 2, slot).start()

    for kk in range(TOP_K):
        pltpu.make_async_copy(ys_ref.at[pl.ds(0, TM * nch)], ybuf.at[slot, kk], csem.at[slot]).wait()
    return slot


def _gather_scratch(d):
    nch = d // LANES
    return [pltpu.VMEM((2, TOP_K, TM * nch, LANES), F32), pltpu.SMEM((2 * PAIRS_PER_TILE,), I32),
            pltpu.SemaphoreType.DMA((2,)), pltpu.SemaphoreType.DMA((2,))]


def _plan_routes(route, counts, n_grid):
    n_exp = counts.shape[-1]
    cnt = counts.reshape(n_exp).astype(I32)
    tiles_per = (cnt + TME - 1) // TME
    tile_end = jnp.cumsum(tiles_per)
    offset = (tile_end - tiles_per) * TME
    n_tiles = tile_end[-1]
    tile_ids = jnp.minimum(jnp.arange(n_grid, dtype=I32), n_tiles - 1)
    tile_expert = jnp.sum((tile_ids[:, None] >= tile_end[None, :]).astype(I32), axis=1)
    expert_ids = jnp.arange(n_exp, dtype=I32)
    of_tile = tile_expert[:, None] == expert_ids
    later = jnp.logical_and(expert_ids[None, :] > expert_ids[:, None], (tiles_per > 0)[None, :])
    next_of = jnp.min(jnp.where(later, expert_ids[None, :], n_exp), axis=1)
    next_of = jnp.where(next_of == n_exp, -1, next_of)
    next_expert = jnp.sum(jnp.where(of_tile, next_of, 0), axis=1)
    parity = jnp.sum(jnp.where(of_tile, jnp.cumsum((tiles_per > 0).astype(I32)), 0), axis=1) % 2
    ahead = (tile_ids - jnp.sum(jnp.where(of_tile, tile_end - tiles_per, 0), axis=1)) * TME
    tile_valid = jnp.clip(jnp.sum(jnp.where(of_tile, cnt, 0), axis=1) - ahead, 1, TME)
    eidx = route[:, :TOP_K].astype(I32)
    rank = route[:, TOP_K:2 * TOP_K].astype(I32)
    pos = jnp.sum(jnp.where(eidx[..., None] == expert_ids, offset, 0), axis=-1) + rank
    pad_lo = jnp.concatenate([offset + cnt, (n_tiles * TME).reshape(1)])
    pad_hi = jnp.concatenate([tile_end * TME, jnp.full((1,), n_grid * TME, I32)])
    return (tile_expert, next_expert, parity, tile_valid, n_tiles.reshape(1)), pos.reshape(-1), pad_lo, pad_hi


def _moe_layer(route, counts, h2, w1, b1, w2, b2, layer):
    n_exp, d = w1.shape[1], w1.shape[2]
    n_tok = route.shape[0]
    n_grid = n_tok * TOP_K // TME + n_exp
    tiles, pos, pad_lo, pad_hi = _plan_routes(route, counts, n_grid)
    xs = _dispatch(pos, pad_lo, pad_hi, h2, n_grid * TME, d)
    return pos, _routed_experts(*tiles, xs, w1, b1, w2, b2, layer)


def _combine(ybuf, slot, route_ref, d):
    nch = d // LANES
    gates = route_ref[...]
    gk = [jnp.broadcast_to(gates[:, 2 * TOP_K + kk:2 * TOP_K + kk + 1], (TM, LANES)) for kk in range(TOP_K)]
    chunks = []
    for c in range(nch):
        acc = gk[0] * ybuf[slot, 0, pl.ds(c, TM, stride=nch), :]
        for kk in range(1, TOP_K):
            acc = acc + gk[kk] * ybuf[slot, kk, pl.ds(c, TM, stride=nch), :]
        chunks.append(acc)
    return jnp.concatenate(chunks, axis=1)


def _in1_kernel(x_ref, pos_ref, ys_ref, route_ref, mod0_ref, mod1_ref, g_ref, w_ref, x2_ref, gg_ref, u_ref,
                ybuf, pbuf, psem, csem):
    slot = _gather_choices(pos_ref, ys_ref, ybuf, pbuf, psem, csem)
    x2 = x_ref[...] + mod0_ref[0][5:6] * _combine(ybuf, slot, route_ref, x_ref.shape[1])
    x2_ref[...] = x2
    mod1 = mod1_ref[0]
    h = _norm_mod(x2, g_ref[...], mod1[0:1], mod1[1:2])
    y = _dot(h.astype(BF16), w_ref[...])
    half = y.shape[1] // 2
    gg_ref[...] = jax.nn.gelu(y[:, :half]).astype(BF16)
    u_ref[...] = y[:, half:]


def _in_proj1(x, pos, ys, route, modv0, modv1, g, w, ntb):
    t, d = x.shape
    n = w.shape[1]
    row = lambda i: (i, 0)
    mod_map = lambda i: ((i // ntb) * 2 + jnp.minimum(i % ntb, 1), 0, 0)
    return pl.pallas_call(
        _in1_kernel,
        grid=(t // TM,),
        in_specs=[
            pl.BlockSpec((TM, d), row),
            pl.BlockSpec(memory_space=pl.ANY),
            pl.BlockSpec(memory_space=pl.ANY),
            pl.BlockSpec((TM, LANES), row),
            pl.BlockSpec((1, 6, d), mod_map),
            pl.BlockSpec((1, 6, d), mod_map),
            pl.BlockSpec((1, d), lambda i: (0, 0)),
            pl.BlockSpec((d, n), lambda i: (0, 0)),
        ],
        out_specs=[pl.BlockSpec((TM, d), row), pl.BlockSpec((TM, n // 2), row), pl.BlockSpec((TM, n // 2), row)],
        out_shape=[jax.ShapeDtypeStruct((t, d), F32), jax.ShapeDtypeStruct((t, n // 2), BF16),
                   jax.ShapeDtypeStruct((t, n // 2), F32)],
        scratch_shapes=_gather_scratch(d),
        compiler_params=_cparams(("arbitrary",)),
        name="l1_combine_in_proj",
    )(x, pos, ys, route, modv0, modv1, g, w)


def _rglru_kernel(u_ref, halo_ref, cw_ref, cb_ref, gaw_ref, gab_ref, gxw_ref, gxb_ref, lam_ref, o_ref,
                  ext, a_s, b_s, h_s, state, *, ntb):
    d = pl.program_id(0)
    s = pl.program_id(2)
    nb, tc, cw_ = u_ref.shape
    u = u_ref[...]
    cw = cw_ref[0]

    def finish(xc, reverse):
        xc2 = xc.reshape(nb * tc, cw_) + cb_ref[0]
        xb = xc2.astype(BF16)
        t_r = jnp.tanh(_dot(xb, gaw_ref[0, 0]) + gab_ref[0])
        t_i = jnp.tanh(_dot(xb, gxw_ref[0, 0]) + gxb_ref[0])
        nl = -lam_ref[0]
        softplus = jnp.maximum(nl, 0.0) + jnp.log1p(jnp.exp(-jnp.abs(nl)))
        half_rate = (-0.5 * LRU_C) * softplus
        log_a = half_rate * t_r + half_rate
        a = jnp.exp(log_a)
        half_x = 0.5 * xc2
        bb = jnp.sqrt(1.0 - a * a) * (half_x * t_i + half_x)
        n_lane = cw_ // LANES
        pitch = tc + SCAN_PAD
        for c in range(n_lane):
            for bi in range(nb):
                a_s[c, bi * pitch:bi * pitch + tc, :] = a[bi * tc:(bi + 1) * tc, c * LANES:(c + 1) * LANES]
                b_s[c, bi * pitch:bi * pitch + tc, :] = bb[bi * tc:(bi + 1) * tc, c * LANES:(c + 1) * LANES]

        @pl.when(s == 0)
        def _():
            state[...] = jnp.zeros_like(state)

        def steps(tb, hs):
            hs = list(hs)
            for uu in range(SCAN_UNROLL):
                tt = tb * SCAN_UNROLL + uu
                t = (tc - 1 - tt) if reverse else tt
                rows = pl.ds(t, nb, stride=pitch)
                for c in range(n_lane):
                    hs[c] = a_s[c, rows, :] * hs[c] + b_s[c, rows, :]
                    h_s[c, rows, :] = hs[c]
            return tuple(hs)

        hs = lax.fori_loop(0, tc // SCAN_UNROLL, steps, tuple(state[c] for c in range(n_lane)))
        for c in range(n_lane):
            state[c] = hs[c]
            for bi in range(nb):
                o_ref[0, bi, :, c * LANES:(c + 1) * LANES] = h_s[c, bi * pitch:bi * pitch + tc, :].astype(o_ref.dtype)

    @pl.when(d == 0)
    def _():
        chunk = s
        keep = chunk > 1
        ext[:, 0:SUBLANES, :] = jnp.where(keep, halo_ref[...], 0.0)
        ext[:, SUBLANES:, :] = u
        xc = cw[3:4] * u
        for j in range(1, C_CONV):
            xc = xc + cw[3 - j:4 - j] * ext[:, SUBLANES - j:SUBLANES - j + tc, :]
        finish(xc, False)

    @pl.when(d == 1)
    def _():
        chunk = jnp.where(s == 0, 0, ntb - s)
        keep = jnp.logical_and(chunk > 0, chunk < ntb - 1)
        ext[:, 0:tc, :] = u
        ext[:, tc:, :] = jnp.where(keep, halo_ref[...], 0.0)
        xc = cw[0:1] * u
        for j in range(1, C_CONV):
            xc = xc + cw[j:j + 1] * ext[:, j:j + tc, :]
        finish(xc, True)


def _rglru(u, conv_w, conv_b, ga_w, ga_b, gx_w, gx_b, lam, b, l):
    width = u.shape[1]
    cb = width // C_BLOCKS
    ntb = l // TM
    u3 = u.reshape(b, l, width)
    nblk = TM // SUBLANES

    def chunk_of(d, s):
        return jnp.where(d == 0, s, jnp.where(s == 0, 0, ntb - s))

    def halo_of(d, s):
        c = chunk_of(d, s)
        return jnp.where(d == 0, jnp.maximum(c * nblk - 1, 0), jnp.minimum((c + 1) * nblk, l // SUBLANES - 1))

    vec = pl.BlockSpec((1, 1, cb), lambda d, g, s: (d, 0, g))
    mat = pl.BlockSpec((1, 1, cb, cb), lambda d, g, s: (d, g, 0, 0))
    return pl.pallas_call(
        functools.partial(_rglru_kernel, ntb=ntb),
        grid=(2, C_BLOCKS, ntb),
        in_specs=[
            pl.BlockSpec((b, TM, cb), lambda d, g, s: (0, chunk_of(d, s), g)),
            pl.BlockSpec((b, SUBLANES, cb), lambda d, g, s: (0, halo_of(d, s), g)),
            pl.BlockSpec((1, C_CONV, cb), lambda d, g, s: (d, 0, g)),
            vec, mat, vec, mat, vec, vec,
        ],
        out_specs=pl.BlockSpec((1, b, TM, cb), lambda d, g, s: (d, 0, chunk_of(d, s), g)),
        out_shape=jax.ShapeDtypeStruct((2, b, l, width), BF16),
        scratch_shapes=[
            pltpu.VMEM((b, TM + SUBLANES, cb), F32),
            pltpu.VMEM((cb // LANES, b * (TM + SCAN_PAD), LANES), F32),
            pltpu.VMEM((cb // LANES, b * (TM + SCAN_PAD), LANES), F32),
            pltpu.VMEM((cb // LANES, b * (TM + SCAN_PAD), LANES), F32),
            pltpu.VMEM((cb // LANES, b, LANES), F32),
        ],
        compiler_params=_cparams(("arbitrary", "arbitrary", "arbitrary")),
        name="l1_rglru",
    )(u3, u3, conv_w, conv_b.reshape(2, 1, width), (0.5 * ga_w).astype(BF16), (0.5 * ga_b).reshape(2, 1, width),
      (0.5 * gx_w).astype(BF16), (0.5 * gx_b).reshape(2, 1, width), lam.reshape(2, 1, width))


def _final_kernel(x_ref, pos_ref, ys_ref, route_ref, mod_ref, g_ref, o_ref, ybuf, pbuf, psem, csem):
    slot = _gather_choices(pos_ref, ys_ref, ybuf, pbuf, psem, csem)
    x = x_ref[...] + mod_ref[0][5:6] * _combine(ybuf, slot, route_ref, x_ref.shape[1])
    ms = jnp.mean(x * x, axis=-1, keepdims=True)
    o_ref[...] = x * lax.rsqrt(ms + NORM_EPS) * g_ref[...]


def _final(x, pos, ys, route, modv, g, b, s_len):
    t, d = x.shape
    per_b = s_len // TM
    row = lambda i: (i, 0)
    return pl.pallas_call(
        _final_kernel,
        grid=(t // TM,),
        in_specs=[
            pl.BlockSpec((TM, d), row),
            pl.BlockSpec(memory_space=pl.ANY),
            pl.BlockSpec(memory_space=pl.ANY),
            pl.BlockSpec((TM, LANES), row),
            pl.BlockSpec((1, 6, d), lambda i: ((i // per_b) * 2 + 1, 0, 0)),
            pl.BlockSpec((1, d), lambda i: (0, 0)),
        ],
        out_specs=pl.BlockSpec((TM, d), row),
        out_shape=jax.ShapeDtypeStruct((t, d), F32),
        scratch_shapes=_gather_scratch(d),
        compiler_params=_cparams(("arbitrary",)),
        name="final_combine_norm",
    )(x, pos, ys, route, modv, g)


def _rope_tables(s_len, n_ctx):
    n_rows = s_len // GRID_W
    rows, cols = jnp.meshgrid(jnp.arange(n_rows), jnp.arange(GRID_W), indexing="ij")
    pos = jnp.stack([rows.reshape(-1), cols.reshape(-1)], axis=-1).astype(F32)
    n_freq = HEAD_DIM // 4
    inv = ROPE_THETA ** (-jnp.arange(n_freq, dtype=F32) / n_freq)
    ang = pos[:, :, None] * inv
    cos, sin = jnp.cos(ang), jnp.sin(ang)
    cos64 = jnp.stack([cos, cos], axis=2).reshape(s_len, HEAD_DIM)
    sin64 = jnp.stack([-sin, sin], axis=2).reshape(s_len, HEAD_DIM)
    cos_l = jnp.tile(cos64, (1, LANES // HEAD_DIM))
    sin_l = jnp.tile(sin64, (1, LANES // HEAD_DIM))
    cos_t = jnp.concatenate([jnp.ones((n_ctx, LANES), F32), cos_l], axis=0)
    sin_t = jnp.concatenate([jnp.zeros((n_ctx, LANES), F32), sin_l], axis=0)
    return cos_t, sin_t


def kernel(x, c, ctx, c_ctx, w_mod, b_mod, norm_mix, norm_ffn, ev_w_in, ev_w_out, ev_lambda_q1, ev_lambda_k1, ev_lambda_q2, ev_lambda_k2, ev_subln, ev_conv_w, od_w_in, od_w_out, od_conv_w, od_conv_b, od_gate_a_w, od_gate_a_b, od_gate_x_w, od_gate_x_b, od_lru_lambda, moe_w_router, moe_b_router, moe_w1, moe_b1, moe_w2, moe_b2, final_norm):
    b, s_len, d = x.shape
    n_ctx = ctx.shape[1]
    l = n_ctx + s_len
    t = b * l
    ntb = l // TM
    assert n_ctx == TM and s_len % TM == 0 and w_mod.shape[0] == 2

    ctx2 = ctx.reshape(b * n_ctx, d)
    x2d = x.reshape(b * s_len, d)

    n_rows = -(-(b + 1) // SUBLANES) * SUBLANES
    cs = jnp.concatenate([c, c_ctx[None, :], jnp.zeros((n_rows - b - 1, d), F32)], axis=0)
    mod = _modulation(cs, w_mod, b_mod)

    def mod_table(i):
        lat = mod[i, :b]
        cx = jnp.broadcast_to(mod[i, b][None, :], lat.shape)
        return jnp.stack([cx, lat], axis=1).reshape(b * 2, 6, d)

    modv0, modv1 = mod_table(0), mod_table(1)

    cos_t, sin_t = _rope_tables(s_len, n_ctx)
    q, k, v, bg, p = _in_proj0(ctx2, x2d, modv0, norm_mix[0:1], ev_w_in[0].astype(BF16), cos_t, sin_t, ntb)
    lam_init = 0.8 - 0.6 * math.exp(-0.3 * 0)
    attn = _diff_attention(q, k, v, ev_lambda_q1[0:1], ev_lambda_k1[0:1], ev_lambda_q2[0:1], ev_lambda_k2[0:1],
                           ev_subln[0:1], lam_init, b, l, n_ctx)
    x1, h2, route0, counts0 = _out_proj0(attn, bg, p, ev_conv_w[0], ev_w_out[0].astype(BF16), ctx2, x2d, modv0,
                                         norm_ffn[0:1], moe_w_router[0], moe_b_router[0:1], ntb)
    pos0, ys0 = _moe_layer(route0, counts0, h2, moe_w1, moe_b1, moe_w2, moe_b2, 0)

    x2, gg, u = _in_proj1(x1, pos0, ys0, route0, modv0, modv1, norm_mix[1:2], od_w_in[0].astype(BF16), ntb)
    hs = _rglru(u, od_conv_w[0], od_conv_b[0], od_gate_a_w[0], od_gate_a_b[0], od_gate_x_w[0], od_gate_x_b[0],
                od_lru_lambda[0], b, l)
    x3, h3, route1, counts1 = _out_proj1(hs.reshape(2, t, hs.shape[-1]), gg, od_w_out[0].astype(BF16), x2, modv1,
                                         norm_ffn[1:2], moe_w_router[1], moe_b_router[1:2], ntb)
    pos1, ys1 = _moe_layer(route1, counts1, h3, moe_w1, moe_b1, moe_w2, moe_b2, 1)

    out = _final(x3, pos1, ys1, route1, modv1, final_norm[None, :], b, s_len)
    return out.reshape(b, s_len, d)
```

```python
import functools
import math

import jax
import jax.numpy as jnp
from jax import lax
from jax.experimental import pallas as pl
from jax.experimental.pallas import tpu as pltpu

F32 = jnp.float32
BF16 = jnp.bfloat16
I32 = jnp.int32

NORM_EPS = 1e-6
ROPE_THETA = 10000.0
GRID_W = 64
N_HEADS = 4
HEAD_DIM = 64
A_WIDTH = 2 * N_HEADS * HEAD_DIM
B_CONV = 3
C_CONV = 4
C_BLOCKS = 4
LRU_C = 8.0
TOP_K = 4
SWIGLU_ALPHA = 1.702
SWIGLU_LIMIT = 7.0

LANES = 128
SUBLANES = 8
TM = 256
TME = 512
VMEM_LIMIT = 56 * 1024 * 1024
SCAN_PAD = 8
SCAN_UNROLL = 8
LOG2_E = 1.4426950408889634
HEADS_PER_STEP = 2


def _cparams(sem, vmem=VMEM_LIMIT):
    return pltpu.CompilerParams(dimension_semantics=sem, vmem_limit_bytes=vmem)


def _norm_mod(x, g, shift, scale):
    ms = jnp.mean(x * x, axis=-1, keepdims=True)
    return (x * lax.rsqrt(ms + NORM_EPS) * g) * (1.0 + scale) + shift


def _dot(a, b):
    return jnp.dot(a, b, preferred_element_type=F32)


def _sigmoid(x):
    return 0.5 * jnp.tanh(0.5 * x) + 0.5


def _mod_kernel(cs_ref, w_ref, b_ref, o_ref):
    s = cs_ref[...]
    s = s * jax.nn.sigmoid(s)
    o_ref[0] = _dot(s.astype(BF16), w_ref[0].astype(BF16)) + b_ref[0]


def _modulation(cs, w_mod, b_mod):
    depth, d, n = w_mod.shape
    rows = cs.shape[0]
    tn = 1536
    return pl.pallas_call(
        _mod_kernel,
        grid=(depth, n // tn),
        in_specs=[
            pl.BlockSpec((rows, d), lambda i, j: (0, 0)),
            pl.BlockSpec((1, d, tn), lambda i, j: (i, 0, j)),
            pl.BlockSpec((1, 1, tn), lambda i, j: (i, 0, j)),
        ],
        out_specs=pl.BlockSpec((1, rows, tn), lambda i, j: (i, 0, j)),
        out_shape=jax.ShapeDtypeStruct((depth, rows, n), F32),
        compiler_params=_cparams(("parallel", "parallel")),
        name="adaln_modulation",
    )(cs, w_mod, b_mod.reshape(depth, 1, n))


def _stream_tile(ctx_ref, x_ref, ntb):
    return jnp.where(pl.program_id(0) % ntb == 0, ctx_ref[...], x_ref[...])


def _stream_specs(d, ntb):
    return [pl.BlockSpec((TM, d), lambda i: (i // ntb, 0)),
            pl.BlockSpec((TM, d), lambda i: ((i // ntb) * (ntb - 1) + jnp.maximum(i % ntb - 1, 0), 0))]


def _in0_kernel(ctx_ref, x_ref, mod_ref, g_ref, w_ref, cos_ref, sin_ref,
                q_ref, k_ref, v_ref, bg_ref, p_ref, *, ntb):
    mod = mod_ref[0]
    h = _norm_mod(_stream_tile(ctx_ref, x_ref, ntb), g_ref[...], mod[0:1], mod[1:2])
    y = _dot(h.astype(BF16), w_ref[...])
    cosv = cos_ref[...]
    sinv = sin_ref[...]
    lane = lax.broadcasted_iota(I32, (TM, LANES), 1)
    first_half = (lane & 16) == 0

    def rope(z):
        outs = []
        for g in range(A_WIDTH // LANES):
            zg = z[:, g * LANES:(g + 1) * LANES]
            partner = jnp.where(first_half, pltpu.roll(zg, LANES - 16, 1), pltpu.roll(zg, 16, 1))
            outs.append(zg * cosv + partner * sinv)
        return jnp.concatenate(outs, axis=1)

    aw = A_WIDTH
    q_ref[...] = (rope(y[:, :aw]) * (HEAD_DIM ** -0.5 * LOG2_E)).astype(BF16)
    k_ref[...] = rope(y[:, aw:2 * aw]).astype(BF16)
    v_ref[...] = y[:, 2 * aw:3 * aw].astype(BF16)
    bw = (y.shape[1] - 3 * aw) // 3
    bg_ref[...] = y[:, 3 * aw:3 * aw + bw].astype(BF16)
    p_ref[...] = (y[:, 3 * aw + bw:3 * aw + 2 * bw] * y[:, 3 * aw + 2 * bw:]).astype(BF16)


def _in_proj0(ctx, x, modv, g, w, cos_t, sin_t, ntb):
    d = x.shape[1]
    t = ctx.shape[0] + x.shape[0]
    n = w.shape[1]
    bw = (n - 3 * A_WIDTH) // 3
    row = lambda i: (i, 0)
    return pl.pallas_call(
        functools.partial(_in0_kernel, ntb=ntb),
        grid=(t // TM,),
        in_specs=_stream_specs(d, ntb) + [
            pl.BlockSpec((1, 6, d), lambda i: ((i // ntb) * 2 + jnp.minimum(i % ntb, 1), 0, 0)),
            pl.BlockSpec((1, d), lambda i: (0, 0)),
            pl.BlockSpec((d, n), lambda i: (0, 0)),
            pl.BlockSpec((TM, LANES), lambda i: (i % ntb, 0)),
            pl.BlockSpec((TM, LANES), lambda i: (i % ntb, 0)),
        ],
        out_specs=[pl.BlockSpec((TM, A_WIDTH), row)] * 3 + [pl.BlockSpec((TM, bw), row)] * 2,
        out_shape=[jax.ShapeDtypeStruct((t, A_WIDTH), BF16)] * 3 + [jax.ShapeDtypeStruct((t, bw), BF16)] * 2,
        compiler_params=_cparams(("parallel",)),
        name="l0_in_proj_rope",
    )(ctx, x, modv, g, w, cos_t, sin_t)


def _attn_kernel(lq1_ref, lk1_ref, lq2_ref, lk2_ref, g_ref, q_ref, k_ref, v_ref, o_ref, *, lam_init, n_ctx):
    qi = pl.program_id(2)
    lam = (jnp.exp(jnp.sum(lq1_ref[...] * lk1_ref[...], axis=-1, keepdims=True))
           - jnp.exp(jnp.sum(lq2_ref[...] * lk2_ref[...], axis=-1, keepdims=True)) + lam_init)
    hw = 2 * HEAD_DIM
    lane = lax.broadcasted_iota(I32, (q_ref.shape[0], hw), 1)
    contract_last = (((1,), (1,)), ((), ()))

    def attend(nk):
        for hh in range(q_ref.shape[1] // hw):
            cols = slice(hh * hw, (hh + 1) * hw)
            q = q_ref[:, cols]
            zero = jnp.zeros_like(q)
            q1 = jnp.where(lane < HEAD_DIM, q, zero)
            q2 = jnp.where(lane < HEAD_DIM, zero, q)
            k = k_ref[0, :nk, cols]
            v = v_ref[0, :nk, cols]
            s1 = lax.dot_general(q1, k, contract_last, preferred_element_type=F32)
            s2 = lax.dot_general(q2, k, contract_last, preferred_element_type=F32)
            p1 = jnp.exp2(s1 - jnp.max(s1, axis=-1, keepdims=True))
            p2 = jnp.exp2(s2 - jnp.max(s2, axis=-1, keepdims=True))
            r1 = 1.0 / jnp.sum(p1, axis=-1, keepdims=True)
            r2 = lam / jnp.sum(p2, axis=-1, keepdims=True)
            o = _dot(p1.astype(BF16), v) * r1 - _dot(p2.astype(BF16), v) * r2
            ms = jnp.mean(o * o, axis=-1, keepdims=True)
            o = o * lax.rsqrt(ms + NORM_EPS) * g_ref[...] * (1.0 - lam_init)
            o_ref[:, cols] = o.astype(BF16)

    @pl.when(qi == 0)
    def _():
        attend(n_ctx)

    @pl.when(qi > 0)
    def _():
        attend(k_ref.shape[1])


def _diff_attention(q, k, v, lq1, lk1, lq2, lk2, subln, lam_init, b, l, n_ctx):
    t = q.shape[0]
    ntb = l // TM
    hw = 2 * HEAD_DIM
    k3 = k.reshape(b, l, A_WIDTH)
    v3 = v.reshape(b, l, A_WIDTH)
    vec = lambda n: pl.BlockSpec((1, n), lambda bi, h, qi: (0, 0))
    gw = HEADS_PER_STEP * hw
    qspec = pl.BlockSpec((TM, gw), lambda bi, h, qi: (bi * ntb + qi, h))
    kspec = pl.BlockSpec((1, l, gw), lambda bi, h, qi: (bi, 0, h))
    return pl.pallas_call(
        functools.partial(_attn_kernel, lam_init=lam_init, n_ctx=n_ctx),
        grid=(b, N_HEADS // HEADS_PER_STEP, ntb),
        in_specs=[vec(HEAD_DIM)] * 4 + [vec(hw), qspec, kspec, kspec],
        out_specs=qspec,
        out_shape=jax.ShapeDtypeStruct((t, A_WIDTH), BF16),
        compiler_params=_cparams(("parallel", "parallel", "parallel")),
        name="l0_diff_attention",
    )(lq1, lk1, lq2, lk2, subln, q, k3, v3)


def _route_tail(y, x_in, mod_ref, g_ref, wr_ref, br_ref,
                x1_ref, h2_ref, route_ref, cnt_ref, carry_ref):
    mod = mod_ref[0]
    x1 = x_in + mod[2:3] * y
    x1_ref[...] = x1
    h2 = _norm_mod(x1, g_ref[...], mod[3:4], mod[4:5])
    nch = h2.shape[1] // LANES
    for s in range(nch):
        h2_ref[pl.ds(s, TM, stride=nch), :] = h2[:, s * LANES:(s + 1) * LANES]

    logits = _dot(h2.astype(BF16), wr_ref[...].astype(BF16)) + br_ref[...]

    n_exp = logits.shape[1]
    lane = lax.broadcasted_iota(I32, logits.shape, 1).astype(F32)
    work = logits
    sels, vals, idxs = [], [], []
    for _ in range(TOP_K):
        m = jnp.max(work, axis=-1, keepdims=True)
        idx = jnp.min(jnp.where(work == m, lane, float(n_exp)), axis=-1, keepdims=True)
        sel = lane == idx
        sels.append(sel)
        vals.append(m)
        idxs.append(idx)
        work = jnp.where(sel, -jnp.inf, work)
    exps = [jnp.exp(vv - vals[0]) for vv in vals]
    inv_den = 1.0 / (exps[0] + exps[1] + exps[2] + exps[3])

    chosen = jnp.zeros(logits.shape, F32)
    for sel in sels:
        chosen = chosen + jnp.where(sel, 1.0, 0.0)
    r_i = lax.broadcasted_iota(I32, (TM, TM), 0)
    c_i = lax.broadcasted_iota(I32, (TM, TM), 1)
    earlier = jnp.where(c_i < r_i, 1.0, 0.0).astype(BF16)
    rank = _dot(earlier, chosen.astype(BF16)) + carry_ref[...]
    carry_ref[...] = carry_ref[...] + jnp.sum(chosen, axis=0, keepdims=True)
    cnt_ref[...] = carry_ref[...]

    out_lane = lax.broadcasted_iota(I32, (TM, LANES), 1)
    packed = jnp.zeros((TM, LANES), F32)
    for kk in range(TOP_K):
        rank_k = jnp.sum(jnp.where(sels[kk], rank, 0.0), axis=-1, keepdims=True)
        packed = jnp.where(out_lane == kk, idxs[kk], packed)
        packed = jnp.where(out_lane == TOP_K + kk, rank_k, packed)
        packed = jnp.where(out_lane == 2 * TOP_K + kk, exps[kk] * inv_den, packed)
    route_ref[...] = packed


def _out0_kernel(attn_ref, bg_ref, p_ref, pprev_ref, pnext_ref, cw_ref, wo_ref,
                 ctx_ref, x_ref, mod_ref, g_ref, wr_ref, br_ref,
                 x1_ref, h2_ref, route_ref, cnt_ref, carry_ref, *, ntb):
    i = pl.program_id(0)
    seg = i % ntb

    @pl.when(i == 0)
    def _():
        carry_ref[...] = jnp.zeros_like(carry_ref)

    p = p_ref[...].astype(F32)
    row = lax.broadcasted_iota(I32, p.shape, 0)
    has_prev = seg > 1
    has_next = jnp.logical_and(seg > 0, seg < ntb - 1)
    prev_row = jnp.where(has_prev, pprev_ref[SUBLANES - 1:SUBLANES, :].astype(F32), 0.0)
    next_row = jnp.where(has_next, pnext_ref[0:1, :].astype(F32), 0.0)
    before = jnp.where(row == 0, prev_row, pltpu.roll(p, 1, 0))
    after = jnp.where(row == TM - 1, next_row, pltpu.roll(p, TM - 1, 0))
    cw = cw_ref[...]
    conv = bg_ref[...].astype(F32) * (cw[0:1] * before + cw[1:2] * p + cw[2:3] * after)
    aw = attn_ref.shape[1]
    y = _dot(attn_ref[...], wo_ref[:aw, :]) + _dot(conv.astype(BF16), wo_ref[aw:, :])
    _route_tail(y, _stream_tile(ctx_ref, x_ref, ntb), mod_ref, g_ref, wr_ref, br_ref,
                x1_ref, h2_ref, route_ref, cnt_ref, carry_ref)


def _out1_kernel(hs_ref, gg_ref, wo_ref, x_ref, mod_ref, g_ref, wr_ref, br_ref,
                 x1_ref, h2_ref, route_ref, cnt_ref, carry_ref, *, ntb):
    i = pl.program_id(0)

    @pl.when(i == 0)
    def _():
        carry_ref[...] = jnp.zeros_like(carry_ref)

    @pl.when(i % ntb > 0)
    def _():
        rec = hs_ref[0].astype(F32) + hs_ref[1].astype(F32)
        y = _dot((rec * gg_ref[...].astype(F32)).astype(BF16), wo_ref[...])
        _route_tail(y, x_ref[...], mod_ref, g_ref, wr_ref, br_ref, x1_ref, h2_ref, route_ref, cnt_ref, carry_ref)


def _tail_specs(d, n_exp, ntb, latent_only):
    row = lambda i: (i, 0)
    const = lambda i: (0, 0)
    out_row = (lambda i: ((i // ntb) * (ntb - 1) + jnp.maximum(i % ntb - 1, 0), 0)) if latent_only else row
    in_specs = [
        pl.BlockSpec((TM, d), row),
        pl.BlockSpec((1, 6, d), lambda i: ((i // ntb) * 2 + jnp.minimum(i % ntb, 1), 0, 0)),
        pl.BlockSpec((1, d), const),
        pl.BlockSpec((d, n_exp), const),
        pl.BlockSpec((1, n_exp), const),
    ]
    out_specs = [
        pl.BlockSpec((TM, d), out_row),
        pl.BlockSpec((TM * (d // LANES), LANES), out_row),
        pl.BlockSpec((TM, LANES), out_row),
        pl.BlockSpec((1, n_exp), const),
    ]
    return in_specs, out_specs


def _tail_shapes(t, d, n_exp):
    return [jax.ShapeDtypeStruct((t, d), F32), jax.ShapeDtypeStruct((t * (d // LANES), LANES), F32),
            jax.ShapeDtypeStruct((t, LANES), F32), jax.ShapeDtypeStruct((1, n_exp), F32)]


def _out_proj0(attn, bg, p, conv_w, w_out, ctx, x, modv, g, w_r, b_r, ntb):
    d = x.shape[1]
    t = ctx.shape[0] + x.shape[0]
    n_exp = w_r.shape[1]
    bw = bg.shape[1]
    row = lambda i: (i, 0)
    const = lambda i: (0, 0)
    nblk = TM // SUBLANES
    tail_in, tail_out = _tail_specs(d, n_exp, ntb, False)
    return pl.pallas_call(
        functools.partial(_out0_kernel, ntb=ntb),
        grid=(t // TM,),
        in_specs=[
            pl.BlockSpec((TM, attn.shape[1]), row),
            pl.BlockSpec((TM, bw), row),
            pl.BlockSpec((TM, bw), row),
            pl.BlockSpec((SUBLANES, bw), lambda i: (jnp.maximum(i * nblk - 1, 0), 0)),
            pl.BlockSpec((SUBLANES, bw), lambda i: (jnp.minimum((i + 1) * nblk, t // SUBLANES - 1), 0)),
            pl.BlockSpec(conv_w.shape, const),
            pl.BlockSpec(w_out.shape, const),
        ] + _stream_specs(d, ntb) + tail_in[1:],
        out_specs=tail_out,
        out_shape=_tail_shapes(t, d, n_exp),
        scratch_shapes=[pltpu.VMEM((1, n_exp), F32)],
        compiler_params=_cparams(("arbitrary",)),
        name="l0_out_proj_router",
    )(attn, bg, p, p, p, conv_w, w_out, ctx, x, modv, g, w_r, b_r)


def _out_proj1(hs, gg, w_out, x, modv, g, w_r, b_r, ntb):
    t, d = x.shape
    n_exp = w_r.shape[1]
    row = lambda i: (i, 0)
    const = lambda i: (0, 0)
    tail_in, tail_out = _tail_specs(d, n_exp, ntb, True)
    t_lat = t // ntb * (ntb - 1)
    return pl.pallas_call(
        functools.partial(_out1_kernel, ntb=ntb),
        grid=(t // TM,),
        in_specs=[
            pl.BlockSpec((2, TM, hs.shape[2]), lambda i: (0, i, 0)),
            pl.BlockSpec((TM, gg.shape[1]), row),
            pl.BlockSpec(w_out.shape, const),
        ] + tail_in,
        out_specs=tail_out,
        out_shape=_tail_shapes(t_lat, d, n_exp),
        scratch_shapes=[pltpu.VMEM((1, n_exp), F32)],
        compiler_params=_cparams(("arbitrary",)),
        name="l1_out_proj_router",
    )(hs, gg, w_out, x, modv, g, w_r, b_r)


PAIRS_PER_TILE = TM * TOP_K


def _stage_pos(pos_ref, pbuf, psem):
    i = pl.program_id(0)
    slot = i % 2

    def chunk(ti, s):
        return pltpu.make_async_copy(pos_ref.at[pl.ds(ti * PAIRS_PER_TILE, PAIRS_PER_TILE)],
                                     pbuf.at[pl.ds(s * PAIRS_PER_TILE, PAIRS_PER_TILE)], psem.at[s])

    @pl.when(i == 0)
    def _():
        chunk(0, 0).start()

    chunk(i, slot).wait()

    @pl.when(i + 1 < pl.num_programs(0))
    def _():
        chunk(i + 1, 1 - slot).start()

    return slot * PAIRS_PER_TILE


def _dispatch_kernel(pad_lo_ref, pad_hi_ref, pos_ref, h2_ref, xs_ref, pbuf, zbuf, psem, dsem, zsem):
    i = pl.program_id(0)
    nch = h2_ref.shape[0] // TM
    n_tails = pad_lo_ref.shape[0] - 1

    def zero_slots(first, n):
        return pltpu.make_async_copy(zbuf.at[pl.ds(0, n * nch)],
                                     xs_ref.at[pl.ds(pl.multiple_of(first * nch, nch), n * nch)], zsem)

    def over_padding(act):
        def tail(e, carry):
            lo = pad_lo_ref[e]
            n = pad_hi_ref[e] - lo
            for bit in range(TME.bit_length() - 1):
                size = 1 << bit

                @pl.when((n & size) != 0)
                def _():
                    act(zero_slots(lo + (n & -(2 * size)), size))
            return carry

        lax.fori_loop(0, n_tails, tail, 0)

        def tile(j, c):
            act(zero_slots(j * TME, TME))
            return c

        lax.fori_loop(pad_lo_ref[n_tails] // TME, pad_hi_ref[n_tails] // TME, tile, 0)

    @pl.when(i == 0)
    def _():
        zbuf[...] = jnp.zeros(zbuf.shape, zbuf.dtype)
        over_padding(lambda cp: cp.start())
        over_padding(lambda cp: cp.wait())

    off = _stage_pos(pos_ref, pbuf, psem)
    for r in range(TM):
        for kk in range(TOP_K):
            dst = pl.multiple_of(pbuf[off + r * TOP_K + kk] * nch, nch)
            pltpu.make_async_copy(h2_ref.at[pl.ds(r * nch, nch)], xs_ref.at[pl.ds(dst, nch)], dsem).start(
                priority=kk % 2)
    for _ in range(TOP_K):
        pltpu.make_async_copy(h2_ref, xs_ref.at[pl.ds(0, TM * nch)], dsem).wait()


def _dispatch(pos, pad_lo, pad_hi, h2, n_slots, d):
    nch = d // LANES
    n_tok = h2.shape[0] // nch
    grid_spec = pltpu.PrefetchScalarGridSpec(
        num_scalar_prefetch=2,
        grid=(n_tok // TM,),
        in_specs=[pl.BlockSpec(memory_space=pl.ANY),
                  pl.BlockSpec((TM * nch, LANES), lambda i, lo, hi: (i, 0))],
        out_specs=pl.BlockSpec(memory_space=pl.ANY),
        scratch_shapes=[pltpu.SMEM((2 * PAIRS_PER_TILE,), I32), pltpu.VMEM((TME * nch, LANES), F32),
                        pltpu.SemaphoreType.DMA((2,)), pltpu.SemaphoreType.DMA, pltpu.SemaphoreType.DMA],
    )
    return pl.pallas_call(
        _dispatch_kernel,
        grid_spec=grid_spec,
        out_shape=jax.ShapeDtypeStruct((n_slots * nch, LANES), F32),
        compiler_params=_cparams(("arbitrary",)),
        name="expert_dispatch",
    )(pad_lo, pad_hi, pos, h2)


def _moe_kernel(te_ref, nx_ref, par_ref, tv_ref, meta_ref, x_ref, w1_hbm, b1_ref, w2_hbm, b2_ref, y_ref,
                w1f, w2f, w1b, w2b, wsem, *, layer):
    i = pl.program_id(0)
    n_tiles = meta_ref[0]
    d = w1b.shape[0]
    ff = w2b.shape[0]
    nch = d // LANES

    def weight_copies(e, s):
        return (pltpu.make_async_copy(w1_hbm.at[layer, e], w1f.at[s], wsem.at[0, s]),
                pltpu.make_async_copy(w2_hbm.at[layer, e], w2f.at[s], wsem.at[1, s]))

    @pl.when(i == 0)
    def _():
        for cp in weight_copies(te_ref[0], par_ref[0]):
            cp.start()

    @pl.when(i < n_tiles)
    def _():
        new_expert = jnp.logical_or(i == 0, te_ref[i] != te_ref[jnp.maximum(i - 1, 0)])

        @pl.when(new_expert)
        def _():
            s = par_ref[i]
            for cp in weight_copies(te_ref[i], s):
                cp.wait()
            w1b[...] = w1f[s].astype(BF16)
            w2b[...] = w2f[s].astype(BF16)

            @pl.when(nx_ref[i] >= 0)
            def _():
                for cp in weight_copies(nx_ref[i], 1 - s):
                    cp.start()

        half = TME // 2

        def ffn_rows(r0):
            x = jnp.concatenate([x_ref[pl.ds(r0 * nch + c, half, stride=nch), :] for c in range(nch)], axis=1)
            h = _dot(x.astype(BF16), w1b[...]) + b1_ref[0]
            glu = jnp.minimum(h[:, :ff], SWIGLU_LIMIT)
            lin = jnp.clip(h[:, ff:], -SWIGLU_LIMIT, SWIGLU_LIMIT)
            act = glu * _sigmoid(SWIGLU_ALPHA * glu) * (lin + 1.0)
            y = _dot(act.astype(BF16), w2b[...]) + b2_ref[0]
            for c in range(nch):
                y_ref[pl.ds(r0 * nch + c, half, stride=nch), :] = y[:, c * LANES:(c + 1) * LANES]

        ffn_rows(0)

        @pl.when(tv_ref[i] > half)
        def _():
            ffn_rows(half)

        @pl.when(tv_ref[i] <= half)
        def _():
            y_ref[pl.ds(half * nch, half * nch), :] = jnp.zeros((half * nch, LANES), y_ref.dtype)

    @pl.when(i >= n_tiles)
    def _():
        y_ref[...] = jnp.zeros(y_ref.shape, y_ref.dtype)


def _routed_experts(tile_expert, next_expert, parity, tile_valid, meta, xs, w1, b1, w2, b2, layer):
    _, n_exp, d, ff2 = w1.shape
    ff = w2.shape[2]
    nch = d // LANES
    n_grid = tile_expert.shape[0]
    used = lambda i, mt: jnp.minimum(i, mt[0] - 1)
    bmap = lambda i, te, nx, pr, tv, mt: (layer * n_exp + te[used(i, mt)], 0, 0)
    grid_spec = pltpu.PrefetchScalarGridSpec(
        num_scalar_prefetch=5,
        grid=(n_grid,),
        in_specs=[
            pl.BlockSpec((TME * nch, LANES), lambda i, te, nx, pr, tv, mt: (used(i, mt), 0)),
            pl.BlockSpec(memory_space=pl.ANY),
            pl.BlockSpec((1, 1, ff2), bmap),
            pl.BlockSpec(memory_space=pl.ANY),
            pl.BlockSpec((1, 1, d), bmap),
        ],
        out_specs=pl.BlockSpec((TME * nch, LANES), lambda i, te, nx, pr, tv, mt: (i, 0)),
        scratch_shapes=[pltpu.VMEM((2, d, ff2), F32), pltpu.VMEM((2, ff, d), F32),
                        pltpu.VMEM((d, ff2), BF16), pltpu.VMEM((ff, d), BF16), pltpu.SemaphoreType.DMA((2, 2))],
    )
    return pl.pallas_call(
        functools.partial(_moe_kernel, layer=layer),
        grid_spec=grid_spec,
        out_shape=jax.ShapeDtypeStruct(xs.shape, F32),
        compiler_params=_cparams(("arbitrary",)),
        name="routed_experts",
    )(tile_expert, next_expert, parity, tile_valid, meta, xs, w1, b1.reshape(-1, 1, ff2), w2, b2.reshape(-1, 1, d))


def _gather_choices(pos_ref, ys_ref, ybuf, pbuf, psem, csem):
    i = pl.program_id(0)
    n = pl.num_programs(0)
    slot = i % 2
    nch = ybuf.shape[2] // TM

    def chunk(ti, s):
        return pltpu.make_async_copy(pos_ref.at[pl.ds(ti * PAIRS_PER_TILE, PAIRS_PER_TILE)],
                                     pbuf.at[pl.ds(s * PAIRS_PER_TILE, PAIRS_PER_TILE)], psem.at[s])

    def row_copy(s, r, kk):
        src = pl.multiple_of(pbuf[s * PAIRS_PER_TILE + r * TOP_K + kk] * nch, nch)
        return pltpu.make_async_copy(ys_ref.at[pl.ds(src, nch)], ybuf.at[s, kk, pl.ds(r * nch, nch)], csem.at[s])

    @pl.when(i == 0)
    def _():
        chunk(0, 0).start()
        chunk(0, 0).wait()

        def first_rows(r, c):
            for kk in range(TOP_K):
                row_copy(0, r, kk).start(priority=kk % 2)
            return c

        lax.fori_loop(0, TM, first_rows, 0)

        @pl.when(n > 1)
        def _():
            chunk(1, 1).start()

    @pl.when(i + 1 < n)
    def _():
        chunk(i + 1, 1 - slot).wait()
        for r in range(TM):
            for kk in range(TOP_K):
                row_copy(1 - slot, r, kk).start(priority=kk % 2)

        @pl.when(i + 2 < n)
        def _():
            chunk(i + 2, slot).start()

    for kk in range(TOP_K):
        pltpu.make_async_copy(ys_ref.at[pl.ds(0, TM * nch)], ybuf.at[slot, kk], csem.at[slot]).wait()
    return slot


def _gather_scratch(d):
    nch = d // LANES
    return [pltpu.VMEM((2, TOP_K, TM * nch, LANES), F32), pltpu.SMEM((2 * PAIRS_PER_TILE,), I32),
            pltpu.SemaphoreType.DMA((2,)), pltpu.SemaphoreType.DMA((2,))]


def _plan_routes(route, counts, n_grid):
    n_exp = counts.shape[-1]
    cnt = counts.reshape(n_exp).astype(I32)
    tiles_per = (cnt + TME - 1) // TME
    tile_end = jnp.cumsum(tiles_per)
    offset = (tile_end - tiles_per) * TME
    n_tiles = tile_end[-1]
    tile_ids = jnp.minimum(jnp.arange(n_grid, dtype=I32), n_tiles - 1)
    tile_expert = jnp.sum((tile_ids[:, None] >= tile_end[None, :]).astype(I32), axis=1)
    expert_ids = jnp.arange(n_exp, dtype=I32)
    of_tile = tile_expert[:, None] == expert_ids
    later = jnp.logical_and(expert_ids[None, :] > expert_ids[:, None], (tiles_per > 0)[None, :])
    next_of = jnp.min(jnp.where(later, expert_ids[None, :], n_exp), axis=1)
    next_of = jnp.where(next_of == n_exp, -1, next_of)
    next_expert = jnp.sum(jnp.where(of_tile, next_of, 0), axis=1)
    parity = jnp.sum(jnp.where(of_tile, jnp.cumsum((tiles_per > 0).astype(I32)), 0), axis=1) % 2
    ahead = (tile_ids - jnp.sum(jnp.where(of_tile, tile_end - tiles_per, 0), axis=1)) * TME
    tile_valid = jnp.clip(jnp.sum(jnp.where(of_tile, cnt, 0), axis=1) - ahead, 1, TME)
    eidx = route[:, :TOP_K].astype(I32)
    rank = route[:, TOP_K:2 * TOP_K].astype(I32)
    pos = jnp.sum(jnp.where(eidx[..., None] == expert_ids, offset, 0), axis=-1) + rank
    pad_lo = jnp.concatenate([offset + cnt, (n_tiles * TME).reshape(1)])
    pad_hi = jnp.concatenate([tile_end * TME, jnp.full((1,), n_grid * TME, I32)])
    return (tile_expert, next_expert, parity, tile_valid, n_tiles.reshape(1)), pos.reshape(-1), pad_lo, pad_hi


def _moe_layer(route, counts, h2, w1, b1, w2, b2, layer):
    n_exp, d = w1.shape[1], w1.shape[2]
    n_tok = route.shape[0]
    n_grid = n_tok * TOP_K // TME + n_exp
    tiles, pos, pad_lo, pad_hi = _plan_routes(route, counts, n_grid)
    xs = _dispatch(pos, pad_lo, pad_hi, h2, n_grid * TME, d)
    return pos, _routed_experts(*tiles, xs, w1, b1, w2, b2, layer)


def _combine(ybuf, slot, route_ref, d):
    nch = d // LANES
    gates = route_ref[...]
    gk = [jnp.broadcast_to(gates[:, 2 * TOP_K + kk:2 * TOP_K + kk + 1], (TM, LANES)) for kk in range(TOP_K)]
    chunks = []
    for c in range(nch):
        acc = gk[0] * ybuf[slot, 0, pl.ds(c, TM, stride=nch), :]
        for kk in range(1, TOP_K):
            acc = acc + gk[kk] * ybuf[slot, kk, pl.ds(c, TM, stride=nch), :]
        chunks.append(acc)
    return jnp.concatenate(chunks, axis=1)


def _in1_kernel(x_ref, pos_ref, ys_ref, route_ref, mod0_ref, mod1_ref, g_ref, w_ref, x2_ref, gg_ref, u_ref,
                ybuf, pbuf, psem, csem):
    slot = _gather_choices(pos_ref, ys_ref, ybuf, pbuf, psem, csem)
    x2 = x_ref[...] + mod0_ref[0][5:6] * _combine(ybuf, slot, route_ref, x_ref.shape[1])
    x2_ref[...] = x2
    mod1 = mod1_ref[0]
    h = _norm_mod(x2, g_ref[...], mod1[0:1], mod1[1:2])
    y = _dot(h.astype(BF16), w_ref[...])
    half = y.shape[1] // 2
    gg_ref[...] = jax.nn.gelu(y[:, :half]).astype(BF16)
    u_ref[...] = y[:, half:]


def _in_proj1(x, pos, ys, route, modv0, modv1, g, w, ntb):
    t, d = x.shape
    n = w.shape[1]
    row = lambda i: (i, 0)
    mod_map = lambda i: ((i // ntb) * 2 + jnp.minimum(i % ntb, 1), 0, 0)
    return pl.pallas_call(
        _in1_kernel,
        grid=(t // TM,),
        in_specs=[
            pl.BlockSpec((TM, d), row),
            pl.BlockSpec(memory_space=pl.ANY),
            pl.BlockSpec(memory_space=pl.ANY),
            pl.BlockSpec((TM, LANES), row),
            pl.BlockSpec((1, 6, d), mod_map),
            pl.BlockSpec((1, 6, d), mod_map),
            pl.BlockSpec((1, d), lambda i: (0, 0)),
            pl.BlockSpec((d, n), lambda i: (0, 0)),
        ],
        out_specs=[pl.BlockSpec((TM, d), row), pl.BlockSpec((TM, n // 2), row), pl.BlockSpec((TM, n // 2), row)],
        out_shape=[jax.ShapeDtypeStruct((t, d), F32), jax.ShapeDtypeStruct((t, n // 2), BF16),
                   jax.ShapeDtypeStruct((t, n // 2), F32)],
        scratch_shapes=_gather_scratch(d),
        compiler_params=_cparams(("arbitrary",)),
        name="l1_combine_in_proj",
    )(x, pos, ys, route, modv0, modv1, g, w)


def _rglru_kernel(u_ref, halo_ref, cw_ref, cb_ref, gaw_ref, gab_ref, gxw_ref, gxb_ref, lam_ref, o_ref,
                  ext, a_s, b_s, h_s, state, *, ntb):
    d = pl.program_id(0)
    s = pl.program_id(2)
    nb, tc, cw_ = u_ref.shape
    u = u_ref[...]
    cw = cw_ref[0]

    def finish(xc, reverse):
        xc2 = xc.reshape(nb * tc, cw_) + cb_ref[0]
        xb = xc2.astype(BF16)
        t_r = jnp.tanh(_dot(xb, gaw_ref[0, 0]) + gab_ref[0])
        t_i = jnp.tanh(_dot(xb, gxw_ref[0, 0]) + gxb_ref[0])
        nl = -lam_ref[0]
        softplus = jnp.maximum(nl, 0.0) + jnp.log1p(jnp.exp(-jnp.abs(nl)))
        half_rate = (-0.5 * LRU_C) * softplus
        log_a = half_rate * t_r + half_rate
        a = jnp.exp(log_a)
        half_x = 0.5 * xc2
        bb = jnp.sqrt(1.0 - a * a) * (half_x * t_i + half_x)
        n_lane = cw_ // LANES
        pitch = tc + SCAN_PAD
        for c in range(n_lane):
            for bi in range(nb):
                a_s[c, bi * pitch:bi * pitch + tc, :] = a[bi * tc:(bi + 1) * tc, c * LANES:(c + 1) * LANES]
                b_s[c, bi * pitch:bi * pitch + tc, :] = bb[bi * tc:(bi + 1) * tc, c * LANES:(c + 1) * LANES]

        @pl.when(s == 0)
        def _():
            state[...] = jnp.zeros_like(state)

        def steps(tb, hs):
            hs = list(hs)
            for uu in range(SCAN_UNROLL):
                tt = tb * SCAN_UNROLL + uu
                t = (tc - 1 - tt) if reverse else tt
                rows = pl.ds(t, nb, stride=pitch)
                for c in range(n_lane):
                    hs[c] = a_s[c, rows, :] * hs[c] + b_s[c, rows, :]
                    h_s[c, rows, :] = hs[c]
            return tuple(hs)

        hs = lax.fori_loop(0, tc // SCAN_UNROLL, steps, tuple(state[c] for c in range(n_lane)))
        for c in range(n_lane):
            state[c] = hs[c]
            for bi in range(nb):
                o_ref[0, bi, :, c * LANES:(c + 1) * LANES] = h_s[c, bi * pitch:bi * pitch + tc, :].astype(o_ref.dtype)

    @pl.when(d == 0)
    def _():
        chunk = s
        keep = chunk > 1
        ext[:, 0:SUBLANES, :] = jnp.where(keep, halo_ref[...], 0.0)
        ext[:, SUBLANES:, :] = u
        xc = cw[3:4] * u
        for j in range(1, C_CONV):
            xc = xc + cw[3 - j:4 - j] * ext[:, SUBLANES - j:SUBLANES - j + tc, :]
        finish(xc, False)

    @pl.when(d == 1)
    def _():
        chunk = jnp.where(s == 0, 0, ntb - s)
        keep = jnp.logical_and(chunk > 0, chunk < ntb - 1)
        ext[:, 0:tc, :] = u
        ext[:, tc:, :] = jnp.where(keep, halo_ref[...], 0.0)
        xc = cw[0:1] * u
        for j in range(1, C_CONV):
            xc = xc + cw[j:j + 1] * ext[:, j:j + tc, :]
        finish(xc, True)


def _rglru(u, conv_w, conv_b, ga_w, ga_b, gx_w, gx_b, lam, b, l):
    width = u.shape[1]
    cb = width // C_BLOCKS
    ntb = l // TM
    u3 = u.reshape(b, l, width)
    nblk = TM // SUBLANES

    def chunk_of(d, s):
        return jnp.where(d == 0, s, jnp.where(s == 0, 0, ntb - s))

    def halo_of(d, s):
        c = chunk_of(d, s)
        return jnp.where(d == 0, jnp.maximum(c * nblk - 1, 0), jnp.minimum((c + 1) * nblk, l // SUBLANES - 1))

    vec = pl.BlockSpec((1, 1, cb), lambda d, g, s: (d, 0, g))
    mat = pl.BlockSpec((1, 1, cb, cb), lambda d, g, s: (d, g, 0, 0))
    return pl.pallas_call(
        functools.partial(_rglru_kernel, ntb=ntb),
        grid=(2, C_BLOCKS, ntb),
        in_specs=[
            pl.BlockSpec((b, TM, cb), lambda d, g, s: (0, chunk_of(d, s), g)),
            pl.BlockSpec((b, SUBLANES, cb), lambda d, g, s: (0, halo_of(d, s), g)),
            pl.BlockSpec((1, C_CONV, cb), lambda d, g, s: (d, 0, g)),
            vec, mat, vec, mat, vec, vec,
        ],
        out_specs=pl.BlockSpec((1, b, TM, cb), lambda d, g, s: (d, 0, chunk_of(d, s), g)),
        out_shape=jax.ShapeDtypeStruct((2, b, l, width), BF16),
        scratch_shapes=[
            pltpu.VMEM((b, TM + SUBLANES, cb), F32),
            pltpu.VMEM((cb // LANES, b * (TM + SCAN_PAD), LANES), F32),
            pltpu.VMEM((cb // LANES, b * (TM + SCAN_PAD), LANES), F32),
            pltpu.VMEM((cb // LANES, b * (TM + SCAN_PAD), LANES), F32),
            pltpu.VMEM((cb // LANES, b, LANES), F32),
        ],
        compiler_params=_cparams(("arbitrary", "arbitrary", "arbitrary")),
        name="l1_rglru",
    )(u3, u3, conv_w, conv_b.reshape(2, 1, width), (0.5 * ga_w).astype(BF16), (0.5 * ga_b).reshape(2, 1, width),
      (0.5 * gx_w).astype(BF16), (0.5 * gx_b).reshape(2, 1, width), lam.reshape(2, 1, width))


def _final_kernel(x_ref, pos_ref, ys_ref, route_ref, mod_ref, g_ref, o_ref, ybuf, pbuf, psem, csem):
    slot = _gather_choices(pos_ref, ys_ref, ybuf, pbuf, psem, csem)
    x = x_ref[...] + mod_ref[0][5:6] * _combine(ybuf, slot, route_ref, x_ref.shape[1])
    ms = jnp.mean(x * x, axis=-1, keepdims=True)
    o_ref[...] = x * lax.rsqrt(ms + NORM_EPS) * g_ref[...]


def _final(x, pos, ys, route, modv, g, b, s_len):
    t, d = x.shape
    per_b = s_len // TM
    row = lambda i: (i, 0)
    return pl.pallas_call(
        _final_kernel,
        grid=(t // TM,),
        in_specs=[
            pl.BlockSpec((TM, d), row),
            pl.BlockSpec(memory_space=pl.ANY),
            pl.BlockSpec(memory_space=pl.ANY),
            pl.BlockSpec((TM, LANES), row),
            pl.BlockSpec((1, 6, d), lambda i: ((i // per_b) * 2 + 1, 0, 0)),
            pl.BlockSpec((1, d), lambda i: (0, 0)),
        ],
        out_specs=pl.BlockSpec((TM, d), row),
        out_shape=jax.ShapeDtypeStruct((t, d), F32),
        scratch_shapes=_gather_scratch(d),
        compiler_params=_cparams(("arbitrary",)),
        name="final_combine_norm",
    )(x, pos, ys, route, modv, g)


def _rope_tables(s_len, n_ctx):
    n_rows = s_len // GRID_W
    rows, cols = jnp.meshgrid(jnp.arange(n_rows), jnp.arange(GRID_W), indexing="ij")
    pos = jnp.stack([rows.reshape(-1), cols.reshape(-1)], axis=-1).astype(F32)
    n_freq = HEAD_DIM // 4
    inv = ROPE_THETA ** (-jnp.arange(n_freq, dtype=F32) / n_freq)
    ang = pos[:, :, None] * inv
    cos, sin = jnp.cos(ang), jnp.sin(ang)
    cos64 = jnp.stack([cos, cos], axis=2).reshape(s_len, HEAD_DIM)
    sin64 = jnp.stack([-sin, sin], axis=2).reshape(s_len, HEAD_DIM)
    cos_l = jnp.tile(cos64, (1, LANES // HEAD_DIM))
    sin_l = jnp.tile(sin64, (1, LANES // HEAD_DIM))
    cos_t = jnp.concatenate([jnp.ones((n_ctx, LANES), F32), cos_l], axis=0)
    sin_t = jnp.concatenate([jnp.zeros((n_ctx, LANES), F32), sin_l], axis=0)
    return cos_t, sin_t


def kernel(x, c, ctx, c_ctx, w_mod, b_mod, norm_mix, norm_ffn, ev_w_in, ev_w_out, ev_lambda_q1, ev_lambda_k1, ev_lambda_q2, ev_lambda_k2, ev_subln, ev_conv_w, od_w_in, od_w_out, od_conv_w, od_conv_b, od_gate_a_w, od_gate_a_b, od_gate_x_w, od_gate_x_b, od_lru_lambda, moe_w_router, moe_b_router, moe_w1, moe_b1, moe_w2, moe_b2, final_norm):
    b, s_len, d = x.shape
    n_ctx = ctx.shape[1]
    l = n_ctx + s_len
    t = b * l
    ntb = l // TM
    assert n_ctx == TM and s_len % TM == 0 and w_mod.shape[0] == 2

    ctx2 = ctx.reshape(b * n_ctx, d)
    x2d = x.reshape(b * s_len, d)

    n_rows = -(-(b + 1) // SUBLANES) * SUBLANES
    cs = jnp.concatenate([c, c_ctx[None, :], jnp.zeros((n_rows - b - 1, d), F32)], axis=0)
    mod = _modulation(cs, w_mod, b_mod)

    def mod_table(i):
        lat = mod[i, :b]
        cx = jnp.broadcast_to(mod[i, b][None, :], lat.shape)
        return jnp.stack([cx, lat], axis=1).reshape(b * 2, 6, d)

    modv0, modv1 = mod_table(0), mod_table(1)

    cos_t, sin_t = _rope_tables(s_len, n_ctx)
    q, k, v, bg, p = _in_proj0(ctx2, x2d, modv0, norm_mix[0:1], ev_w_in[0].astype(BF16), cos_t, sin_t, ntb)
    lam_init = 0.8 - 0.6 * math.exp(-0.3 * 0)
    attn = _diff_attention(q, k, v, ev_lambda_q1[0:1], ev_lambda_k1[0:1], ev_lambda_q2[0:1], ev_lambda_k2[0:1],
                           ev_subln[0:1], lam_init, b, l, n_ctx)
    x1, h2, route0, counts0 = _out_proj0(attn, bg, p, ev_conv_w[0], ev_w_out[0].astype(BF16), ctx2, x2d, modv0,
                                         norm_ffn[0:1], moe_w_router[0], moe_b_router[0:1], ntb)
    pos0, ys0 = _moe_layer(route0, counts0, h2, moe_w1, moe_b1, moe_w2, moe_b2, 0)

    x2, gg, u = _in_proj1(x1, pos0, ys0, route0, modv0, modv1, norm_mix[1:2], od_w_in[0].astype(BF16), ntb)
    hs = _rglru(u, od_conv_w[0], od_conv_b[0], od_gate_a_w[0], od_gate_a_b[0], od_gate_x_w[0], od_gate_x_b[0],
                od_lru_lambda[0], b, l)
    x3, h3, route1, counts1 = _out_proj1(hs.reshape(2, t, hs.shape[-1]), gg, od_w_out[0].astype(BF16), x2, modv1,
                                         norm_ffn[1:2], moe_w_router[1], moe_b_router[1:2], ntb)
    pos1, ys1 = _moe_layer(route1, counts1, h3, moe_w1, moe_b1, moe_w2, moe_b2, 1)

    out = _final(x3, pos1, ys1, route1, modv1, final_norm[None, :], b, s_len)
    return out.reshape(b, s_len, d)
```
